```python
import math
import jax, jax.numpy as jnp
from jax import lax
import numpy as np

D_MODEL = 1024
BATCH = 2
SEQ = 8192
DEPTH = 2
DEC_BATCH = 128
DEC_SEQ = 4
PAST_LEN = 2048
PAGE_SIZE = 128

MIX_W = D_MODEL
ML_W = MIX_W // 2
ML_HEADS = 4
ML_HD = ML_W // ML_HEADS
ML_CHUNK = 64
NSA_W = MIX_W - ML_W
NSA_HEADS = 8
NSA_HD = NSA_W // NSA_HEADS
KV_HEADS = 2
Q_PER_KV = NSA_HEADS // KV_HEADS
KV_W = KV_HEADS * NSA_HD
CMP_BLOCK = 32
CMP_STRIDE = 16
CMP_HIDDEN = 256
SEL_BLOCK = 64
N_SELECT = 16
WINDOW = 512
Q_BLOCK = 128
NUM_BUCKETS = 32
REL_MAX_DIST = 128
D_FF = 4 * D_MODEL
ALPHA = (2 * DEPTH) ** 0.25
BETA = (8 * DEPTH) ** -0.25
LN_EPS = 1e-5
NEG = -1e30
FORCE = 1e9
IN_SIZES = (ML_W, ML_W, ML_W, ML_W, ML_HEADS, ML_HEADS, NSA_W, KV_W, KV_W, KV_W, KV_W, KV_W, KV_W, 3 * NSA_HEADS)
N_IN = sum(IN_SIZES)

kernel_name = 'hymba_mlstm_nsa_deepnorm_step'


def layer_norm(x, g, b):
    xf = x.astype(jnp.float32)
    mu = xf.mean(-1, keepdims=True)
    var = jnp.square(xf - mu).mean(-1, keepdims=True)
    return ((xf - mu) * lax.rsqrt(var + LN_EPS) * g + b).astype(x.dtype)


def head_norm(h, g):
    mu = h.mean(-1, keepdims=True)
    var = jnp.square(h - mu).mean(-1, keepdims=True)
    return (h - mu) * lax.rsqrt(var + LN_EPS) * g.reshape(ML_HEADS, ML_HD).astype(jnp.float32)


def rel_bucket(dist):
    n = jnp.maximum(dist, 0)
    max_exact = NUM_BUCKETS // 2
    nf = jnp.maximum(n, 1).astype(jnp.float32)
    large = max_exact + (jnp.log(nf / max_exact) / math.log(REL_MAX_DIST / max_exact)
                         * (NUM_BUCKETS - max_exact)).astype(jnp.int32)
    large = jnp.minimum(large, NUM_BUCKETS - 1)
    return jnp.where(n < max_exact, n, large)


def masked_softmax(s, mask):
    return jax.nn.softmax(jnp.where(mask, s, NEG), axis=-1) * mask


def mlstm_chunk(carry, inputs):
    C, n, m0 = carry
    q, k, v, ig, lf = inputs
    T = q.shape[1]
    F = jnp.cumsum(lf, axis=1)
    m = F + jnp.maximum(m0[:, None], lax.cummax(ig - F, axis=1))
    causal = (jnp.arange(T)[:, None] >= jnp.arange(T)[None, :])[None, :, :, None]
    log_d = F[:, :, None] - F[:, None, :] + ig[:, None, :] - m[:, :, None]
    dmat = jnp.exp(jnp.where(causal, log_d, NEG))
    w = jnp.einsum('bthd,bshd->btsh', q, k) * dmat
    decay = jnp.exp(F + m0[:, None] - m)
    num = jnp.einsum('btsh,bshd->bthd', w, v) + jnp.einsum('bthk,bhkv->bthv', q, C) * decay[..., None]
    den = w.sum(axis=2) + jnp.einsum('bthk,bhk->bth', q, n) * decay
    h = num / jnp.maximum(jnp.abs(den), jnp.exp(-m))[..., None]
    m_end = m[:, -1]
    f_end = F[:, -1]
    w_end = jnp.exp(f_end[:, None] - F + ig - m_end[:, None])
    carry_decay = jnp.exp(f_end + m0 - m_end)
    C_new = carry_decay[..., None, None] * C + jnp.einsum('bth,bthk,bthv->bhkv', w_end, k, v)
    n_new = carry_decay[..., None] * n + jnp.einsum('bth,bthk->bhk', w_end, k)
    return (C_new, n_new, m_end), h


def mlstm_prompt(q, k, v, ig, lf):
    B, S = q.shape[0], q.shape[1]
    nc = S // ML_CHUNK
    chunks = tuple(a.reshape(B, nc, ML_CHUNK, *a.shape[2:]).swapaxes(0, 1) for a in (q, k, v, ig, lf))
    carry0 = (jnp.zeros((B, ML_HEADS, ML_HD, ML_HD), jnp.float32),
              jnp.zeros((B, ML_HEADS, ML_HD), jnp.float32),
              jnp.zeros((B, ML_HEADS), jnp.float32))
    carry, h = lax.scan(mlstm_chunk, carry0, chunks)
    return h.swapaxes(0, 1).reshape(B, S, ML_HEADS, ML_HD), carry


def mlstm_sample(q, k, v, ig, lf, C, n, m):
    carry = (C.astype(jnp.float32), n.astype(jnp.float32), m.astype(jnp.float32))
    carry, h = mlstm_chunk(carry, (q, k, v, ig, lf))
    return h, carry


def compress_kv(kv, pe, w1, w2):
    L = kv.shape[1]
    n_cmp = (L - CMP_BLOCK) // CMP_STRIDE + 1
    idx = (jnp.arange(n_cmp) * CMP_STRIDE)[:, None] + jnp.arange(CMP_BLOCK)[None, :]
    blocks = kv[:, idx] + pe.transpose(1, 0, 2)[:, :, None, :]
    w1r = w1.reshape(2, CMP_BLOCK, NSA_HD, CMP_HIDDEN)
    hid = jax.nn.gelu(jnp.einsum('bncsgd,scdh->bnsgh', blocks, w1r))
    return jnp.einsum('bnsgh,shd->bnsgd', hid, w2)


def nsa_attend(q, qpos, kc, vc, ks, vs, kw, vw, wpos, gates, rel_bias):
    B, T = q.shape[0], q.shape[1]
    scale = NSA_HD ** -0.5
    tbl = rel_bias.reshape(NUM_BUCKETS, KV_HEADS, Q_PER_KV)
    n_cmp = kc.shape[1]
    c_start = jnp.arange(n_cmp) * CMP_STRIDE
    dist_c = qpos[:, None] - (c_start + CMP_BLOCK - 1)[None, :]
    bias_c = tbl[rel_bucket(dist_c)].transpose(0, 2, 3, 1)[None]
    s_c = jnp.einsum('btgrd,bngd->btgrn', q, kc).astype(jnp.float32) * scale + bias_c
    p_c = masked_softmax(s_c, (dist_c >= 0)[None, :, None, None, :])
    o_c = jnp.einsum('btgrn,bngd->btgrd', p_c.astype(vc.dtype), vc)
    n_sel = ks.shape[1] // SEL_BLOCK
    s_start = jnp.arange(n_sel) * SEL_BLOCK
    cover = ((c_start[:, None] < s_start[None, :] + SEL_BLOCK)
             & (c_start[:, None] + CMP_BLOCK > s_start[None, :])).astype(jnp.float32)
    imp = jnp.einsum('btgrn,nj->btgj', p_c, cover)
    cur = qpos // SEL_BLOCK
    jb = jnp.arange(n_sel)
    forced = (jb[None] == 0) | (jb[None] == cur[:, None]) | (jb[None] == cur[:, None] - 1)
    future = s_start[None, :] > qpos[:, None]
    score = jnp.where(forced[None, :, None, :], FORCE, jnp.where(future[None, :, None, :], -FORCE, imp))
    n_top = min(N_SELECT, n_sel)
    _, top = lax.top_k(score, n_top)
    pos = (top[..., None] * SEL_BLOCK + jnp.arange(SEL_BLOCK)).reshape(B, T, KV_HEADS, n_top * SEL_BLOCK)
    bi = jnp.arange(B)[:, None, None, None]
    gi = jnp.arange(KV_HEADS)[None, None, :, None]
    k_sel = ks[bi, pos, gi]
    v_sel = vs[bi, pos, gi]
    dist_s = qpos[None, :, None, None] - pos
    bias_s = tbl.transpose(1, 0, 2)[gi, rel_bucket(dist_s)]
    s_s = jnp.einsum('btgrd,btgkd->btgrk', q, k_sel).astype(jnp.float32) * scale + jnp.swapaxes(bias_s, -1, -2)
    p_s = masked_softmax(s_s, (dist_s >= 0)[:, :, :, None, :])
    o_s = jnp.einsum('btgrk,btgkd->btgrd', p_s.astype(v_sel.dtype), v_sel)
    dist_w = qpos[:, None] - wpos[None, :]
    valid_w = ((dist_w >= 0) & (dist_w < WINDOW) & (wpos[None, :] >= 0))[None, :, None, None, :]
    bias_w = tbl[rel_bucket(dist_w)].transpose(0, 2, 3, 1)[None]
    s_w = jnp.einsum('btgrd,bwgd->btgrw', q, kw).astype(jnp.float32) * scale + bias_w
    p_w = masked_softmax(s_w, valid_w)
    o_w = jnp.einsum('btgrw,bwgd->btgrd', p_w.astype(vw.dtype), vw)
    return gates[..., 0:1] * o_c + gates[..., 1:2] * o_s + gates[..., 2:3] * o_w


def nsa_prompt(q, kv_c, kv_s, kv_w, gates, pe, w1, w2, rel_bias):
    B, S = q.shape[0], q.shape[1]
    ckv = compress_kv(kv_c, pe, w1, w2)
    kwp = jnp.pad(kv_w, ((0, 0), (WINDOW, 0), (0, 0), (0, 0), (0, 0)))

    def query_block(blk):
        t0 = blk * Q_BLOCK
        qpos = t0 + jnp.arange(Q_BLOCK)
        kw = lax.dynamic_slice_in_dim(kwp, t0, WINDOW + Q_BLOCK, axis=1)
        wpos = t0 - WINDOW + jnp.arange(WINDOW + Q_BLOCK)
        return nsa_attend(lax.dynamic_slice_in_dim(q, t0, Q_BLOCK, axis=1), qpos,
                          ckv[:, :, 0], ckv[:, :, 1], kv_s[:, :, 0], kv_s[:, :, 1],
                          kw[:, :, 0], kw[:, :, 1], wpos,
                          lax.dynamic_slice_in_dim(gates, t0, Q_BLOCK, axis=1), rel_bias)

    o = lax.map(query_block, jnp.arange(S // Q_BLOCK))
    o = jnp.moveaxis(o, 0, 1).reshape(B, S, KV_HEADS, Q_PER_KV, NSA_HD)
    win = min(WINDOW, S)
    return o, (kv_c, kv_s, kv_w[:, S - win:])


def gather_pages(pool, page_table):
    rows = pool[page_table]
    return rows.reshape(page_table.shape[0], page_table.shape[1] * PAGE_SIZE, *pool.shape[2:])


def nsa_sample(q, kv_c, kv_s, kv_w, gates, pool_c, pool_s, win_buf, page_table, pe, w1, w2, rel_bias):
    T = q.shape[1]
    past = page_table.shape[1] * PAGE_SIZE
    full_c = jnp.concatenate([gather_pages(pool_c, page_table), kv_c], axis=1)
    ckv = compress_kv(full_c, pe, w1, w2)
    full_s = jnp.concatenate([gather_pages(pool_s, page_table), kv_s], axis=1)
    pad = -full_s.shape[1] % SEL_BLOCK
    full_s = jnp.pad(full_s, ((0, 0), (0, pad), (0, 0), (0, 0), (0, 0)))
    kw_all = jnp.concatenate([win_buf, kv_w], axis=1)
    wb = win_buf.shape[1]
    wpos = past - wb + jnp.arange(wb + T)
    qpos = past + jnp.arange(T)
    o = nsa_attend(q, qpos, ckv[:, :, 0], ckv[:, :, 1], full_s[:, :, 0], full_s[:, :, 1],
                   kw_all[:, :, 0], kw_all[:, :, 1], wpos, gates, rel_bias)
    return o, (kv_c, kv_s, kw_all[:, -wb:])


def trunk_layer(x, c, w_ada, b_ada, w_in, b_gate, ml_norm_g, w_out, ln_g, ln_b, w_up, w_down, mlstm_fn, nsa_fn):
    B, T = x.shape[0], x.shape[1]
    f32 = jnp.float32
    sh1, sc1, g1, sh2, sc2, g2 = jnp.split((jax.nn.silu(c) @ w_ada + b_ada)[:, None, :], 6, axis=-1)
    u = x * (1 + sc1) + sh1
    split_idx = np.cumsum(IN_SIZES)[:-1].tolist()
    mq, mk, mv, mo, mi, mf, nq, ck, cv, sk, sv, wk, wv, ng = jnp.split(u @ w_in, split_idx, axis=-1)
    heads = (B, T, ML_HEADS, ML_HD)
    q = mq.reshape(heads).astype(f32)
    k = mk.reshape(heads).astype(f32) * ML_HD ** -0.5
    v = mv.reshape(heads).astype(f32)
    ig = (mi + b_gate[:ML_HEADS]).astype(f32)
    lf = jax.nn.log_sigmoid((mf + b_gate[ML_HEADS:]).astype(f32))
    h_ml, ml_state = mlstm_fn(q, k, v, ig, lf)

    def kv_pair(a, b):
        return jnp.stack([a.reshape(B, T, KV_HEADS, NSA_HD), b.reshape(B, T, KV_HEADS, NSA_HD)], axis=2)

    nsa_q = nq.reshape(B, T, KV_HEADS, Q_PER_KV, NSA_HD)
    gates = jax.nn.sigmoid(ng).reshape(B, T, KV_HEADS, Q_PER_KV, 3)
    o_nsa, nsa_state = nsa_fn(nsa_q, kv_pair(ck, cv), kv_pair(sk, sv), kv_pair(wk, wv), gates)
    h_ml = (head_norm(h_ml, ml_norm_g) * jax.nn.sigmoid(mo.reshape(heads).astype(f32))).reshape(B, T, ML_W)
    mixed = jnp.concatenate([h_ml.astype(x.dtype), o_nsa.reshape(B, T, NSA_W).astype(x.dtype)], axis=-1) @ w_out
    x = layer_norm(ALPHA * x + g1 * mixed, ln_g[0], ln_b[0])
    u2 = x * (1 + sc2) + sh2
    ff = jnp.square(jax.nn.relu(u2 @ w_up)) @ w_down
    x = layer_norm(ALPHA * x + g2 * ff, ln_g[1], ln_b[1])
    return x, ml_state, nsa_state


def setup_inputs(seed: int = 0) -> dict:
    key = jax.random.key(seed)
    k = jax.random.split(key, 28)
    nrm = jax.random.normal
    f32 = jnp.float32
    D = D_MODEL
    n_pages = PAST_LEN // PAGE_SIZE
    n_used = DEC_BATCH * n_pages
    n_phys = (5 * n_used + 3) // 4
    win_len = min(WINDOW, PAST_LEN)
    page_table = jax.random.permutation(k[8], n_phys)[:n_used].reshape(DEC_BATCH, n_pages).astype(jnp.int32)
    gate_off = jnp.concatenate([jnp.zeros((2 * D,), f32), jnp.ones((D,), f32),
                                jnp.zeros((2 * D,), f32), jnp.ones((D,), f32)])
    forget_bias = jnp.linspace(3.0, 6.0, ML_HEADS, dtype=f32)
    return {
        'x_prompt': nrm(k[0], (BATCH, SEQ, D), f32),
        'x_sample': nrm(k[1], (DEC_BATCH, DEC_SEQ, D), f32),
        'cache_cmp_kv': nrm(k[2], (DEPTH, n_phys, PAGE_SIZE, 2, KV_HEADS, NSA_HD), f32),
        'cache_slc_kv': nrm(k[3], (DEPTH, n_phys, PAGE_SIZE, 2, KV_HEADS, NSA_HD), f32),
        'cache_win_kv': nrm(k[4], (DEPTH, DEC_BATCH, win_len, 2, KV_HEADS, NSA_HD), f32),
        'state_mlstm_C': 0.3 * nrm(k[5], (DEPTH, DEC_BATCH, ML_HEADS, ML_HD, ML_HD), f32),
        'state_mlstm_n': 0.3 * nrm(k[6], (DEPTH, DEC_BATCH, ML_HEADS, ML_HD), f32),
        'state_mlstm_m': jax.random.uniform(k[7], (DEPTH, DEC_BATCH, ML_HEADS), f32, 0.0, 3.0),
        'page_table': page_table,
        'c_prompt': nrm(k[9], (BATCH, D), f32),
        'c_sample': nrm(k[10], (DEC_BATCH, D), f32),
        'w_ada': 0.2 * D ** -0.5 * nrm(k[11], (DEPTH, D, 6 * D), f32),
        'b_ada': gate_off + 0.02 * nrm(k[12], (DEPTH, 6 * D), f32),
        'w_in': D ** -0.5 * nrm(k[13], (DEPTH, D, N_IN), f32),
        'b_gate': jnp.concatenate([0.1 * nrm(k[14], (DEPTH, ML_HEADS), f32),
                                   forget_bias + 0.1 * nrm(k[15], (DEPTH, ML_HEADS), f32)], axis=1),
        'ml_norm_g': 1.0 + 0.02 * nrm(k[16], (DEPTH, ML_W), f32),
        'cmp_pe': 0.1 * nrm(k[17], (DEPTH, 2, CMP_BLOCK, NSA_HD), f32),
        'cmp_w1': (CMP_BLOCK * NSA_HD) ** -0.5 * nrm(k[18], (DEPTH, 2, CMP_BLOCK * NSA_HD, CMP_HIDDEN), f32),
        'cmp_w2': CMP_HIDDEN ** -0.5 * nrm(k[19], (DEPTH, 2, CMP_HIDDEN, NSA_HD), f32),
        'rel_bias': 0.2 * nrm(k[20], (NUM_BUCKETS, NSA_HEADS), f32),
        'w_out': BETA * MIX_W ** -0.5 * nrm(k[21], (DEPTH, MIX_W, D), f32),
        'ln_g': 1.0 + 0.02 * nrm(k[22], (DEPTH, 2, D), f32),
        'ln_b': 0.02 * nrm(k[23], (DEPTH, 2, D), f32),
        'w_up': D ** -0.5 * nrm(k[24], (DEPTH, D, D_FF), f32),
        'w_down': BETA * D_FF ** -0.5 * nrm(k[25], (DEPTH, D_FF, D), f32),
    }


def reference(x_prompt, x_sample, cache_cmp_kv, cache_slc_kv, cache_win_kv, state_mlstm_C, state_mlstm_n,
              state_mlstm_m, page_table, c_prompt, c_sample, w_ada, b_ada, w_in, b_gate, ml_norm_g, cmp_pe,
              cmp_w1, cmp_w2, rel_bias, w_out, ln_g, ln_b, w_up, w_down):
    xp, xs = x_prompt, x_sample
    cmp_p, cmp_s, slc_p, slc_s, win_p, win_s = [], [], [], [], [], []
    C_p, C_s, n_p, n_s, m_p, m_s = [], [], [], [], [], []
    for l in range(DEPTH):
        lw = (w_ada[l], b_ada[l], w_in[l], b_gate[l], ml_norm_g[l], w_out[l], ln_g[l], ln_b[l], w_up[l], w_down[l])
        cmp_l = (cmp_pe[l], cmp_w1[l], cmp_w2[l])
        xp, mlp_state, nsp_state = trunk_layer(
            xp, c_prompt, *lw, mlstm_prompt,
            lambda *a: nsa_prompt(*a, *cmp_l, rel_bias))
        ml_s = lambda *a: mlstm_sample(*a, state_mlstm_C[l], state_mlstm_n[l], state_mlstm_m[l])
        nsa_s = lambda *a: nsa_sample(*a, cache_cmp_kv[l], cache_slc_kv[l], cache_win_kv[l], page_table,
                                      *cmp_l, rel_bias)
        xs, mls_state, nss_state = trunk_layer(xs, c_sample, *lw, ml_s, nsa_s)
        cmp_p.append(nsp_state[0]); slc_p.append(nsp_state[1]); win_p.append(nsp_state[2])
        cmp_s.append(nss_state[0]); slc_s.append(nss_state[1]); win_s.append(nss_state[2])
        C_p.append(mlp_state[0]); n_p.append(mlp_state[1]); m_p.append(mlp_state[2])
        C_s.append(mls_state[0]); n_s.append(mls_state[1]); m_s.append(mls_state[2])
    return (xp, xs,
            jnp.stack(cmp_p), jnp.stack(cmp_s),
            jnp.stack(slc_p), jnp.stack(slc_s),
            jnp.stack(win_p), jnp.stack(win_s),
            jnp.stack(C_p), jnp.stack(C_s),
            jnp.stack(n_p), jnp.stack(n_s),
            jnp.stack(m_p), jnp.stack(m_s))
```

```python
import functools
import math

import numpy as np
import jax
import jax.numpy as jnp
from jax import lax
from jax.experimental import pallas as pl
from jax.experimental.pallas import tpu as pltpu

F32 = jnp.float32
BF16 = jnp.bfloat16

D_MODEL = 1024
DEPTH = 2
PAGE = 128
ML_HEADS = 4
ML_HD = 128
ML_W = ML_HEADS * ML_HD
NSA_HEADS = 8
NSA_HD = 64
NSA_W = NSA_HEADS * NSA_HD
KV_HEADS = 2
Q_PER_KV = 4
KV_W = KV_HEADS * NSA_HD
CMP_BLOCK = 32
CMP_STRIDE = 16
CMP_HIDDEN = 256
SEL_BLOCK = 64
N_SELECT = 16
WINDOW = 512
NUM_BUCKETS = 32
REL_MAX_DIST = 128
D_FF = 4 * D_MODEL
ALPHA = (2 * DEPTH) ** 0.25
LN_EPS = 1e-5
NEG = -1e30
FORCE = 1e9
REMOVED = -3e38

LANES = 128
QT = 128
KT = 128
NC_PAD_FRONT = 16
NEAR_ROWS = 24

C_MQ, C_MK, C_MV, C_MO = 0, 512, 1024, 1536
C_NQ = 2048
C_CKV, C_SKV, C_WKV = 2560, 2816, 3072
C_GATE = 3328
C_NG = 3336
N_IN_PAD = 3456
VMEM_LIMIT = 56 * 1024 * 1024


def _cparams(sem):
    return pltpu.CompilerParams(dimension_semantics=sem, vmem_limit_bytes=VMEM_LIMIT)


def _dot(a, b):
    return jnp.dot(a, b, preferred_element_type=F32)


def _dot_nt(a, b):
    return lax.dot_general(a, b, (((1,), (1,)), ((), ())), preferred_element_type=F32)


def _dot_tn(a, b):
    return lax.dot_general(a, b, (((0,), (0,)), ((), ())), preferred_element_type=F32)


def _sigmoid(x):
    return 1.0 / (1.0 + jnp.exp(-x))


def _layer_norm(y, g, b):
    mu = jnp.mean(y, axis=-1, keepdims=True)
    d = y - mu
    var = jnp.mean(d * d, axis=-1, keepdims=True)
    return d * lax.rsqrt(var + LN_EPS) * g + b


def _ada_kernel(c_ref, w_ref, b_ref, o_ref):
    c = c_ref[...]
    a = (c * _sigmoid(c)).astype(BF16)
    o_ref[...] = _dot(a, w_ref[...]) + b_ref[...]


def ada_mod(c, w_bf, b):
    m, d = c.shape
    n = w_bf.shape[1]
    tn = 1536
    return pl.pallas_call(
        _ada_kernel,
        grid=(n // tn,),
        in_specs=[pl.BlockSpec((m, d), lambda j: (0, 0)),
                  pl.BlockSpec((d, tn), lambda j: (0, j)),
                  pl.BlockSpec((1, tn), lambda j: (0, j))],
        out_specs=pl.BlockSpec((m, tn), lambda j: (0, j)),
        out_shape=jax.ShapeDtypeStruct((m, n), F32),
        compiler_params=_cparams(("arbitrary",)),
        name="ada_mod",
    )(c, w_bf, b.reshape(1, n))


def _inproj_kernel(x_ref, sc_ref, sh_ref, w_ref, o_ref, u_scr):
    @pl.when(pl.program_id(1) == 0)
    def _():
        u_scr[...] = (x_ref[...] * (1.0 + sc_ref[0]) + sh_ref[0]).astype(BF16)

    o_ref[...] = _dot(u_scr[...], w_ref[...])


def _mod_spec(mod, tm, tiles_per_group):
    r = mod.shape[1]
    return pl.BlockSpec((1, r, D_MODEL), lambda i, j: (i // tiles_per_group, 0, 0))


def in_proj(x, sc, sh, w_bf, tm, tiles_per_group):
    m = x.shape[0]
    n = w_bf.shape[1]
    tn = 1152
    return pl.pallas_call(
        _inproj_kernel,
        grid=(m // tm, n // tn),
        in_specs=[pl.BlockSpec((tm, D_MODEL), lambda i, j: (i, 0)),
                  _mod_spec(sc, tm, tiles_per_group),
                  _mod_spec(sh, tm, tiles_per_group),
                  pl.BlockSpec((D_MODEL, tn), lambda i, j: (0, j))],
        out_specs=pl.BlockSpec((tm, tn), lambda i, j: (i, j)),
        out_shape=jax.ShapeDtypeStruct((m, n), F32),
        scratch_shapes=[pltpu.VMEM((tm, D_MODEL), BF16)],
        compiler_params=_cparams(("parallel", "arbitrary")),
        name="in_proj",
    )(x, sc, sh, w_bf)


def _outproj_kernel(x_ref, a1_ref, a2_ref, w1_ref, w2_ref, g_ref, lg_ref, lb_ref, o_ref):
    mixed = _dot(a1_ref[...], w1_ref[...]) + _dot(a2_ref[...], w2_ref[...])
    y = ALPHA * x_ref[...] + g_ref[0] * mixed
    o_ref[...] = _layer_norm(y, lg_ref[...], lb_ref[...])


def out_proj_ln(x, a1, a2, w_bf, gate, ln_g, ln_b, tm, tiles_per_group):
    m = x.shape[0]
    k1 = a1.shape[1]
    r = gate.shape[1]
    return pl.pallas_call(
        _outproj_kernel,
        grid=(m // tm,),
        in_specs=[pl.BlockSpec((tm, D_MODEL), lambda i: (i, 0)),
                  pl.BlockSpec((tm, k1), lambda i: (i, 0)),
                  pl.BlockSpec((tm, k1), lambda i: (i, 0)),
                  pl.BlockSpec((k1, D_MODEL), lambda i: (0, 0)),
                  pl.BlockSpec((k1, D_MODEL), lambda i: (1, 0)),
                  pl.BlockSpec((1, r, D_MODEL), lambda i: (i // tiles_per_group, 0, 0)),
                  pl.BlockSpec((1, D_MODEL), lambda i: (0, 0)),
                  pl.BlockSpec((1, D_MODEL), lambda i: (0, 0))],
        out_specs=pl.BlockSpec((tm, D_MODEL), lambda i: (i, 0)),
        out_shape=jax.ShapeDtypeStruct((m, D_MODEL), F32),
        compiler_params=_cparams(("parallel",)),
        name="out_proj_ln",
    )(x, a1, a2, w_bf, w_bf, gate, ln_g.reshape(1, -1), ln_b.reshape(1, -1))


def _ffn_kernel(x_ref, sc_ref, sh_ref, g_ref, wu_ref, wd_ref, lg_ref, lb_ref, o_ref, u_scr, acc_scr):
    f = pl.program_id(1)

    @pl.when(f == 0)
    def _():
        u_scr[...] = (x_ref[...] * (1.0 + sc_ref[0]) + sh_ref[0]).astype(BF16)
        acc_scr[...] = jnp.zeros_like(acc_scr)

    h = jnp.maximum(_dot(u_scr[...], wu_ref[...]), 0.0)
    acc_scr[...] += _dot((h * h).astype(BF16), wd_ref[...])

    @pl.when(f == pl.num_programs(1) - 1)
    def _():
        y = ALPHA * x_ref[...] + g_ref[0] * acc_scr[...]
        o_ref[...] = _layer_norm(y, lg_ref[...], lb_ref[...])


def ffn_ln(x, sc, sh, gate, wu_bf, wd_bf, ln_g, ln_b, tm, tiles_per_group):
    m = x.shape[0]
    tf = 512
    r = sc.shape[1]
    mod = pl.BlockSpec((1, r, D_MODEL), lambda i, f: (i // tiles_per_group, 0, 0))
    return pl.pallas_call(
        _ffn_kernel,
        grid=(m // tm, D_FF // tf),
        in_specs=[pl.BlockSpec((tm, D_MODEL), lambda i, f: (i, 0)),
                  mod, mod, mod,
                  pl.BlockSpec((D_MODEL, tf), lambda i, f: (0, f)),
                  pl.BlockSpec((tf, D_MODEL), lambda i, f: (f, 0)),
                  pl.BlockSpec((1, D_MODEL), lambda i, f: (0, 0)),
                  pl.BlockSpec((1, D_MODEL), lambda i, f: (0, 0))],
        out_specs=pl.BlockSpec((tm, D_MODEL), lambda i, f: (i, 0)),
        out_shape=jax.ShapeDtypeStruct((m, D_MODEL), F32),
        scratch_shapes=[pltpu.VMEM((tm, D_MODEL), BF16), pltpu.VMEM((tm, D_MODEL), F32)],
        compiler_params=_cparams(("parallel", "arbitrary")),
        name="ffn_ln",
    )(x, sc, sh, gate, wu_bf, wd_bf, ln_g.reshape(1, -1), ln_b.reshape(1, -1))


def _mlstm_kernel(bg_ref, q_ref, k_ref, v_ref, o_ref, g_ref, ng_ref, c0_ref, n0_ref, m0_ref,
                  h_ref, c_out, n_out, m_out, c_scr, n_scr, m_scr, *, nb, t, valid):
    ci = pl.program_id(1)

    @pl.when(ci == 0)
    def _():
        c_scr[...] = c0_ref[...]
        n_scr[...] = n0_ref[...]
        m_scr[...] = m0_ref[...]

    row = lax.broadcasted_iota(jnp.int32, (t, t), 0)
    col = lax.broadcasted_iota(jnp.int32, (t, t), 1)
    tri = col <= row
    tri_t = row <= col
    eye = row == col
    row1 = lax.broadcasted_iota(jnp.int32, (t, 1), 0)
    scale = ML_HD ** -0.5

    for b in range(nb):
        gates = g_ref[b]
        for h in range(ML_HEADS):
            hs = slice(h * ML_HD, (h + 1) * ML_HD)
            ig_col = gates[:, h:h + 1] + bg_ref[h]
            fr = gates[:, ML_HEADS + h:ML_HEADS + h + 1] + bg_ref[ML_HEADS + h]
            lf_col = jnp.minimum(fr, 0.0) - jnp.log1p(jnp.exp(-jnp.abs(fr)))
            if valid < t:
                ig_col = jnp.where(row1 < valid, ig_col, NEG)
                lf_col = jnp.where(row1 < valid, lf_col, 0.0)
            lf_row = jnp.sum(jnp.where(eye, lf_col, 0.0), axis=0, keepdims=True)
            ig_row = jnp.sum(jnp.where(eye, ig_col, 0.0), axis=0, keepdims=True)
            f_col = jnp.sum(jnp.where(tri, lf_row, 0.0), axis=1, keepdims=True)
            f_row = jnp.sum(jnp.where(tri_t, lf_col, 0.0), axis=0, keepdims=True)
            a_row = ig_row - f_row
            a_col = ig_col - f_col
            m0 = m_scr[b, h]
            cm_col = jnp.max(jnp.where(tri, a_row, NEG), axis=1, keepdims=True)
            g_col = jnp.maximum(m0, cm_col)
            dmat = jnp.exp(jnp.where(tri, a_row - g_col, NEG))
            decay_col = jnp.exp(m0 - g_col)
            m_col = f_col + g_col
            g_end = jnp.maximum(m0, jnp.max(a_row, axis=1, keepdims=True))
            f_end = jnp.sum(lf_row, axis=1, keepdims=True)
            w_end_col = jnp.exp(a_col - g_end)
            carry = jnp.exp(m0 - g_end)

            qf = q_ref[b][:, hs]
            qb = qf.astype(BF16)
            kf = k_ref[b][:, hs] * scale
            kb = kf.astype(BF16)
            vb = v_ref[b][:, hs].astype(BF16)
            cmat = c_scr[b, h]
            nvec = n_scr[b, h]

            w = _dot_nt(qb, kb) * dmat
            num = _dot(w.astype(BF16), vb) + _dot(qb, cmat.astype(BF16)) * decay_col
            den = (jnp.sum(w, axis=1, keepdims=True)
                   + jnp.sum(qf * nvec, axis=1, keepdims=True) * decay_col)
            hh = num / jnp.maximum(jnp.abs(den), jnp.exp(-m_col))

            mu = jnp.mean(hh, axis=1, keepdims=True)
            dd = hh - mu
            var = jnp.mean(dd * dd, axis=1, keepdims=True)
            hn = dd * lax.rsqrt(var + LN_EPS) * ng_ref[:, hs] * _sigmoid(o_ref[b][:, hs])
            h_ref[b, :, hs] = hn.astype(h_ref.dtype)

            kw = kf * w_end_col
            c_scr[b, h] = carry * cmat + _dot_tn(kw.astype(BF16), vb)
            n_scr[b, h] = carry * nvec + jnp.sum(kw, axis=0, keepdims=True)
            m_scr[b, h] = f_end + g_end

    @pl.when(ci == pl.num_programs(1) - 1)
    def _():
        c_out[...] = c_scr[...]
        n_out[...] = n_scr[...]
        m_out[...] = m_scr[...]


def mlstm(proj, b_gate, norm_g, c0, n0, m0, *, nb, t, valid):
    bsz, length, _ = proj.shape
    nchunk = length // t
    wide = lambda cb: pl.BlockSpec((nb, t, ML_W), lambda i, c: (i, c, cb))
    state4 = lambda s: pl.BlockSpec((nb,) + s, lambda i, c: (i, 0, 0, 0))
    kern = functools.partial(_mlstm_kernel, nb=nb, t=t, valid=valid)
    h, c1, n1, m1 = pl.pallas_call(
        kern,
        grid=(bsz // nb, nchunk),
        in_specs=[pl.BlockSpec(memory_space=pltpu.SMEM),
                  wide(C_MQ // ML_W), wide(C_MK // ML_W), wide(C_MV // ML_W), wide(C_MO // ML_W),
                  pl.BlockSpec((nb, t, LANES), lambda i, c: (i, c, C_GATE // LANES)),
                  pl.BlockSpec((1, ML_W), lambda i, c: (0, 0)),
                  state4((ML_HEADS, ML_HD, ML_HD)), state4((ML_HEADS, 1, ML_HD)), state4((ML_HEADS, 1, 1))],
        out_specs=[pl.BlockSpec((nb, t, ML_W), lambda i, c: (i, c, 0)),
                   state4((ML_HEADS, ML_HD, ML_HD)), state4((ML_HEADS, 1, ML_HD)), state4((ML_HEADS, 1, 1))],
        out_shape=[jax.ShapeDtypeStruct((bsz, length, ML_W), BF16),
                   jax.ShapeDtypeStruct((bsz, ML_HEADS, ML_HD, ML_HD), F32),
                   jax.ShapeDtypeStruct((bsz, ML_HEADS, 1, ML_HD), F32),
                   jax.ShapeDtypeStruct((bsz, ML_HEADS, 1, 1), F32)],
        scratch_shapes=[pltpu.VMEM((nb, ML_HEADS, ML_HD, ML_HD), F32),
                        pltpu.VMEM((nb, ML_HEADS, 1, ML_HD), F32),
                        pltpu.VMEM((nb, ML_HEADS, 1, 1), F32)],
        compiler_params=_cparams(("parallel", "arbitrary")),
        name="mlstm",
    )(b_gate, proj, proj, proj, proj, proj, norm_g.reshape(1, ML_W),
      c0, n0.reshape(bsz, ML_HEADS, 1, ML_HD), m0.reshape(bsz, ML_HEADS, 1, 1))
    return h, c1, n1.reshape(bsz, ML_HEADS, ML_HD), m1.reshape(bsz, ML_HEADS)


def _gelu_tanh(x):
    return 0.5 * x * (1.0 + jnp.tanh(math.sqrt(2.0 / math.pi) * (x + 0.044715 * (x * x * x))))


def _compress_kernel(*refs, npages, rows):
    pe_ref, w1_ref, w2_ref, o_ref = refs[-4:]
    page_refs = refs[-4 - 2 * npages:-4]
    jp = rows // CMP_STRIDE
    j = npages * jp
    lane = lax.broadcasted_iota(jnp.int32, (j, LANES), 1)
    low = lane < NSA_HD

    def chunk_phase(c, s):
        parts = [page_refs[2 * p + s][0, pl.ds(c, jp, stride=CMP_STRIDE), :] for p in range(npages)]
        return parts[0] if npages == 1 else jnp.concatenate(parts, axis=0)

    out = jnp.zeros((j, 2 * KV_HEADS * NSA_HD), F32)
    for s in range(2):
        acc = [jnp.zeros((j, 2 * CMP_HIDDEN), F32) for _ in range(KV_HEADS)]
        for q4 in range(CMP_STRIDE // 4):
            halves = [[], []]
            for c in (4 * q4, 4 * q4 + 2):
                va = chunk_phase(c, s)
                vb = chunk_phase(c + 1, s)
                halves[0].append(jnp.where(low, va, pltpu.roll(vb, NSA_HD, axis=1)))
                halves[1].append(jnp.where(low, pltpu.roll(va, NSA_HD, axis=1), vb))
            for g in range(KV_HEADS):
                lhs = jnp.concatenate(halves[g], axis=1).astype(BF16)
                acc[g] = acc[g] + _dot(lhs, w1_ref[s, q4])
        pe_c = jnp.zeros((16, 2 * CMP_HIDDEN), F32)
        for q4 in range(CMP_STRIDE // 4):
            pe_c = pe_c + _dot(pe_ref[s, q4], w1_ref[s, q4])
        pe_const = pe_c[0:1, :CMP_HIDDEN] + pe_c[1:2, CMP_HIDDEN:]
        for g in range(KV_HEADS):
            hid = acc[g][:, :CMP_HIDDEN] + pltpu.roll(acc[g][:, CMP_HIDDEN:], j - 1, axis=0) + pe_const
            out = out + _dot(_gelu_tanh(hid).astype(BF16), w2_ref[s * KV_HEADS + g])
    o_ref[0] = out


def _compress_weights(pe, w1, w2):
    w1r = w1.reshape(2, CMP_BLOCK, NSA_HD, CMP_HIDDEN)
    wa = w1r[:, :CMP_STRIDE].reshape(2, 4, 4 * NSA_HD, CMP_HIDDEN)
    wb = w1r[:, CMP_STRIDE:].reshape(2, 4, 4 * NSA_HD, CMP_HIDDEN)
    w1p = jnp.concatenate([wa, wb], axis=-1).astype(BF16)
    pea = pe[:, :CMP_STRIDE].reshape(2, 4, 1, 4 * NSA_HD)
    peb = pe[:, CMP_STRIDE:].reshape(2, 4, 1, 4 * NSA_HD)
    pep = jnp.concatenate([pea, peb, jnp.zeros((2, 4, 14, 4 * NSA_HD), F32)], axis=2).astype(BF16)
    w2p = jnp.zeros((2, KV_HEADS, CMP_HIDDEN, 2, KV_HEADS, NSA_HD), F32)
    for s in range(2):
        for g in range(KV_HEADS):
            w2p = w2p.at[s, g, :, s, g, :].set(w2[s])
    w2p = w2p.reshape(2 * KV_HEADS, CMP_HIDDEN, 2 * KV_W).astype(BF16)
    return pep, w1p, w2p


def compress(pages, page_index_maps, num_scalar_prefetch, prefetch, grid_n, npages, rows, cw):
    pep, w1p, w2p = cw
    j = npages * rows // CMP_STRIDE
    const = lambda nd: (lambda *a: (0,) * nd)
    grid_spec = pltpu.PrefetchScalarGridSpec(
        num_scalar_prefetch=num_scalar_prefetch,
        grid=(grid_n,),
        in_specs=[pl.BlockSpec((1, rows, KV_W), im(s)) for im in page_index_maps for s in range(2)]
        + [pl.BlockSpec(pep.shape, const(4)), pl.BlockSpec(w1p.shape, const(4)), pl.BlockSpec(w2p.shape, const(3))],
        out_specs=pl.BlockSpec((1, j, 2 * KV_W), lambda b, *a: (b, 0, 0)),
    )
    return pl.pallas_call(
        functools.partial(_compress_kernel, npages=npages, rows=rows),
        grid_spec=grid_spec,
        out_shape=jax.ShapeDtypeStruct((grid_n, j, 2 * KV_W), F32),
        compiler_params=_cparams(("parallel",)),
        name="compress",
    )(*prefetch, *([pages] * (2 * npages)), pep, w1p, w2p)


def _rel_bucket(dist):
    n = jnp.maximum(dist, 0)
    max_exact = NUM_BUCKETS // 2
    nf = jnp.maximum(n, 1).astype(F32)
    large = max_exact + (jnp.log(nf / max_exact) / math.log(REL_MAX_DIST / max_exact)
                         * (NUM_BUCKETS - max_exact)).astype(jnp.int32)
    large = jnp.minimum(large, NUM_BUCKETS - 1)
    return jnp.where(n < max_exact, n, large)


def _delta_bias(rel_bias, dist):
    far = rel_bias[NUM_BUCKETS - 1]
    val = rel_bias[_rel_bucket(dist)] - far
    return jnp.where(((dist >= 0) & (dist < REL_MAX_DIST))[..., None], val, 0.0)


def _lanes_ri(tab, g):
    rows = tab.shape[0]
    return tab[:, :, g * Q_PER_KV:(g + 1) * Q_PER_KV].transpose(0, 2, 1).reshape(rows, Q_PER_KV * QT)


def _prompt_tables(rel_bias):
    i = jnp.arange(QT)[None, :]
    j = jnp.arange(KT)[:, None]
    d_diag = i - j
    diag = jnp.where((d_diag >= 0)[..., None], _delta_bias(rel_bias, d_diag), NEG)
    sub = _delta_bias(rel_bias, i + KT - j)
    n2 = jnp.arange(NEAR_ROWS)[:, None]
    near = _delta_bias(rel_bias, i + (CMP_STRIDE * NC_PAD_FRONT - CMP_BLOCK + 1) - CMP_STRIDE * n2)
    stack = lambda tab: jnp.stack([_lanes_ri(tab, g) for g in range(KV_HEADS)])
    return stack(sub), stack(diag), stack(near)


def _cover_t(n_sel, n_cmp, ncp):
    n = np.arange(ncp) - NC_PAD_FRONT
    c_start = n * CMP_STRIDE
    s_start = np.arange(n_sel)[:, None] * SEL_BLOCK
    cov = (c_start[None, :] < s_start + SEL_BLOCK) & (c_start[None, :] + CMP_BLOCK > s_start)
    cov &= ((n >= 0) & (n < n_cmp))[None, :]
    return jnp.asarray(cov, BF16)


def _nsa_prompt_kernel(q_ref, gt_ref, kc_ref, vct_ref, cov_ref, bct_ref, ks_ref, vst_ref, kw_ref, vwt_ref,
                       tsub_ref, tdiag_ref, o_ref, s_scr, p_scr, sel_scr, m_scr, l_scr, acc_scr, ot_scr,
                       *, nsel, ncp, ncv):
    blk = pl.program_id(1)
    t0 = blk * QT
    lanes4 = Q_PER_KV * QT
    q_t = (q_ref[0] * (NSA_HD ** -0.5)).T

    def tile4(x):
        return jnp.concatenate([x] * Q_PER_KV, axis=1)

    def reset():
        m_scr[...] = jnp.full(m_scr.shape, NEG, F32)
        l_scr[...] = jnp.zeros(l_scr.shape, F32)
        acc_scr[...] = jnp.zeros(acc_scr.shape, F32)

    def attend(k_tile, vt_tile, qpad, add):
        s = _dot(k_tile, qpad)
        if add is not None:
            s = s + add
        m_old = m_scr[...]
        m_new = jnp.maximum(m_old, jnp.max(s, axis=0, keepdims=True))
        alpha = jnp.exp(m_old - m_new)
        p = jnp.exp(s - m_new)
        l_scr[...] = alpha * l_scr[...] + jnp.sum(p, axis=0, keepdims=True)
        acc_scr[...] = alpha * acc_scr[...] + _dot(vt_tile, p.astype(BF16))
        m_scr[...] = m_new

    def finish():
        inv = jnp.where(m_scr[...] > 0.5 * NEG, 1.0 / l_scr[...], 0.0)
        return acc_scr[...] * inv

    row_k = lax.broadcasted_iota(jnp.int32, (KT, QT), 0)
    lane_q = lax.broadcasted_iota(jnp.int32, (KT, QT), 1)
    anti = tile4(jnp.where(row_k > lane_q, 0.0, NEG))

    for g in range(KV_HEADS):
        gs = slice(g * NSA_HD, (g + 1) * NSA_HD)
        qg = jnp.concatenate([q_t[(g * Q_PER_KV + r) * NSA_HD:(g * Q_PER_KV + r + 1) * NSA_HD, :]
                              for r in range(Q_PER_KV)], axis=1)
        zero = jnp.zeros_like(qg)
        qpad = jnp.concatenate([qg, zero] if g == 0 else [zero, qg], axis=0).astype(BF16)

        s_scr[...] = _dot(kc_ref[0], qpad)
        near0 = pl.multiple_of(blk * (QT // CMP_STRIDE), 8)
        s_scr[pl.ds(near0, NEAR_ROWS), :] = s_scr[pl.ds(near0, NEAR_ROWS), :] + bct_ref[g]
        mx = jnp.full((1, lanes4), NEG, F32)
        nchunk = ncp // KT
        for c in range(nchunk):
            n = row_k + (c * KT - NC_PAD_FRONT)
            ok = (n >= 0) & (n < ncv) & (CMP_STRIDE * n + (CMP_BLOCK - 1) <= t0 + lane_q)
            sc = s_scr[c * KT:(c + 1) * KT, :] + tile4(jnp.where(ok, 0.0, NEG))
            s_scr[c * KT:(c + 1) * KT, :] = sc
            mx = jnp.maximum(mx, jnp.max(sc, axis=0, keepdims=True))
        lsum = jnp.zeros((1, lanes4), F32)
        for c in range(nchunk):
            p = jnp.exp(s_scr[c * KT:(c + 1) * KT, :] - mx)
            lsum = lsum + jnp.sum(p, axis=0, keepdims=True)
            p_scr[c * KT:(c + 1) * KT, :] = p.astype(BF16)
        inv_c = jnp.where(mx > 0.5 * NEG, 1.0 / lsum, 0.0)
        o_cmp = _dot(vct_ref[0, gs, :], p_scr[...]) * inv_c
        imp4 = _dot(cov_ref[...], p_scr[...]) * inv_c
        imp = imp4[:, 0:QT]
        for r in range(1, Q_PER_KV):
            imp = imp + imp4[:, r * QT:(r + 1) * QT]

        jb = lax.broadcasted_iota(jnp.int32, (nsel, QT), 0)
        tq = t0 + lax.broadcasted_iota(jnp.int32, (nsel, QT), 1)
        cur = tq // SEL_BLOCK
        forced = (jb == 0) | (jb == cur) | (jb == cur - 1)
        future = jb * SEL_BLOCK > tq
        score = jnp.where(forced, FORCE, jnp.where(future, -FORCE, imp))
        chosen_any = jnp.zeros((nsel, QT), jnp.bool_)
        for _ in range(min(N_SELECT, nsel)):
            best = jnp.max(score, axis=0, keepdims=True)
            first = jnp.min(jnp.where(score == best, jb, nsel), axis=0, keepdims=True)
            chosen = jb == first
            chosen_any = chosen_any | chosen
            score = jnp.where(chosen, REMOVED, score)
        sel_scr[...] = tile4(jnp.where(chosen_any, 0.0, NEG))

        def sel_mask(kt):
            r0 = sel_scr[pl.ds(2 * kt, 1), :]
            r1 = sel_scr[pl.ds(2 * kt + 1, 1), :]
            return jnp.concatenate([jnp.broadcast_to(r0, (SEL_BLOCK, lanes4)),
                                    jnp.broadcast_to(r1, (SEL_BLOCK, lanes4))], axis=0)

        reset()

        def far_body(kt, carry):
            attend(ks_ref[0, kt], vst_ref[0, kt, gs, :], qpad, sel_mask(kt))
            return carry

        lax.fori_loop(0, jnp.maximum(blk - 1, 0), far_body, 0)

        @pl.when(blk >= 1)
        def _():
            attend(ks_ref[0, blk - 1], vst_ref[0, blk - 1, gs, :], qpad, sel_mask(blk - 1) + tsub_ref[g])

        attend(ks_ref[0, blk], vst_ref[0, blk, gs, :], qpad, sel_mask(blk) + tdiag_ref[g])
        o_sel = finish()

        reset()
        for back, add in ((4, anti), (3, None), (2, None)):
            @pl.when(blk >= back)
            def _(back=back, add=add):
                attend(kw_ref[0, blk - back], vwt_ref[0, blk - back, gs, :], qpad, add)

        @pl.when(blk >= 1)
        def _():
            attend(kw_ref[0, blk - 1], vwt_ref[0, blk - 1, gs, :], qpad, tsub_ref[g])

        attend(kw_ref[0, blk], vwt_ref[0, blk, gs, :], qpad, tdiag_ref[g])
        o_win = finish()

        gl = gt_ref[0, 0]
        o_g = (_sigmoid(gl[3 * g:3 * g + 1, :]) * o_cmp + _sigmoid(gl[3 * g + 1:3 * g + 2, :]) * o_sel
               + _sigmoid(gl[3 * g + 2:3 * g + 3, :]) * o_win)
        for r in range(Q_PER_KV):
            h = g * Q_PER_KV + r
            ot_scr[h * NSA_HD:(h + 1) * NSA_HD, :] = o_g[:, r * QT:(r + 1) * QT]

    o_ref[0] = ot_scr[...].T.astype(o_ref.dtype)


def nsa_prompt(proj, ckv, rel_tabs):
    bsz, seq, _ = proj.shape
    nq = seq // QT
    n_cmp = (seq - CMP_BLOCK) // CMP_STRIDE + 1
    nsel = seq // SEL_BLOCK
    ncp = -(-(NC_PAD_FRONT + seq // CMP_STRIDE) // KT) * KT
    tsub, tdiag, bct = rel_tabs

    def tiles(cols):
        return proj[:, :, cols:cols + KV_W].astype(BF16).reshape(bsz, nq, KT, KV_W)

    ks = tiles(C_SKV)
    vst = tiles(C_SKV + KV_W).transpose(0, 1, 3, 2)
    kw = tiles(C_WKV)
    vwt = tiles(C_WKV + KV_W).transpose(0, 1, 3, 2)
    back = ncp - NC_PAD_FRONT - ckv.shape[1]
    kc = jnp.pad(ckv[:, :, :KV_W], ((0, 0), (NC_PAD_FRONT, back), (0, 0))).astype(BF16)
    vct = jnp.pad(ckv[:, :, KV_W:], ((0, 0), (NC_PAD_FRONT, back), (0, 0))).astype(BF16).transpose(0, 2, 1)
    cov = _cover_t(nsel, n_cmp, ncp)
    ng = proj[:, :, C_NG:C_NG + 3 * NSA_HEADS].reshape(bsz, nq, QT, KV_HEADS, Q_PER_KV, 3)
    gt = ng.transpose(0, 1, 3, 5, 4, 2).reshape(bsz, nq, KV_HEADS * 3, Q_PER_KV * QT)
    gt = jnp.pad(gt, ((0, 0), (0, 0), (0, 2), (0, 0)))

    whole = lambda a: pl.BlockSpec((1,) + a.shape[1:], lambda b, i: (b,) + (0,) * (a.ndim - 1))
    const = lambda a: pl.BlockSpec(a.shape, lambda b, i: (0,) * a.ndim)
    lanes4 = Q_PER_KV * QT
    return pl.pallas_call(
        functools.partial(_nsa_prompt_kernel, nsel=nsel, ncp=ncp, ncv=n_cmp),
        grid=(bsz, nq),
        in_specs=[pl.BlockSpec((1, QT, NSA_W), lambda b, i: (b, i, C_NQ // NSA_W)),
                  pl.BlockSpec((1, 1, 8, lanes4), lambda b, i: (b, i, 0, 0)),
                  whole(kc), whole(vct), const(cov), const(bct),
                  whole(ks), whole(vst), whole(kw), whole(vwt), const(tsub), const(tdiag)],
        out_specs=pl.BlockSpec((1, QT, NSA_W), lambda b, i: (b, i, 0)),
        out_shape=jax.ShapeDtypeStruct((bsz, seq, NSA_W), BF16),
        scratch_shapes=[pltpu.VMEM((ncp, lanes4), F32), pltpu.VMEM((ncp, lanes4), BF16),
                        pltpu.VMEM((nsel, lanes4), F32),
                        pltpu.VMEM((1, lanes4), F32), pltpu.VMEM((1, lanes4), F32),
                        pltpu.VMEM((NSA_HD, lanes4), F32), pltpu.VMEM((NSA_W, QT), F32)],
        compiler_params=_cparams(("parallel", "arbitrary")),
        name="nsa_prompt",
    )(proj, gt, kc, vct, cov, bct, ks, vst, kw, vwt, tsub, tdiag)


S_ROWS = Q_PER_KV * KV_HEADS * 4
NEW_PAD = 16


def _sample_tables(rel_bias, past, t_new, n_cmp, win_len):
    row = jnp.arange(S_ROWS)
    i = (row % t_new)[:, None]
    g = (row // t_new) % KV_HEADS
    r = row // (t_new * KV_HEADS)
    head = g * Q_PER_KV + r

    def pick(tab):
        return jnp.take_along_axis(tab, head[:, None, None], axis=2)[..., 0]

    n = jnp.arange(LANES)[None, :]
    bcs = jnp.where(n < n_cmp, pick(_delta_bias(rel_bias, past + i - (CMP_STRIDE * n + CMP_BLOCK - 1))), NEG)
    jj = jnp.arange(LANES)[None, :]
    last = pick(_delta_bias(rel_bias, LANES + i - jj))
    w_old = jnp.where(jj > i, 0.0, NEG)
    j2 = jnp.arange(NEW_PAD)[None, :]
    d_new = i - j2
    new = jnp.where((j2 < t_new) & (d_new >= 0), pick(_delta_bias(rel_bias, d_new)), NEG)
    return bcs, jnp.stack([last, w_old]), new


def _nsa_sample_kernel(*refs, npages, past, t_new):
    pt_ref, q_ref, g_ref, kc_ref = refs[:4]
    page_refs = refs[4:4 + npages]
    win_ref, news_ref, neww_ref, bcs_ref, tab_ref, tnew_ref, cov_ref, o_ref, s_scr = refs[4 + npages:]
    del pt_ref
    qb = (q_ref[0] * (NSA_HD ** -0.5)).astype(BF16)

    def softmax_pv(tiles):
        mx = jnp.full((S_ROWS, 1), NEG, F32)
        for c, (keys_fn, _, add_fn, width) in enumerate(tiles):
            s = _dot_nt(qb, keys_fn())
            if add_fn is not None:
                s = s + add_fn()
            s_scr[:, c * LANES:c * LANES + width] = s
            mx = jnp.maximum(mx, s.max(axis=1, keepdims=True))
        lsum = jnp.zeros((S_ROWS, 1), F32)
        acc = jnp.zeros((S_ROWS, KV_W), F32)
        for c, (_, values_fn, _, width) in enumerate(tiles):
            p = jnp.exp(s_scr[:, c * LANES:c * LANES + width] - mx)
            lsum = lsum + p.sum(axis=1, keepdims=True)
            acc = acc + _dot(p.astype(BF16), values_fn())
        return acc / lsum

    kc = kc_ref[0]
    s_c = _dot_nt(qb, kc[:, :KV_W].astype(BF16)) + bcs_ref[...]
    p_c = jnp.exp(s_c - s_c.max(axis=1, keepdims=True))
    p_c = (p_c / p_c.sum(axis=1, keepdims=True)).astype(BF16)
    o_cmp = _dot(p_c, kc[:, KV_W:].astype(BF16))
    imp_r = _dot(p_c, cov_ref[...])
    rows_gi = KV_HEADS * t_new
    imp = imp_r
    for r in range(1, Q_PER_KV):
        imp = imp + pltpu.roll(imp_r, r * rows_gi, axis=0)

    jb = lax.broadcasted_iota(jnp.int32, (S_ROWS, LANES), 1)
    qpos = past + lax.broadcasted_iota(jnp.int32, (S_ROWS, LANES), 0) % t_new
    cur = qpos // SEL_BLOCK
    forced = (jb == 0) | (jb == cur) | (jb == cur - 1)
    future = jb * SEL_BLOCK > qpos
    score = jnp.where(forced, FORCE, jnp.where(future, -FORCE, imp))
    chosen_any = jnp.zeros((S_ROWS, LANES), jnp.bool_)
    for _ in range(N_SELECT):
        best = jnp.max(score, axis=1, keepdims=True)
        first = jnp.min(jnp.where(score == best, jb, LANES), axis=1, keepdims=True)
        chosen = jb == first
        chosen_any = chosen_any | chosen
        score = jnp.where(chosen, REMOVED, score)
    sel = jnp.where(chosen_any, 1.0, 0.0).astype(BF16)

    def block_mask(j0, nkeys=LANES):
        jrow = lax.broadcasted_iota(jnp.int32, (LANES, nkeys), 0)
        kcol = lax.broadcasted_iota(jnp.int32, (LANES, nkeys), 1)
        expand = jnp.where(jrow == j0 + kcol // SEL_BLOCK, 1.0, 0.0).astype(BF16)
        return jnp.where(_dot(sel, expand) > 0.5, 0.0, NEG)

    jn = past // SEL_BLOCK
    tiles = []
    for p in range(npages):
        if p == npages - 1:
            add = lambda p=p: block_mask(2 * p) + tab_ref[0]
        else:
            add = lambda p=p: block_mask(2 * p)
        tiles.append((lambda p=p: page_refs[p][0, :, :KV_W].astype(BF16),
                      lambda p=p: page_refs[p][0, :, KV_W:].astype(BF16), add, LANES))
    tiles.append((lambda: news_ref[0, :, :KV_W].astype(BF16), lambda: news_ref[0, :, KV_W:].astype(BF16),
                  lambda: tnew_ref[...] + block_mask(jn, NEW_PAD), NEW_PAD))
    o_sel = softmax_pv(tiles)

    nwin = win_ref.shape[1] // LANES
    tiles = []
    for c in range(nwin):
        if c == 0:
            add = lambda: tab_ref[1]
        elif c == nwin - 1:
            add = lambda: tab_ref[0]
        else:
            add = None
        tiles.append((lambda c=c: win_ref[0, c * LANES:(c + 1) * LANES, :KV_W].astype(BF16),
                      lambda c=c: win_ref[0, c * LANES:(c + 1) * LANES, KV_W:].astype(BF16), add, LANES))
    tiles.append((lambda: neww_ref[0, :, :KV_W].astype(BF16), lambda: neww_ref[0, :, KV_W:].astype(BF16),
                  lambda: tnew_ref[...], NEW_PAD))
    o_win = softmax_pv(tiles)

    gl = g_ref[0]
    o_ref[0] = (_sigmoid(gl[:, 0:1]) * o_cmp + _sigmoid(gl[:, 1:2]) * o_sel + _sigmoid(gl[:, 2:3]) * o_win)


def nsa_sample(proj, ckv, pool_s, win_buf, page_table, rel_bias):
    bsz, t_new, _ = proj.shape
    npages = page_table.shape[1]
    past = npages * PAGE
    n_cmp = (past + t_new - CMP_BLOCK) // CMP_STRIDE + 1
    n_sel = -(-(past + t_new) // SEL_BLOCK)
    assert n_cmp <= LANES and n_sel <= LANES and win_buf.shape[1] == WINDOW and t_new == 4
    assert (past + t_new - 1) // SEL_BLOCK == past // SEL_BLOCK
    bcs, tab, tnew = _sample_tables(rel_bias, past, t_new, n_cmp, win_buf.shape[1])
    n = np.arange(LANES)
    jsel = np.arange(LANES)
    cov = ((n[:, None] * CMP_STRIDE < jsel[None, :] * SEL_BLOCK + SEL_BLOCK)
           & (n[:, None] * CMP_STRIDE + CMP_BLOCK > jsel[None, :] * SEL_BLOCK)
           & (n[:, None] < n_cmp) & (jsel[None, :] < n_sel))
    cov = jnp.asarray(cov, BF16)

    eye = jnp.eye(KV_HEADS, dtype=F32)
    q5 = proj[:, :, C_NQ:C_NQ + NSA_W].reshape(bsz, t_new, KV_HEADS, Q_PER_KV, NSA_HD).transpose(0, 3, 2, 1, 4)
    qpad = (q5[:, :, :, :, None, :] * eye[None, None, :, None, :, None]).reshape(bsz, S_ROWS, KV_W)
    ng = proj[:, :, C_NG:C_NG + 3 * NSA_HEADS].reshape(bsz, t_new, KV_HEADS, Q_PER_KV, 3).transpose(0, 3, 2, 1, 4)
    gl = jnp.pad(ng.reshape(bsz, S_ROWS, 3), ((0, 0), (0, 0), (0, LANES - 3)))
    pad_new = lambda c: jnp.pad(proj[:, :, c:c + 2 * KV_W], ((0, 0), (0, NEW_PAD - t_new), (0, 0)))

    per_b = lambda a: pl.BlockSpec((1,) + a.shape[1:], lambda b, pt: (b,) + (0,) * (a.ndim - 1))
    const = lambda a: pl.BlockSpec(a.shape, lambda b, pt: (0,) * a.ndim)
    new_s, new_w = pad_new(C_SKV), pad_new(C_WKV)
    grid_spec = pltpu.PrefetchScalarGridSpec(
        num_scalar_prefetch=1,
        grid=(bsz,),
        in_specs=[per_b(qpad), per_b(gl), per_b(ckv)]
        + [pl.BlockSpec((1, PAGE, 2 * KV_W), (lambda b, pt, p=p: (pt[b, p], 0, 0))) for p in range(npages)]
        + [per_b(win_buf), per_b(new_s), per_b(new_w), const(bcs), const(tab), const(tnew), const(cov)],
        out_specs=pl.BlockSpec((1, S_ROWS, KV_W), lambda b, pt: (b, 0, 0)),
        scratch_shapes=[pltpu.VMEM((S_ROWS, (npages + 1) * LANES), F32)],
    )
    out = pl.pallas_call(
        functools.partial(_nsa_sample_kernel, npages=npages, past=past, t_new=t_new),
        grid_spec=grid_spec,
        out_shape=jax.ShapeDtypeStruct((bsz, S_ROWS, KV_W), F32),
        compiler_params=_cparams(("parallel",)),
        name="nsa_sample",
    )(page_table, qpad, gl, ckv, *([pool_s] * npages), win_buf, new_s, new_w, bcs, tab, tnew, cov)
    o6 = out.reshape(bsz, Q_PER_KV, KV_HEADS, t_new, KV_HEADS, NSA_HD)
    o5 = jnp.stack([o6[:, :, g, :, g, :] for g in range(KV_HEADS)], axis=2)
    return o5.transpose(0, 3, 2, 1, 4).reshape(bsz, t_new, NSA_W)


def _in_proj_perm():
    sizes = (ML_W, ML_W, ML_W, ML_W, ML_HEADS, ML_HEADS, NSA_W, KV_W, KV_W, KV_W, KV_W, KV_W, KV_W, 3 * NSA_HEADS)
    off = np.concatenate([[0], np.cumsum(sizes)])
    order = (0, 1, 2, 3, 6, 7, 8, 9, 10, 11, 12, 4, 5, 13)
    return np.concatenate([np.arange(off[k], off[k + 1]) for k in order])


def _layer(x, mods, proj_fn_args, lw, tm, tiles_per_group, mixer):
    sh1, sc1, g1, sh2, sc2, g2 = mods
    w_in_bf, w_out_bf, w_up_bf, w_down_bf, ln_g, ln_b = lw
    proj = in_proj(x, sc1, sh1, w_in_bf, tm, tiles_per_group)
    h_ml, o_nsa, extras = mixer(proj)
    x1 = out_proj_ln(x, h_ml, o_nsa, w_out_bf, g1, ln_g[0], ln_b[0], tm, tiles_per_group)
    x2 = ffn_ln(x1, sc2, sh2, g2, w_up_bf, w_down_bf, ln_g[1], ln_b[1], tm, tiles_per_group)
    return x2, proj, extras


def kernel(x_prompt, x_sample, cache_cmp_kv, cache_slc_kv, cache_win_kv, state_mlstm_C, state_mlstm_n,
           state_mlstm_m, page_table, c_prompt, c_sample, w_ada, b_ada, w_in, b_gate, ml_norm_g, cmp_pe,
           cmp_w1, cmp_w2, rel_bias, w_out, ln_g, ln_b, w_up, w_down):
    bsz, seq, d = x_prompt.shape
    bs, ts, _ = x_sample.shape
    depth = w_in.shape[0]
    n_phys = cache_cmp_kv.shape[1]
    npages = page_table.shape[1]
    win_len = cache_win_kv.shape[2]
    tm = 512
    t_ml = 128
    t_pad = 16

    perm = _in_proj_perm()
    prompt_tabs = _prompt_tables(rel_bias)
    nc_rows = bsz + bs
    c_all = jnp.pad(jnp.concatenate([c_prompt, c_sample], axis=0), ((0, -nc_rows % 8), (0, 0)))

    xp = x_prompt.reshape(bsz * seq, d)
    xs = x_sample.reshape(bs * ts, d)
    outs = {k: [] for k in ("cmp_p", "cmp_s", "slc_p", "slc_s", "win_p", "win_s",
                            "C_p", "C_s", "n_p", "n_s", "m_p", "m_s")}
    kv_shape = lambda b, t: (b, t, 2, KV_HEADS, NSA_HD)

    for l in range(depth):
        ada = ada_mod(c_all, w_ada[l].astype(BF16), b_ada[l]).reshape(c_all.shape[0], 6, d)
        mods_p = [ada[:bsz, k][:, None, :] for k in range(6)]
        mods_s = [jnp.repeat(ada[bsz:nc_rows, k], ts, axis=0)[None] for k in range(6)]
        w_in_bf = jnp.pad(w_in[l][:, perm], ((0, 0), (0, N_IN_PAD - perm.size))).astype(BF16)
        lw = (w_in_bf, w_out[l].astype(BF16), w_up[l].astype(BF16), w_down[l].astype(BF16), ln_g[l], ln_b[l])
        cw = _compress_weights(cmp_pe[l], cmp_w1[l], cmp_w2[l])

        def prompt_mixer(proj):
            proj = proj.reshape(bsz, seq, N_IN_PAD)
            h_ml, c1, n1, m1 = mlstm(proj, b_gate[l], ml_norm_g[l],
                                     jnp.zeros((bsz, ML_HEADS, ML_HD, ML_HD), F32),
                                     jnp.zeros((bsz, ML_HEADS, ML_HD), F32), jnp.zeros((bsz, ML_HEADS), F32),
                                     nb=bsz, t=t_ml, valid=t_ml)
            ckv = compress(proj, [lambda s: (lambda b: (b, 0, C_CKV // KV_W + s))], 0, (), bsz, 1, seq, cw)
            o_nsa = nsa_prompt(proj, ckv, prompt_tabs)
            return h_ml.reshape(bsz * seq, ML_W), o_nsa.reshape(bsz * seq, NSA_W), (c1, n1, m1)

        def sample_mixer(proj):
            proj = proj.reshape(bs, ts, N_IN_PAD)
            proj_pad = jnp.pad(proj, ((0, 0), (0, t_pad - ts), (0, 0)))
            h_ml, c1, n1, m1 = mlstm(proj_pad, b_gate[l], ml_norm_g[l], state_mlstm_C[l], state_mlstm_n[l],
                                     state_mlstm_m[l], nb=8, t=t_pad, valid=ts)
            pages = [(lambda s, p=p: (lambda b, pt: (pt[b, p], 0, s))) for p in range(npages)]
            ckv = compress(cache_cmp_kv[l].reshape(n_phys, PAGE, 2 * KV_W), pages, 1, (page_table,),
                           bs, npages, PAGE, cw)
            o_nsa = nsa_sample(proj, ckv, cache_slc_kv[l].reshape(n_phys, PAGE, 2 * KV_W),
                               cache_win_kv[l].reshape(bs, win_len, 2 * KV_W), page_table, rel_bias)
            return (h_ml[:, :ts].reshape(bs * ts, ML_W), o_nsa.reshape(bs * ts, NSA_W).astype(BF16),
                    (c1, n1, m1))

        xp, proj_p, st_p = _layer(xp, mods_p, None, lw, tm, seq // tm, prompt_mixer)
        xs, proj_s, st_s = _layer(xs, mods_s, None, lw, bs * ts, 1, sample_mixer)

        proj_p = proj_p.reshape(bsz, seq, N_IN_PAD)
        proj_s = proj_s.reshape(bs, ts, N_IN_PAD)
        rows = lambda proj, c: proj[:, :, c:c + 2 * KV_W]
        win = min(WINDOW, seq)
        outs["cmp_p"].append(rows(proj_p, C_CKV).reshape(kv_shape(bsz, seq)))
        outs["slc_p"].append(rows(proj_p, C_SKV).reshape(kv_shape(bsz, seq)))
        outs["win_p"].append(rows(proj_p, C_WKV)[:, seq - win:].reshape(kv_shape(bsz, win)))
        outs["cmp_s"].append(rows(proj_s, C_CKV).reshape(kv_shape(bs, ts)))
        outs["slc_s"].append(rows(proj_s, C_SKV).reshape(kv_shape(bs, ts)))
        win_all = jnp.concatenate([cache_win_kv[l], rows(proj_s, C_WKV).reshape(kv_shape(bs, ts))], axis=1)
        outs["win_s"].append(win_all[:, -win_len:])
        for tag, st in (("p", st_p), ("s", st_s)):
            outs["C_" + tag].append(st[0])
            outs["n_" + tag].append(st[1])
            outs["m_" + tag].append(st[2])

    stk = lambda k: jnp.stack(outs[k])
    return (xp.reshape(bsz, seq, d), xs.reshape(bs, ts, d),
            stk("cmp_p"), stk("cmp_s"), stk("slc_p"), stk("slc_s"), stk("win_p"), stk("win_s"),
            stk("C_p"), stk("C_s"), stk("n_p"), stk("n_s"), stk("m_p"), stk("m_s"))
```

```python
import functools
import math

import numpy as np
import jax
import jax.numpy as jnp
from jax import lax
from jax.experimental import pallas as pl
from jax.experimental.pallas import tpu as pltpu

F32 = jnp.float32
BF16 = jnp.bfloat16

D_MODEL = 1024
DEPTH = 2
PAGE = 128
ML_HEADS = 4
ML_HD = 128
ML_W = ML_HEADS * ML_HD
NSA_HEADS = 8
NSA_HD = 64
NSA_W = NSA_HEADS * NSA_HD
KV_HEADS = 2
Q_PER_KV = 4
KV_W = KV_HEADS * NSA_HD
CMP_BLOCK = 32
CMP_STRIDE = 16
CMP_HIDDEN = 256
SEL_BLOCK = 64
N_SELECT = 16
WINDOW = 512
NUM_BUCKETS = 32
REL_MAX_DIST = 128
D_FF = 4 * D_MODEL
ALPHA = (2 * DEPTH) ** 0.25
LN_EPS = 1e-5
NEG = -1e30
FORCE = 1e9
REMOVED = -3e38

LANES = 128
QT = 128
KT = 128
NC_PAD_FRONT = 16
NEAR_ROWS = 24

C_MQ, C_MK, C_MV, C_MO = 0, 512, 1024, 1536
C_NQ = 2048
C_CKV, C_SKV, C_WKV = 2560, 2816, 3072
C_GATE = 3328
C_NG = 3336
N_IN_PAD = 3456
VMEM_LIMIT = 56 * 1024 * 1024


def _cparams(sem):
    return pltpu.CompilerParams(dimension_semantics=sem, vmem_limit_bytes=VMEM_LIMIT)


def _dot(a, b):
    return jnp.dot(a, b, preferred_element_type=F32)


def _dot_nt(a, b):
    return lax.dot_general(a, b, (((1,), (1,)), ((), ())), preferred_element_type=F32)


def _dot_tn(a, b):
    return lax.dot_general(a, b, (((0,), (0,)), ((), ())), preferred_element_type=F32)


def _sigmoid(x):
    return 1.0 / (1.0 + jnp.exp(-x))


def _layer_norm(y, g, b):
    mu = jnp.mean(y, axis=-1, keepdims=True)
    d = y - mu
    var = jnp.mean(d * d, axis=-1, keepdims=True)
    return d * lax.rsqrt(var + LN_EPS) * g + b


def _ada_kernel(c_ref, w_ref, b_ref, o_ref):
    c = c_ref[...]
    a = (c * _sigmoid(c)).astype(BF16)
    o_ref[...] = _dot(a, w_ref[...]) + b_ref[...]


def ada_mod(c, w_bf, b):
    m, d = c.shape
    n = w_bf.shape[1]
    tn = 1536
    return pl.pallas_call(
        _ada_kernel,
        grid=(n // tn,),
        in_specs=[pl.BlockSpec((m, d), lambda j: (0, 0)),
                  pl.BlockSpec((d, tn), lambda j: (0, j)),
                  pl.BlockSpec((1, tn), lambda j: (0, j))],
        out_specs=pl.BlockSpec((m, tn), lambda j: (0, j)),
        out_shape=jax.ShapeDtypeStruct((m, n), F32),
        compiler_params=_cparams(("arbitrary",)),
        name="ada_mod",
    )(c, w_bf, b.reshape(1, n))


def _inproj_kernel(x_ref, sc_ref, sh_ref, w_ref, o_ref, u_scr):
    @pl.when(pl.program_id(1) == 0)
    def _():
        u_scr[...] = (x_ref[...] * (1.0 + sc_ref[0]) + sh_ref[0]).astype(BF16)

    o_ref[...] = _dot(u_scr[...], w_ref[...])


def _mod_spec(mod, tm, tiles_per_group):
    r = mod.shape[1]
    return pl.BlockSpec((1, r, D_MODEL), lambda i, j: (i // tiles_per_group, 0, 0))


def in_proj(x, sc, sh, w_bf, tm, tiles_per_group):
    m = x.shape[0]
    n = w_bf.shape[1]
    tn = 1152
    return pl.pallas_call(
        _inproj_kernel,
        grid=(m // tm, n // tn),
        in_specs=[pl.BlockSpec((tm, D_MODEL), lambda i, j: (i, 0)),
                  _mod_spec(sc, tm, tiles_per_group),
                  _mod_spec(sh, tm, tiles_per_group),
                  pl.BlockSpec((D_MODEL, tn), lambda i, j: (0, j))],
        out_specs=pl.BlockSpec((tm, tn), lambda i, j: (i, j)),
        out_shape=jax.ShapeDtypeStruct((m, n), F32),
        scratch_shapes=[pltpu.VMEM((tm, D_MODEL), BF16)],
        compiler_params=_cparams(("parallel", "arbitrary")),
        name="in_proj",
    )(x, sc, sh, w_bf)


def _outproj_kernel(x_ref, a1_ref, a2_ref, w1_ref, w2_ref, g_ref, lg_ref, lb_ref, o_ref):
    mixed = _dot(a1_ref[...], w1_ref[...]) + _dot(a2_ref[...], w2_ref[...])
    y = ALPHA * x_ref[...] + g_ref[0] * mixed
    o_ref[...] = _layer_norm(y, lg_ref[...], lb_ref[...])


def out_proj_ln(x, a1, a2, w_bf, gate, ln_g, ln_b, tm, tiles_per_group):
    m = x.shape[0]
    k1 = a1.shape[1]
    r = gate.shape[1]
    return pl.pallas_call(
        _outproj_kernel,
        grid=(m // tm,),
        in_specs=[pl.BlockSpec((tm, D_MODEL), lambda i: (i, 0)),
                  pl.BlockSpec((tm, k1), lambda i: (i, 0)),
                  pl.BlockSpec((tm, k1), lambda i: (i, 0)),
                  pl.BlockSpec((k1, D_MODEL), lambda i: (0, 0)),
                  pl.BlockSpec((k1, D_MODEL), lambda i: (1, 0)),
                  pl.BlockSpec((1, r, D_MODEL), lambda i: (i // tiles_per_group, 0, 0)),
                  pl.BlockSpec((1, D_MODEL), lambda i: (0, 0)),
                  pl.BlockSpec((1, D_MODEL), lambda i: (0, 0))],
        out_specs=pl.BlockSpec((tm, D_MODEL), lambda i: (i, 0)),
        out_shape=jax.ShapeDtypeStruct((m, D_MODEL), F32),
        compiler_params=_cparams(("parallel",)),
        name="out_proj_ln",
    )(x, a1, a2, w_bf, w_bf, gate, ln_g.reshape(1, -1), ln_b.reshape(1, -1))


def _ffn_kernel(x_ref, sc_ref, sh_ref, g_ref, wu_ref, wd_ref, lg_ref, lb_ref, o_ref, u_scr, acc_scr):
    f = pl.program_id(1)

    @pl.when(f == 0)
    def _():
        u_scr[...] = (x_ref[...] * (1.0 + sc_ref[0]) + sh_ref[0]).astype(BF16)
        acc_scr[...] = jnp.zeros_like(acc_scr)

    h = jnp.maximum(_dot(u_scr[...], wu_ref[...]), 0.0)
    acc_scr[...] += _dot((h * h).astype(BF16), wd_ref[...])

    @pl.when(f == pl.num_programs(1) - 1)
    def _():
        y = ALPHA * x_ref[...] + g_ref[0] * acc_scr[...]
        o_ref[...] = _layer_norm(y, lg_ref[...], lb_ref[...])


def ffn_ln(x, sc, sh, gate, wu_bf, wd_bf, ln_g, ln_b, tm, tiles_per_group):
    m = x.shape[0]
    tf = 512
    r = sc.shape[1]
    mod = pl.BlockSpec((1, r, D_MODEL), lambda i, f: (i // tiles_per_group, 0, 0))
    return pl.pallas_call(
        _ffn_kernel,
        grid=(m // tm, D_FF // tf),
        in_specs=[pl.BlockSpec((tm, D_MODEL), lambda i, f: (i, 0)),
                  mod, mod, mod,
                  pl.BlockSpec((D_MODEL, tf), lambda i, f: (0, f)),
                  pl.BlockSpec((tf, D_MODEL), lambda i, f: (f, 0)),
                  pl.BlockSpec((1, D_MODEL), lambda i, f: (0, 0)),
                  pl.BlockSpec((1, D_MODEL), lambda i, f: (0, 0))],
        out_specs=pl.BlockSpec((tm, D_MODEL), lambda i, f: (i, 0)),
        out_shape=jax.ShapeDtypeStruct((m, D_MODEL), F32),
        scratch_shapes=[pltpu.VMEM((tm, D_MODEL), BF16), pltpu.VMEM((tm, D_MODEL), F32)],
        compiler_params=_cparams(("parallel", "arbitrary")),
        name="ffn_ln",
    )(x, sc, sh, gate, wu_bf, wd_bf, ln_g.reshape(1, -1), ln_b.reshape(1, -1))


def _mlstm_kernel(bg_ref, q_ref, k_ref, v_ref, o_ref, g_ref, ng_ref, c0_ref, n0_ref, m0_ref,
                  h_ref, c_out, n_out, m_out, c_scr, n_scr, m_scr, *, nb, t, valid):
    ci = pl.program_id(1)

    @pl.when(ci == 0)
    def _():
        c_scr[...] = c0_ref[...]
        n_scr[...] = n0_ref[...]
        m_scr[...] = m0_ref[...]

    row = lax.broadcasted_iota(jnp.int32, (t, t), 0)
    col = lax.broadcasted_iota(jnp.int32, (t, t), 1)
    tri = col <= row
    tri_t = row <= col
    eye = row == col
    row1 = lax.broadcasted_iota(jnp.int32, (t, 1), 0)
    scale = ML_HD ** -0.5

    for b in range(nb):
        gates = g_ref[b]
        for h in range(ML_HEADS):
            hs = slice(h * ML_HD, (h + 1) * ML_HD)
            ig_col = gates[:, h:h + 1] + bg_ref[h]
            fr = gates[:, ML_HEADS + h:ML_HEADS + h + 1] + bg_ref[ML_HEADS + h]
            lf_col = jnp.minimum(fr, 0.0) - jnp.log1p(jnp.exp(-jnp.abs(fr)))
            if valid < t:
                ig_col = jnp.where(row1 < valid, ig_col, NEG)
                lf_col = jnp.where(row1 < valid, lf_col, 0.0)
            lf_row = jnp.sum(jnp.where(eye, lf_col, 0.0), axis=0, keepdims=True)
            ig_row = jnp.sum(jnp.where(eye, ig_col, 0.0), axis=0, keepdims=True)
            f_col = jnp.sum(jnp.where(tri, lf_row, 0.0), axis=1, keepdims=True)
            f_row = jnp.sum(jnp.where(tri_t, lf_col, 0.0), axis=0, keepdims=True)
            a_row = ig_row - f_row
            a_col = ig_col - f_col
            m0 = m_scr[b, h]
            cm_col = jnp.max(jnp.where(tri, a_row, NEG), axis=1, keepdims=True)
            g_col = jnp.maximum(m0, cm_col)
            dmat = jnp.exp(jnp.where(tri, a_row - g_col, NEG))
            decay_col = jnp.exp(m0 - g_col)
            m_col = f_col + g_col
            g_end = jnp.maximum(m0, jnp.max(a_row, axis=1, keepdims=True))
            f_end = jnp.sum(lf_row, axis=1, keepdims=True)
            w_end_col = jnp.exp(a_col - g_end)
            carry = jnp.exp(m0 - g_end)

            qf = q_ref[b][:, hs]
            qb = qf.astype(BF16)
            kf = k_ref[b][:, hs] * scale
            kb = kf.astype(BF16)
            vb = v_ref[b][:, hs].astype(BF16)
            cmat = c_scr[b, h]
            nvec = n_scr[b, h]

            w = _dot_nt(qb, kb) * dmat
            num = _dot(w.astype(BF16), vb) + _dot(qb, cmat.astype(BF16)) * decay_col
            den = (jnp.sum(w, axis=1, keepdims=True)
                   + jnp.sum(qf * nvec, axis=1, keepdims=True) * decay_col)
            hh = num / jnp.maximum(jnp.abs(den), jnp.exp(-m_col))

            mu = jnp.mean(hh, axis=1, keepdims=True)
            dd = hh - mu
            var = jnp.mean(dd * dd, axis=1, keepdims=True)
            hn = dd * lax.rsqrt(var + LN_EPS) * ng_ref[:, hs] * _sigmoid(o_ref[b][:, hs])
            h_ref[b, :, hs] = hn.astype(h_ref.dtype)

            kw = kf * w_end_col
            c_scr[b, h] = carry * cmat + _dot_tn(kw.astype(BF16), vb)
            n_scr[b, h] = carry * nvec + jnp.sum(kw, axis=0, keepdims=True)
            m_scr[b, h] = f_end + g_end

    @pl.when(ci == pl.num_programs(1) - 1)
    def _():
        c_out[...] = c_scr[...]
        n_out[...] = n_scr[...]
        m_out[...] = m_scr[...]


def mlstm(proj, b_gate, norm_g, c0, n0, m0, *, nb, t, valid):
    bsz, length, _ = proj.shape
    nchunk = length // t
    wide = lambda cb: pl.BlockSpec((nb, t, ML_W), lambda i, c: (i, c, cb))
    state4 = lambda s: pl.BlockSpec((nb,) + s, lambda i, c: (i, 0, 0, 0))
    kern = functools.partial(_mlstm_kernel, nb=nb, t=t, valid=valid)
    h, c1, n1, m1 = pl.pallas_call(
        kern,
        grid=(bsz // nb, nchunk),
        in_specs=[pl.BlockSpec(memory_space=pltpu.SMEM),
                  wide(C_MQ // ML_W), wide(C_MK // ML_W), wide(C_MV // ML_W), wide(C_MO // ML_W),
                  pl.BlockSpec((nb, t, LANES), lambda i, c: (i, c, C_GATE // LANES)),
                  pl.BlockSpec((1, ML_W), lambda i, c: (0, 0)),
                  state4((ML_HEADS, ML_HD, ML_HD)), state4((ML_HEADS, 1, ML_HD)), state4((ML_HEADS, 1, 1))],
        out_specs=[pl.BlockSpec((nb, t, ML_W), lambda i, c: (i, c, 0)),
                   state4((ML_HEADS, ML_HD, ML_HD)), state4((ML_HEADS, 1, ML_HD)), state4((ML_HEADS, 1, 1))],
        out_shape=[jax.ShapeDtypeStruct((bsz, length, ML_W), BF16),
                   jax.ShapeDtypeStruct((bsz, ML_HEADS, ML_HD, ML_HD), F32),
                   jax.ShapeDtypeStruct((bsz, ML_HEADS, 1, ML_HD), F32),
                   jax.ShapeDtypeStruct((bsz, ML_HEADS, 1, 1), F32)],
        scratch_shapes=[pltpu.VMEM((nb, ML_HEADS, ML_HD, ML_HD), F32),
                        pltpu.VMEM((nb, ML_HEADS, 1, ML_HD), F32),
                        pltpu.VMEM((nb, ML_HEADS, 1, 1), F32)],
        compiler_params=_cparams(("parallel", "arbitrary")),
        name="mlstm",
    )(b_gate, proj, proj, proj, proj, proj, norm_g.reshape(1, ML_W),
      c0, n0.reshape(bsz, ML_HEADS, 1, ML_HD), m0.reshape(bsz, ML_HEADS, 1, 1))
    return h, c1, n1.reshape(bsz, ML_HEADS, ML_HD), m1.reshape(bsz, ML_HEADS)


def _gelu_tanh(x):
    return 0.5 * x * (1.0 + jnp.tanh(math.sqrt(2.0 / math.pi) * (x + 0.044715 * (x * x * x))))


def _compress_kernel(*refs, npages, rows, transposed):
    jp = rows // CMP_STRIDE
    j = npages * jp
    lane = lax.broadcasted_iota(jnp.int32, (j, LANES), 1)
    low = lane < NSA_HD
    if transposed:
        pe_ref, w1_ref, w2_ref, o_ref, xk_scr, xv_scr = refs[-6:]
        page_refs = refs[-6 - npages:-6]
        for p in range(npages):
            x = page_refs[p][0, 0].T
            xk_scr[p * rows:(p + 1) * rows, :] = x[:, :KV_W]
            xv_scr[p * rows:(p + 1) * rows, :] = x[:, KV_W:]

        def chunk_phase(c, s):
            return (xk_scr, xv_scr)[s][pl.ds(c, j, stride=CMP_STRIDE), :]
    else:
        pe_ref, w1_ref, w2_ref, o_ref = refs[-4:]
        page_refs = refs[-4 - 2 * npages:-4]

        def chunk_phase(c, s):
            parts = [page_refs[2 * p + s][0, pl.ds(c, jp, stride=CMP_STRIDE), :] for p in range(npages)]
            return parts[0] if npages == 1 else jnp.concatenate(parts, axis=0)

    out = jnp.zeros((j, 2 * KV_HEADS * NSA_HD), F32)
    for s in range(2):
        acc = [jnp.zeros((j, 2 * CMP_HIDDEN), F32) for _ in range(KV_HEADS)]
        for q4 in range(CMP_STRIDE // 4):
            halves = [[], []]
            for c in (4 * q4, 4 * q4 + 2):
                va = chunk_phase(c, s)
                vb = chunk_phase(c + 1, s)
                halves[0].append(jnp.where(low, va, pltpu.roll(vb, NSA_HD, axis=1)))
                halves[1].append(jnp.where(low, pltpu.roll(va, NSA_HD, axis=1), vb))
            for g in range(KV_HEADS):
                lhs = jnp.concatenate(halves[g], axis=1).astype(BF16)
                acc[g] = acc[g] + _dot(lhs, w1_ref[s, q4])
        pe_c = jnp.zeros((16, 2 * CMP_HIDDEN), F32)
        for q4 in range(CMP_STRIDE // 4):
            pe_c = pe_c + _dot(pe_ref[s, q4], w1_ref[s, q4])
        pe_const = pe_c[0:1, :CMP_HIDDEN] + pe_c[1:2, CMP_HIDDEN:]
        for g in range(KV_HEADS):
            hid = acc[g][:, :CMP_HIDDEN] + pltpu.roll(acc[g][:, CMP_HIDDEN:], j - 1, axis=0) + pe_const
            out = out + _dot(_gelu_tanh(hid).astype(BF16), w2_ref[s * KV_HEADS + g])
    o_ref[0] = out


def _compress_weights(pe, w1, w2):
    w1r = w1.reshape(2, CMP_BLOCK, NSA_HD, CMP_HIDDEN)
    wa = w1r[:, :CMP_STRIDE].reshape(2, 4, 4 * NSA_HD, CMP_HIDDEN)
    wb = w1r[:, CMP_STRIDE:].reshape(2, 4, 4 * NSA_HD, CMP_HIDDEN)
    w1p = jnp.concatenate([wa, wb], axis=-1).astype(BF16)
    pea = pe[:, :CMP_STRIDE].reshape(2, 4, 1, 4 * NSA_HD)
    peb = pe[:, CMP_STRIDE:].reshape(2, 4, 1, 4 * NSA_HD)
    pep = jnp.concatenate([pea, peb, jnp.zeros((2, 4, 14, 4 * NSA_HD), F32)], axis=2).astype(BF16)
    w2p = jnp.zeros((2, KV_HEADS, CMP_HIDDEN, 2, KV_HEADS, NSA_HD), F32)
    for s in range(2):
        for g in range(KV_HEADS):
            w2p = w2p.at[s, g, :, s, g, :].set(w2[s])
    w2p = w2p.reshape(2 * KV_HEADS, CMP_HIDDEN, 2 * KV_W).astype(BF16)
    return pep, w1p, w2p


def compress(pages, page_index_maps, num_scalar_prefetch, prefetch, grid_n, npages, rows, cw, transposed=False):
    pep, w1p, w2p = cw
    j = npages * rows // CMP_STRIDE
    const = lambda nd: (lambda *a: (0,) * nd)
    if transposed:
        page_specs = [pl.BlockSpec((1, 1, 2 * KV_W, rows), im) for im in page_index_maps]
        scratch = [pltpu.VMEM((npages * rows, KV_W), F32)] * 2
    else:
        page_specs = [pl.BlockSpec((1, rows, KV_W), im(s)) for im in page_index_maps for s in range(2)]
        scratch = []
    grid_spec = pltpu.PrefetchScalarGridSpec(
        num_scalar_prefetch=num_scalar_prefetch,
        grid=(grid_n,),
        in_specs=page_specs
        + [pl.BlockSpec(pep.shape, const(4)), pl.BlockSpec(w1p.shape, const(4)), pl.BlockSpec(w2p.shape, const(3))],
        out_specs=pl.BlockSpec((1, j, 2 * KV_W), lambda b, *a: (b, 0, 0)),
        scratch_shapes=scratch,
    )
    return pl.pallas_call(
        functools.partial(_compress_kernel, npages=npages, rows=rows, transposed=transposed),
        grid_spec=grid_spec,
        out_shape=jax.ShapeDtypeStruct((grid_n, j, 2 * KV_W), F32),
        compiler_params=_cparams(("parallel",)),
        name="compress",
    )(*prefetch, *([pages] * len(page_specs)), pep, w1p, w2p)


def _rel_bucket(dist):
    n = jnp.maximum(dist, 0)
    max_exact = NUM_BUCKETS // 2
    nf = jnp.maximum(n, 1).astype(F32)
    large = max_exact + (jnp.log(nf / max_exact) / math.log(REL_MAX_DIST / max_exact)
                         * (NUM_BUCKETS - max_exact)).astype(jnp.int32)
    large = jnp.minimum(large, NUM_BUCKETS - 1)
    return jnp.where(n < max_exact, n, large)


def _delta_bias(rel_bias, dist):
    far = rel_bias[NUM_BUCKETS - 1]
    val = rel_bias[_rel_bucket(dist)] - far
    return jnp.where(((dist >= 0) & (dist < REL_MAX_DIST))[..., None], val, 0.0)


def _lanes_ri(tab, g):
    rows = tab.shape[0]
    return tab[:, :, g * Q_PER_KV:(g + 1) * Q_PER_KV].transpose(0, 2, 1).reshape(rows, Q_PER_KV * QT)


def _prompt_tables(rel_bias):
    i = jnp.arange(QT)[None, :]
    j = jnp.arange(KT)[:, None]
    d_diag = i - j
    diag = jnp.where((d_diag >= 0)[..., None], _delta_bias(rel_bias, d_diag), NEG)
    sub = _delta_bias(rel_bias, i + KT - j)
    n2 = jnp.arange(NEAR_ROWS)[:, None]
    near = _delta_bias(rel_bias, i + (CMP_STRIDE * NC_PAD_FRONT - CMP_BLOCK + 1) - CMP_STRIDE * n2)
    stack = lambda tab: jnp.stack([_lanes_ri(tab, g) for g in range(KV_HEADS)])
    return stack(sub), stack(diag), stack(near)


def _cover_t(n_sel, n_cmp, ncp):
    n = np.arange(ncp) - NC_PAD_FRONT
    c_start = n * CMP_STRIDE
    s_start = np.arange(n_sel)[:, None] * SEL_BLOCK
    cov = (c_start[None, :] < s_start + SEL_BLOCK) & (c_start[None, :] + CMP_BLOCK > s_start)
    cov &= ((n >= 0) & (n < n_cmp))[None, :]
    return jnp.asarray(cov, BF16)


def _nsa_prompt_kernel(q_ref, gt_ref, kc_ref, vct_ref, cov_ref, bct_ref, ks_ref, vst_ref, kw_ref, vwt_ref,
                       tsub_ref, tdiag_ref, o_ref, s_scr, p_scr, q_scr, sel_scr, m_scr, l_scr, acc_scr,
                       ocmp_scr, osel_scr, ot_scr, *, nsel, ncp, ncv):
    blk = pl.program_id(1)
    t0 = blk * QT
    lanes4 = Q_PER_KV * QT
    q_t = (q_ref[0] * (NSA_HD ** -0.5)).T

    def tile4(x):
        return jnp.concatenate([x] * Q_PER_KV, axis=1)

    def reset():
        m_scr[...] = jnp.full(m_scr.shape, NEG, F32)
        l_scr[...] = jnp.zeros(l_scr.shape, F32)
        acc_scr[...] = jnp.zeros(acc_scr.shape, F32)

    def attend(tiles):
        for g in range(KV_HEADS):
            qpad = q_scr[g]
            scores = []
            for keys, _, add_fn in tiles:
                s = _dot(keys, qpad)
                add = None if add_fn is None else add_fn(g)
                scores.append(s if add is None else s + add)
            m_old = m_scr[g]
            m_new = m_old
            for s in scores:
                m_new = jnp.maximum(m_new, jnp.max(s, axis=0, keepdims=True))
            alpha = jnp.exp(m_old - m_new)
            lsum = alpha * l_scr[g]
            acc = alpha * acc_scr[g]
            for s, (_, values_fn, _) in zip(scores, tiles):
                p = jnp.exp(s - m_new)
                lsum = lsum + jnp.sum(p, axis=0, keepdims=True)
                acc = acc + _dot(values_fn(g), p.astype(BF16))
            l_scr[g] = lsum
            acc_scr[g] = acc
            m_scr[g] = m_new

    def finish(g):
        inv = jnp.where(m_scr[g] > 0.5 * NEG, 1.0 / l_scr[g], 0.0)
        return acc_scr[g] * inv

    row_k = lax.broadcasted_iota(jnp.int32, (KT, QT), 0)
    lane_q = lax.broadcasted_iota(jnp.int32, (KT, QT), 1)

    for g in range(KV_HEADS):
        gs = slice(g * NSA_HD, (g + 1) * NSA_HD)
        qg = jnp.concatenate([q_t[(g * Q_PER_KV + r) * NSA_HD:(g * Q_PER_KV + r + 1) * NSA_HD, :]
                              for r in range(Q_PER_KV)], axis=1)
        zero = jnp.zeros_like(qg)
        qpad = jnp.concatenate([qg, zero] if g == 0 else [zero, qg], axis=0).astype(BF16)
        q_scr[g] = qpad

        s_scr[...] = _dot(kc_ref[0], qpad)
        near0 = pl.multiple_of(blk * (QT // CMP_STRIDE), 8)
        s_scr[pl.ds(near0, NEAR_ROWS), :] = s_scr[pl.ds(near0, NEAR_ROWS), :] + bct_ref[g]
        mx = jnp.full((1, lanes4), NEG, F32)
        nchunk = ncp // KT
        for c in range(nchunk):
            n = row_k + (c * KT - NC_PAD_FRONT)
            ok = (n >= 0) & (n < ncv) & (CMP_STRIDE * n + (CMP_BLOCK - 1) <= t0 + lane_q)
            sc = s_scr[c * KT:(c + 1) * KT, :] + tile4(jnp.where(ok, 0.0, NEG))
            s_scr[c * KT:(c + 1) * KT, :] = sc
            mx = jnp.maximum(mx, jnp.max(sc, axis=0, keepdims=True))
        lsum = jnp.zeros((1, lanes4), F32)
        for c in range(nchunk):
            p = jnp.exp(s_scr[c * KT:(c + 1) * KT, :] - mx)
            lsum = lsum + jnp.sum(p, axis=0, keepdims=True)
            p_scr[c * KT:(c + 1) * KT, :] = p.astype(BF16)
        inv_c = jnp.where(mx > 0.5 * NEG, 1.0 / lsum, 0.0)
        ocmp_scr[g] = _dot(vct_ref[0, gs, :], p_scr[...]) * inv_c
        imp4 = _dot(cov_ref[...], p_scr[...]) * inv_c
        imp = imp4[:, 0:QT]
        for r in range(1, Q_PER_KV):
            imp = imp + imp4[:, r * QT:(r + 1) * QT]

        jb = lax.broadcasted_iota(jnp.int32, (nsel, QT), 0)
        tq = t0 + lax.broadcasted_iota(jnp.int32, (nsel, QT), 1)
        cur = tq // SEL_BLOCK
        forced = (jb == 0) | (jb == cur) | (jb == cur - 1)
        future = jb * SEL_BLOCK > tq
        score = jnp.where(forced, FORCE, jnp.where(future, -FORCE, imp))
        chosen_any = jnp.zeros((nsel, QT), jnp.bool_)
        for _ in range(min(N_SELECT, nsel)):
            best = jnp.max(score, axis=0, keepdims=True)
            first = jnp.min(jnp.where(score == best, jb, nsel), axis=0, keepdims=True)
            chosen = jb == first
            chosen_any = chosen_any | chosen
            score = jnp.where(chosen, REMOVED, score)
        sel_scr[g] = tile4(jnp.where(chosen_any, 0.0, NEG))

    def sel_mask(g, kt):
        r0 = sel_scr[g, pl.ds(2 * kt, 1), :]
        r1 = sel_scr[g, pl.ds(2 * kt + 1, 1), :]
        return jnp.concatenate([jnp.broadcast_to(r0, (SEL_BLOCK, lanes4)),
                                jnp.broadcast_to(r1, (SEL_BLOCK, lanes4))], axis=0)

    def sel_tile(kt, table_ref=None):
        def add(g):
            return sel_mask(g, kt) if table_ref is None else sel_mask(g, kt) + table_ref[g]
        return (ks_ref[0, kt], lambda g: vst_ref[0, kt, g * NSA_HD:(g + 1) * NSA_HD, :], add)

    reset()
    n_far = jnp.maximum(blk - 1, 0)

    def far_body(i, carry):
        attend([sel_tile(2 * i), sel_tile(2 * i + 1)])
        return carry

    lax.fori_loop(0, n_far // 2, far_body, 0)

    @pl.when(blk == 0)
    def _():
        attend([sel_tile(blk, tdiag_ref)])

    @pl.when((blk >= 1) & (n_far % 2 == 0))
    def _():
        attend([sel_tile(blk - 1, tsub_ref), sel_tile(blk, tdiag_ref)])

    @pl.when(n_far % 2 == 1)
    def _():
        attend([sel_tile(blk - 2), sel_tile(blk - 1, tsub_ref), sel_tile(blk, tdiag_ref)])

    for g in range(KV_HEADS):
        osel_scr[g] = finish(g)

    def win_tile(kt, add_fn):
        return (kw_ref[0, kt], lambda g: vwt_ref[0, kt, g * NSA_HD:(g + 1) * NSA_HD, :], add_fn)

    def anti(g):
        return tile4(jnp.where(row_k > lane_q, 0.0, NEG))

    reset()

    @pl.when(blk >= 4)
    def _():
        attend([win_tile(blk - 4, anti), win_tile(blk - 3, None)])

    @pl.when(blk == 3)
    def _():
        attend([win_tile(blk - 3, None)])

    @pl.when(blk >= 2)
    def _():
        attend([win_tile(blk - 2, None), win_tile(blk - 1, lambda g: tsub_ref[g]),
                win_tile(blk, lambda g: tdiag_ref[g])])

    @pl.when(blk == 1)
    def _():
        attend([win_tile(blk - 1, lambda g: tsub_ref[g]), win_tile(blk, lambda g: tdiag_ref[g])])

    @pl.when(blk == 0)
    def _():
        attend([win_tile(blk, lambda g: tdiag_ref[g])])

    gl = gt_ref[0, 0]
    for g in range(KV_HEADS):
        o_g = (_sigmoid(gl[3 * g:3 * g + 1, :]) * ocmp_scr[g] + _sigmoid(gl[3 * g + 1:3 * g + 2, :]) * osel_scr[g]
               + _sigmoid(gl[3 * g + 2:3 * g + 3, :]) * finish(g))
        for r in range(Q_PER_KV):
            h = g * Q_PER_KV + r
            ot_scr[h * NSA_HD:(h + 1) * NSA_HD, :] = o_g[:, r * QT:(r + 1) * QT]

    o_ref[0] = ot_scr[...].T.astype(o_ref.dtype)


def nsa_prompt(proj, ckv, rel_tabs):
    bsz, seq, _ = proj.shape
    nq = seq // QT
    n_cmp = (seq - CMP_BLOCK) // CMP_STRIDE + 1
    nsel = seq // SEL_BLOCK
    ncp = -(-(NC_PAD_FRONT + seq // CMP_STRIDE) // KT) * KT
    tsub, tdiag, bct = rel_tabs

    def tiles(cols):
        return proj[:, :, cols:cols + KV_W].astype(BF16).reshape(bsz, nq, KT, KV_W)

    ks = tiles(C_SKV)
    vst = tiles(C_SKV + KV_W).transpose(0, 1, 3, 2)
    kw = tiles(C_WKV)
    vwt = tiles(C_WKV + KV_W).transpose(0, 1, 3, 2)
    back = ncp - NC_PAD_FRONT - ckv.shape[1]
    kc = jnp.pad(ckv[:, :, :KV_W], ((0, 0), (NC_PAD_FRONT, back), (0, 0))).astype(BF16)
    vct = jnp.pad(ckv[:, :, KV_W:], ((0, 0), (NC_PAD_FRONT, back), (0, 0))).astype(BF16).transpose(0, 2, 1)
    cov = _cover_t(nsel, n_cmp, ncp)
    ng = proj[:, :, C_NG:C_NG + 3 * NSA_HEADS].reshape(bsz, nq, QT, KV_HEADS, Q_PER_KV, 3)
    gt = ng.transpose(0, 1, 3, 5, 4, 2).reshape(bsz, nq, KV_HEADS * 3, Q_PER_KV * QT)
    gt = jnp.pad(gt, ((0, 0), (0, 0), (0, 2), (0, 0)))

    whole = lambda a: pl.BlockSpec((1,) + a.shape[1:], lambda b, i: (b,) + (0,) * (a.ndim - 1))
    const = lambda a: pl.BlockSpec(a.shape, lambda b, i: (0,) * a.ndim)
    lanes4 = Q_PER_KV * QT
    return pl.pallas_call(
        functools.partial(_nsa_prompt_kernel, nsel=nsel, ncp=ncp, ncv=n_cmp),
        grid=(bsz, nq),
        in_specs=[pl.BlockSpec((1, QT, NSA_W), lambda b, i: (b, i, C_NQ // NSA_W)),
                  pl.BlockSpec((1, 1, 8, lanes4), lambda b, i: (b, i, 0, 0)),
                  whole(kc), whole(vct), const(cov), const(bct),
                  whole(ks), whole(vst), whole(kw), whole(vwt), const(tsub), const(tdiag)],
        out_specs=pl.BlockSpec((1, QT, NSA_W), lambda b, i: (b, i, 0)),
        out_shape=jax.ShapeDtypeStruct((bsz, seq, NSA_W), BF16),
        scratch_shapes=[pltpu.VMEM((ncp, lanes4), F32), pltpu.VMEM((ncp, lanes4), BF16),
                        pltpu.VMEM((KV_HEADS, KV_W, lanes4), BF16),
                        pltpu.VMEM((KV_HEADS, nsel, lanes4), F32),
                        pltpu.VMEM((KV_HEADS, 1, lanes4), F32), pltpu.VMEM((KV_HEADS, 1, lanes4), F32),
                        pltpu.VMEM((KV_HEADS, NSA_HD, lanes4), F32), pltpu.VMEM((KV_HEADS, NSA_HD, lanes4), F32),
                        pltpu.VMEM((KV_HEADS, NSA_HD, lanes4), F32), pltpu.VMEM((NSA_W, QT), F32)],
        compiler_params=_cparams(("parallel", "arbitrary")),
        name="nsa_prompt",
    )(proj, gt, kc, vct, cov, bct, ks, vst, kw, vwt, tsub, tdiag)


S_ROWS = Q_PER_KV * KV_HEADS * 4
NEW_PAD = 16


def _sample_tables(rel_bias, past, t_new, n_cmp, win_len):
    row = jnp.arange(S_ROWS)
    i = (row % t_new)[:, None]
    g = (row // t_new) % KV_HEADS
    r = row // (t_new * KV_HEADS)
    head = g * Q_PER_KV + r

    def pick(tab):
        return jnp.take_along_axis(tab, head[:, None, None], axis=2)[..., 0]

    n = jnp.arange(LANES)[None, :]
    bcs = jnp.where(n < n_cmp, pick(_delta_bias(rel_bias, past + i - (CMP_STRIDE * n + CMP_BLOCK - 1))), NEG)
    jj = jnp.arange(LANES)[None, :]
    last = pick(_delta_bias(rel_bias, LANES + i - jj))
    w_old = jnp.where(jj > i, 0.0, NEG)
    j2 = jnp.arange(NEW_PAD)[None, :]
    d_new = i - j2
    new = jnp.where((j2 < t_new) & (d_new >= 0), pick(_delta_bias(rel_bias, d_new)), NEG)
    return bcs, jnp.stack([last, w_old]), new


def _nsa_sample_kernel(*refs, npages, past, t_new):
    pt_ref, q_ref, g_ref, kc_ref = refs[:4]
    page_refs = refs[4:4 + npages]
    win_ref, news_ref, neww_ref, bcs_ref, tab_ref, tnew_ref, cov_ref, o_ref, s_scr = refs[4 + npages:]
    del pt_ref
    qb = (q_ref[0] * (NSA_HD ** -0.5)).astype(BF16)

    def softmax_pv(tiles):
        mx = jnp.full((S_ROWS, 1), NEG, F32)
        for c, (score_fn, _, add_fn, width) in enumerate(tiles):
            s = score_fn()
            if add_fn is not None:
                s = s + add_fn()
            s_scr[:, c * LANES:c * LANES + width] = s
            mx = jnp.maximum(mx, s.max(axis=1, keepdims=True))
        lsum = jnp.zeros((S_ROWS, 1), F32)
        acc = jnp.zeros((S_ROWS, KV_W), F32)
        for c, (_, pv_fn, _, width) in enumerate(tiles):
            p = jnp.exp(s_scr[:, c * LANES:c * LANES + width] - mx)
            lsum = lsum + p.sum(axis=1, keepdims=True)
            acc = acc + pv_fn(p.astype(BF16))
        return acc / lsum

    def cached_tile(ref, lanes, add_fn):
        return (lambda: _dot(qb, ref[0, 0, :KV_W, lanes].astype(BF16)),
                lambda p: _dot_nt(p, ref[0, 0, KV_W:, lanes].astype(BF16)), add_fn, LANES)

    def new_tile(ref, add_fn):
        return (lambda: _dot_nt(qb, ref[0, :, :KV_W].astype(BF16)),
                lambda p: _dot(p, ref[0, :, KV_W:].astype(BF16)), add_fn, NEW_PAD)

    kc = kc_ref[0]
    s_c = _dot_nt(qb, kc[:, :KV_W].astype(BF16)) + bcs_ref[...]
    p_c = jnp.exp(s_c - s_c.max(axis=1, keepdims=True))
    p_c = (p_c / p_c.sum(axis=1, keepdims=True)).astype(BF16)
    o_cmp = _dot(p_c, kc[:, KV_W:].astype(BF16))
    imp_r = _dot(p_c, cov_ref[...])
    rows_gi = KV_HEADS * t_new
    imp = imp_r
    for r in range(1, Q_PER_KV):
        imp = imp + pltpu.roll(imp_r, r * rows_gi, axis=0)

    jb = lax.broadcasted_iota(jnp.int32, (S_ROWS, LANES), 1)
    qpos = past + lax.broadcasted_iota(jnp.int32, (S_ROWS, LANES), 0) % t_new
    cur = qpos // SEL_BLOCK
    forced = (jb == 0) | (jb == cur) | (jb == cur - 1)
    future = jb * SEL_BLOCK > qpos
    score = jnp.where(forced, FORCE, jnp.where(future, -FORCE, imp))
    chosen_any = jnp.zeros((S_ROWS, LANES), jnp.bool_)
    for _ in range(N_SELECT):
        best = jnp.max(score, axis=1, keepdims=True)
        first = jnp.min(jnp.where(score == best, jb, LANES), axis=1, keepdims=True)
        chosen = jb == first
        chosen_any = chosen_any | chosen
        score = jnp.where(chosen, REMOVED, score)
    sel = jnp.where(chosen_any, 1.0, 0.0).astype(BF16)

    def block_mask(j0, nkeys=LANES):
        jrow = lax.broadcasted_iota(jnp.int32, (LANES, nkeys), 0)
        kcol = lax.broadcasted_iota(jnp.int32, (LANES, nkeys), 1)
        expand = jnp.where(jrow == j0 + kcol // SEL_BLOCK, 1.0, 0.0).astype(BF16)
        return jnp.where(_dot(sel, expand) > 0.5, 0.0, NEG)

    jn = past // SEL_BLOCK
    tiles = []
    for p in range(npages):
        if p == npages - 1:
            add = lambda p=p: block_mask(2 * p) + tab_ref[0]
        else:
            add = lambda p=p: block_mask(2 * p)
        tiles.append(cached_tile(page_refs[p], slice(None), add))
    tiles.append(new_tile(news_ref, lambda: tnew_ref[...] + block_mask(jn, NEW_PAD)))
    o_sel = softmax_pv(tiles)

    nwin = win_ref.shape[3] // LANES
    tiles = []
    for c in range(nwin):
        if c == 0:
            add = lambda: tab_ref[1]
        elif c == nwin - 1:
            add = lambda: tab_ref[0]
        else:
            add = None
        tiles.append(cached_tile(win_ref, slice(c * LANES, (c + 1) * LANES), add))
    tiles.append(new_tile(neww_ref, lambda: tnew_ref[...]))
    o_win = softmax_pv(tiles)

    gl = g_ref[0]
    o_ref[0] = (_sigmoid(gl[:, 0:1]) * o_cmp + _sigmoid(gl[:, 1:2]) * o_sel + _sigmoid(gl[:, 2:3]) * o_win)


def nsa_sample(proj, ckv, pool_s, win_buf, layer, page_table, rel_bias):
    bsz, t_new, _ = proj.shape
    npages = page_table.shape[1]
    past = npages * PAGE
    n_cmp = (past + t_new - CMP_BLOCK) // CMP_STRIDE + 1
    n_sel = -(-(past + t_new) // SEL_BLOCK)
    assert n_cmp <= LANES and n_sel <= LANES and win_buf.shape[3] == WINDOW and t_new == 4
    assert (past + t_new - 1) // SEL_BLOCK == past // SEL_BLOCK
    bcs, tab, tnew = _sample_tables(rel_bias, past, t_new, n_cmp, win_buf.shape[3])
    n = np.arange(LANES)
    jsel = np.arange(LANES)
    cov = ((n[:, None] * CMP_STRIDE < jsel[None, :] * SEL_BLOCK + SEL_BLOCK)
           & (n[:, None] * CMP_STRIDE + CMP_BLOCK > jsel[None, :] * SEL_BLOCK)
           & (n[:, None] < n_cmp) & (jsel[None, :] < n_sel))
    cov = jnp.asarray(cov, BF16)

    eye = jnp.eye(KV_HEADS, dtype=F32)
    q5 = proj[:, :, C_NQ:C_NQ + NSA_W].reshape(bsz, t_new, KV_HEADS, Q_PER_KV, NSA_HD).transpose(0, 3, 2, 1, 4)
    qpad = (q5[:, :, :, :, None, :] * eye[None, None, :, None, :, None]).reshape(bsz, S_ROWS, KV_W)
    ng = proj[:, :, C_NG:C_NG + 3 * NSA_HEADS].reshape(bsz, t_new, KV_HEADS, Q_PER_KV, 3).transpose(0, 3, 2, 1, 4)
    gl = jnp.pad(ng.reshape(bsz, S_ROWS, 3), ((0, 0), (0, 0), (0, LANES - 3)))
    pad_new = lambda c: jnp.pad(proj[:, :, c:c + 2 * KV_W], ((0, 0), (0, NEW_PAD - t_new), (0, 0)))

    per_b = lambda a: pl.BlockSpec((1,) + a.shape[1:], lambda b, pt: (b,) + (0,) * (a.ndim - 1))
    const = lambda a: pl.BlockSpec(a.shape, lambda b, pt: (0,) * a.ndim)
    new_s, new_w = pad_new(C_SKV), pad_new(C_WKV)
    grid_spec = pltpu.PrefetchScalarGridSpec(
        num_scalar_prefetch=1,
        grid=(bsz,),
        in_specs=[per_b(qpad), per_b(gl), per_b(ckv)]
        + [pl.BlockSpec((1, 1, 2 * KV_W, PAGE), (lambda b, pt, p=p: (layer, pt[b, p], 0, 0))) for p in range(npages)]
        + [pl.BlockSpec((1, 1) + win_buf.shape[2:], lambda b, pt: (layer, b, 0, 0)),
           per_b(new_s), per_b(new_w), const(bcs), const(tab), const(tnew), const(cov)],
        out_specs=pl.BlockSpec((1, S_ROWS, KV_W), lambda b, pt: (b, 0, 0)),
        scratch_shapes=[pltpu.VMEM((S_ROWS, (npages + 1) * LANES), F32)],
    )
    out = pl.pallas_call(
        functools.partial(_nsa_sample_kernel, npages=npages, past=past, t_new=t_new),
        grid_spec=grid_spec,
        out_shape=jax.ShapeDtypeStruct((bsz, S_ROWS, KV_W), F32),
        compiler_params=_cparams(("parallel",)),
        name="nsa_sample",
    )(page_table, qpad, gl, ckv, *([pool_s] * npages), win_buf, new_s, new_w, bcs, tab, tnew, cov)
    o6 = out.reshape(bsz, Q_PER_KV, KV_HEADS, t_new, KV_HEADS, NSA_HD)
    o5 = jnp.stack([o6[:, :, g, :, g, :] for g in range(KV_HEADS)], axis=2)
    return o5.transpose(0, 3, 2, 1, 4).reshape(bsz, t_new, NSA_W)


def _in_proj_perm():
    sizes = (ML_W, ML_W, ML_W, ML_W, ML_HEADS, ML_HEADS, NSA_W, KV_W, KV_W, KV_W, KV_W, KV_W, KV_W, 3 * NSA_HEADS)
    off = np.concatenate([[0], np.cumsum(sizes)])
    order = (0, 1, 2, 3, 6, 7, 8, 9, 10, 11, 12, 4, 5, 13)
    return np.concatenate([np.arange(off[k], off[k + 1]) for k in order])


def _layer(x, mods, proj_fn_args, lw, tm, tiles_per_group, mixer):
    sh1, sc1, g1, sh2, sc2, g2 = mods
    w_in_bf, w_out_bf, w_up_bf, w_down_bf, ln_g, ln_b = lw
    proj = in_proj(x, sc1, sh1, w_in_bf, tm, tiles_per_group)
    h_ml, o_nsa, extras = mixer(proj)
    x1 = out_proj_ln(x, h_ml, o_nsa, w_out_bf, g1, ln_g[0], ln_b[0], tm, tiles_per_group)
    x2 = ffn_ln(x1, sc2, sh2, g2, w_up_bf, w_down_bf, ln_g[1], ln_b[1], tm, tiles_per_group)
    return x2, proj, extras


def kernel(x_prompt, x_sample, cache_cmp_kv, cache_slc_kv, cache_win_kv, state_mlstm_C, state_mlstm_n,
           state_mlstm_m, page_table, c_prompt, c_sample, w_ada, b_ada, w_in, b_gate, ml_norm_g, cmp_pe,
           cmp_w1, cmp_w2, rel_bias, w_out, ln_g, ln_b, w_up, w_down):
    bsz, seq, d = x_prompt.shape
    bs, ts, _ = x_sample.shape
    depth = w_in.shape[0]
    n_phys = cache_cmp_kv.shape[1]
    npages = page_table.shape[1]
    win_len = cache_win_kv.shape[2]
    tm = 512
    t_ml = 128
    t_pad = 16

    rows_last = lambda a: a.transpose(0, 1, 3, 4, 5, 2).reshape(a.shape[0], a.shape[1], 2 * KV_W, a.shape[2])
    pool_c, pool_s, win_t = rows_last(cache_cmp_kv), rows_last(cache_slc_kv), rows_last(cache_win_kv)

    perm = _in_proj_perm()
    prompt_tabs = _prompt_tables(rel_bias)
    nc_rows = bsz + bs
    c_all = jnp.pad(jnp.concatenate([c_prompt, c_sample], axis=0), ((0, -nc_rows % 8), (0, 0)))

    xp = x_prompt.reshape(bsz * seq, d)
    xs = x_sample.reshape(bs * ts, d)
    outs = {k: [] for k in ("cmp_p", "cmp_s", "slc_p", "slc_s", "win_p", "win_s",
                            "C_p", "C_s", "n_p", "n_s", "m_p", "m_s")}
    kv_shape = lambda b, t: (b, t, 2, KV_HEADS, NSA_HD)

    for l in range(depth):
        ada = ada_mod(c_all, w_ada[l].astype(BF16), b_ada[l]).reshape(c_all.shape[0], 6, d)
        mods_p = [ada[:bsz, k][:, None, :] for k in range(6)]
        mods_s = [jnp.repeat(ada[bsz:nc_rows, k], ts, axis=0)[None] for k in range(6)]
        w_in_bf = jnp.pad(w_in[l][:, perm], ((0, 0), (0, N_IN_PAD - perm.size))).astype(BF16)
        lw = (w_in_bf, w_out[l].astype(BF16), w_up[l].astype(BF16), w_down[l].astype(BF16), ln_g[l], ln_b[l])
        cw = _compress_weights(cmp_pe[l], cmp_w1[l], cmp_w2[l])

        def prompt_mixer(proj):
            proj = proj.reshape(bsz, seq, N_IN_PAD)
            h_ml, c1, n1, m1 = mlstm(proj, b_gate[l], ml_norm_g[l],
                                     jnp.zeros((bsz, ML_HEADS, ML_HD, ML_HD), F32),
                                     jnp.zeros((bsz, ML_HEADS, ML_HD), F32), jnp.zeros((bsz, ML_HEADS), F32),
                                     nb=bsz, t=t_ml, valid=t_ml)
            ckv = compress(proj, [lambda s: (lambda b: (b, 0, C_CKV // KV_W + s))], 0, (), bsz, 1, seq, cw)
            o_nsa = nsa_prompt(proj, ckv, prompt_tabs)
            return h_ml.reshape(bsz * seq, ML_W), o_nsa.reshape(bsz * seq, NSA_W), (c1, n1, m1)

        def sample_mixer(proj):
            proj = proj.reshape(bs, ts, N_IN_PAD)
            proj_pad = jnp.pad(proj, ((0, 0), (0, t_pad - ts), (0, 0)))
            h_ml, c1, n1, m1 = mlstm(proj_pad, b_gate[l], ml_norm_g[l], state_mlstm_C[l], state_mlstm_n[l],
                                     state_mlstm_m[l], nb=8, t=t_pad, valid=ts)
            pages = [(lambda b, pt, p=p: (l, pt[b, p], 0, 0)) for p in range(npages)]
            ckv = compress(pool_c, pages, 1, (page_table,), bs, npages, PAGE, cw, transposed=True)
            o_nsa = nsa_sample(proj, ckv, pool_s, win_t, l, page_table, rel_bias)
            return (h_ml[:, :ts].reshape(bs * ts, ML_W), o_nsa.reshape(bs * ts, NSA_W).astype(BF16),
                    (c1, n1, m1))

        xp, proj_p, st_p = _layer(xp, mods_p, None, lw, tm, seq // tm, prompt_mixer)
        xs, proj_s, st_s = _layer(xs, mods_s, None, lw, bs * ts, 1, sample_mixer)

        proj_p = proj_p.reshape(bsz, seq, N_IN_PAD)
        proj_s = proj_s.reshape(bs, ts, N_IN_PAD)
        rows = lambda proj, c: proj[:, :, c:c + 2 * KV_W]
        win = min(WINDOW, seq)
        outs["cmp_p"].append(rows(proj_p, C_CKV).reshape(kv_shape(bsz, seq)))
        outs["slc_p"].append(rows(proj_p, C_SKV).reshape(kv_shape(bsz, seq)))
        outs["win_p"].append(rows(proj_p, C_WKV)[:, seq - win:].reshape(kv_shape(bsz, win)))
        outs["cmp_s"].append(rows(proj_s, C_CKV).reshape(kv_shape(bs, ts)))
        outs["slc_s"].append(rows(proj_s, C_SKV).reshape(kv_shape(bs, ts)))
        win_all = jnp.concatenate([cache_win_kv[l], rows(proj_s, C_WKV).reshape(kv_shape(bs, ts))], axis=1)
        outs["win_s"].append(win_all[:, -win_len:])
        for tag, st in (("p", st_p), ("s", st_s)):
            outs["C_" + tag].append(st[0])
            outs["n_" + tag].append(st[1])
            outs["m_" + tag].append(st[2])

    stk = lambda k: jnp.stack(outs[k])
    return (xp.reshape(bsz, seq, d), xs.reshape(bs, ts, d),
            stk("cmp_p"), stk("cmp_s"), stk("slc_p"), stk("slc_s"), stk("win_p"), stk("win_s"),
            stk("C_p"), stk("C_s"), stk("n_p"), stk("n_s"), stk("m_p"), stk("m_s"))
```

```python
import functools
import math

import numpy as np
import jax
import jax.numpy as jnp
from jax import lax
from jax.experimental import pallas as pl
from jax.experimental.pallas import tpu as pltpu

F32 = jnp.float32
BF16 = jnp.bfloat16

D_MODEL = 1024
DEPTH = 2
PAGE = 128
ML_HEADS = 4
ML_HD = 128
ML_W = ML_HEADS * ML_HD
NSA_HEADS = 8
NSA_HD = 64
NSA_W = NSA_HEADS * NSA_HD
KV_HEADS = 2
Q_PER_KV = 4
KV_W = KV_HEADS * NSA_HD
CMP_BLOCK = 32
CMP_STRIDE = 16
CMP_HIDDEN = 256
SEL_BLOCK = 64
N_SELECT = 16
WINDOW = 512
NUM_BUCKETS = 32
REL_MAX_DIST = 128
D_FF = 4 * D_MODEL
ALPHA = (2 * DEPTH) ** 0.25
LN_EPS = 1e-5
NEG = -1e30
FORCE = 1e9
REMOVED = -3e38
LOG2E = math.log2(math.e)

LANES = 128
QT = 128
KT = 128
V_ROWS = 80
NC_PAD_FRONT = 16
NEAR_ROWS = 24

C_MQ, C_MK, C_MV, C_MO = 0, 512, 1024, 1536
C_NQ = 2048
C_CKV, C_SKV, C_WKV = 2560, 2816, 3072
C_GATE = 3328
C_NG = 3336
N_IN_PAD = 3456
VMEM_LIMIT = 56 * 1024 * 1024


def _cparams(sem):
    return pltpu.CompilerParams(dimension_semantics=sem, vmem_limit_bytes=VMEM_LIMIT)


def _dot(a, b):
    return jnp.dot(a, b, preferred_element_type=F32)


def _dot_nt(a, b):
    return lax.dot_general(a, b, (((1,), (1,)), ((), ())), preferred_element_type=F32)


def _dot_tn(a, b):
    return lax.dot_general(a, b, (((0,), (0,)), ((), ())), preferred_element_type=F32)


def _sigmoid(x):
    return 1.0 / (1.0 + jnp.exp(-x))


def _layer_norm(y, g, b):
    mu = jnp.mean(y, axis=-1, keepdims=True)
    d = y - mu
    var = jnp.mean(d * d, axis=-1, keepdims=True)
    return d * lax.rsqrt(var + LN_EPS) * g + b


def _ada_kernel(c_ref, w_ref, b_ref, o_ref):
    c = c_ref[...]
    a = (c * _sigmoid(c)).astype(BF16)
    o_ref[...] = _dot(a, w_ref[...]) + b_ref[...]


def ada_mod(c, w_bf, b):
    m, d = c.shape
    n = w_bf.shape[1]
    tn = 1536
    return pl.pallas_call(
        _ada_kernel,
        grid=(n // tn,),
        in_specs=[pl.BlockSpec((m, d), lambda j: (0, 0)),
                  pl.BlockSpec((d, tn), lambda j: (0, j)),
                  pl.BlockSpec((1, tn), lambda j: (0, j))],
        out_specs=pl.BlockSpec((m, tn), lambda j: (0, j)),
        out_shape=jax.ShapeDtypeStruct((m, n), F32),
        compiler_params=_cparams(("arbitrary",)),
        name="ada_mod",
    )(c, w_bf, b.reshape(1, n))


def _inproj_kernel(x_ref, sc_ref, sh_ref, w_ref, o_ref, u_scr):
    @pl.when(pl.program_id(1) == 0)
    def _():
        u_scr[...] = (x_ref[...] * (1.0 + sc_ref[0]) + sh_ref[0]).astype(BF16)

    o_ref[...] = _dot(u_scr[...], w_ref[...])


def _mod_spec(mod, tm, tiles_per_group):
    r = mod.shape[1]
    return pl.BlockSpec((1, r, D_MODEL), lambda i, j: (i // tiles_per_group, 0, 0))


def in_proj(x, sc, sh, w_bf, tm, tiles_per_group):
    m = x.shape[0]
    n = w_bf.shape[1]
    tn = 1152
    return pl.pallas_call(
        _inproj_kernel,
        grid=(m // tm, n // tn),
        in_specs=[pl.BlockSpec((tm, D_MODEL), lambda i, j: (i, 0)),
                  _mod_spec(sc, tm, tiles_per_group),
                  _mod_spec(sh, tm, tiles_per_group),
                  pl.BlockSpec((D_MODEL, tn), lambda i, j: (0, j))],
        out_specs=pl.BlockSpec((tm, tn), lambda i, j: (i, j)),
        out_shape=jax.ShapeDtypeStruct((m, n), F32),
        scratch_shapes=[pltpu.VMEM((tm, D_MODEL), BF16)],
        compiler_params=_cparams(("parallel", "arbitrary")),
        name="in_proj",
    )(x, sc, sh, w_bf)


def _outproj_kernel(x_ref, a1_ref, a2_ref, w1_ref, w2_ref, g_ref, lg_ref, lb_ref, o_ref):
    mixed = _dot(a1_ref[...], w1_ref[...]) + _dot(a2_ref[...], w2_ref[...])
    y = ALPHA * x_ref[...] + g_ref[0] * mixed
    o_ref[...] = _layer_norm(y, lg_ref[...], lb_ref[...])


def out_proj_ln(x, a1, a2, w_bf, gate, ln_g, ln_b, tm, tiles_per_group):
    m = x.shape[0]
    k1 = a1.shape[1]
    r = gate.shape[1]
    return pl.pallas_call(
        _outproj_kernel,
        grid=(m // tm,),
        in_specs=[pl.BlockSpec((tm, D_MODEL), lambda i: (i, 0)),
                  pl.BlockSpec((tm, k1), lambda i: (i, 0)),
                  pl.BlockSpec((tm, k1), lambda i: (i, 0)),
                  pl.BlockSpec((k1, D_MODEL), lambda i: (0, 0)),
                  pl.BlockSpec((k1, D_MODEL), lambda i: (1, 0)),
                  pl.BlockSpec((1, r, D_MODEL), lambda i: (i // tiles_per_group, 0, 0)),
                  pl.BlockSpec((1, D_MODEL), lambda i: (0, 0)),
                  pl.BlockSpec((1, D_MODEL), lambda i: (0, 0))],
        out_specs=pl.BlockSpec((tm, D_MODEL), lambda i: (i, 0)),
        out_shape=jax.ShapeDtypeStruct((m, D_MODEL), F32),
        compiler_params=_cparams(("parallel",)),
        name="out_proj_ln",
    )(x, a1, a2, w_bf, w_bf, gate, ln_g.reshape(1, -1), ln_b.reshape(1, -1))


def _ffn_kernel(x_ref, sc_ref, sh_ref, g_ref, wu_ref, wd_ref, lg_ref, lb_ref, o_ref, u_scr, acc_scr):
    f = pl.program_id(1)

    @pl.when(f == 0)
    def _():
        u_scr[...] = (x_ref[...] * (1.0 + sc_ref[0]) + sh_ref[0]).astype(BF16)
        acc_scr[...] = jnp.zeros_like(acc_scr)

    h = jnp.maximum(_dot(u_scr[...], wu_ref[...]), 0.0)
    acc_scr[...] += _dot((h * h).astype(BF16), wd_ref[...])

    @pl.when(f == pl.num_programs(1) - 1)
    def _():
        y = ALPHA * x_ref[...] + g_ref[0] * acc_scr[...]
        o_ref[...] = _layer_norm(y, lg_ref[...], lb_ref[...])


def ffn_ln(x, sc, sh, gate, wu_bf, wd_bf, ln_g, ln_b, tm, tiles_per_group):
    m = x.shape[0]
    tf = 512
    r = sc.shape[1]
    mod = pl.BlockSpec((1, r, D_MODEL), lambda i, f: (i // tiles_per_group, 0, 0))
    return pl.pallas_call(
        _ffn_kernel,
        grid=(m // tm, D_FF // tf),
        in_specs=[pl.BlockSpec((tm, D_MODEL), lambda i, f: (i, 0)),
                  mod, mod, mod,
                  pl.BlockSpec((D_MODEL, tf), lambda i, f: (0, f)),
                  pl.BlockSpec((tf, D_MODEL), lambda i, f: (f, 0)),
                  pl.BlockSpec((1, D_MODEL), lambda i, f: (0, 0)),
                  pl.BlockSpec((1, D_MODEL), lambda i, f: (0, 0))],
        out_specs=pl.BlockSpec((tm, D_MODEL), lambda i, f: (i, 0)),
        out_shape=jax.ShapeDtypeStruct((m, D_MODEL), F32),
        scratch_shapes=[pltpu.VMEM((tm, D_MODEL), BF16), pltpu.VMEM((tm, D_MODEL), F32)],
        compiler_params=_cparams(("parallel", "arbitrary")),
        name="ffn_ln",
    )(x, sc, sh, gate, wu_bf, wd_bf, ln_g.reshape(1, -1), ln_b.reshape(1, -1))


def _mlstm_kernel(bg_ref, q_ref, k_ref, v_ref, o_ref, g_ref, ng_ref, c0_ref, n0_ref, m0_ref,
                  h_ref, c_out, n_out, m_out, c_scr, n_scr, m_scr, *, nb, t, valid):
    ci = pl.program_id(1)

    @pl.when(ci == 0)
    def _():
        c_scr[...] = c0_ref[...]
        n_scr[...] = n0_ref[...]
        m_scr[...] = m0_ref[...]

    row = lax.broadcasted_iota(jnp.int32, (t, t), 0)
    col = lax.broadcasted_iota(jnp.int32, (t, t), 1)
    tri = col <= row
    tri_t = row <= col
    eye = row == col
    row1 = lax.broadcasted_iota(jnp.int32, (t, 1), 0)
    scale = ML_HD ** -0.5

    for b in range(nb):
        gates = g_ref[b]
        for h in range(ML_HEADS):
            hs = slice(h * ML_HD, (h + 1) * ML_HD)
            ig_col = gates[:, h:h + 1] + bg_ref[h]
            fr = gates[:, ML_HEADS + h:ML_HEADS + h + 1] + bg_ref[ML_HEADS + h]
            lf_col = jnp.minimum(fr, 0.0) - jnp.log1p(jnp.exp(-jnp.abs(fr)))
            if valid < t:
                ig_col = jnp.where(row1 < valid, ig_col, NEG)
                lf_col = jnp.where(row1 < valid, lf_col, 0.0)
            lf_row = jnp.sum(jnp.where(eye, lf_col, 0.0), axis=0, keepdims=True)
            ig_row = jnp.sum(jnp.where(eye, ig_col, 0.0), axis=0, keepdims=True)
            f_col = jnp.sum(jnp.where(tri, lf_row, 0.0), axis=1, keepdims=True)
            f_row = jnp.sum(jnp.where(tri_t, lf_col, 0.0), axis=0, keepdims=True)
            a_row = ig_row - f_row
            a_col = ig_col - f_col
            m0 = m_scr[b, h]
            cm_col = jnp.max(jnp.where(tri, a_row, NEG), axis=1, keepdims=True)
            g_col = jnp.maximum(m0, cm_col)
            dmat = jnp.exp(jnp.where(tri, a_row - g_col, NEG))
            decay_col = jnp.exp(m0 - g_col)
            m_col = f_col + g_col
            g_end = jnp.maximum(m0, jnp.max(a_row, axis=1, keepdims=True))
            f_end = jnp.sum(lf_row, axis=1, keepdims=True)
            w_end_col = jnp.exp(a_col - g_end)
            carry = jnp.exp(m0 - g_end)

            qf = q_ref[b][:, hs]
            qb = qf.astype(BF16)
            kf = k_ref[b][:, hs] * scale
            kb = kf.astype(BF16)
            vb = v_ref[b][:, hs].astype(BF16)
            cmat = c_scr[b, h]
            nvec = n_scr[b, h]

            w = _dot_nt(qb, kb) * dmat
            num = _dot(w.astype(BF16), vb) + _dot(qb, cmat.astype(BF16)) * decay_col
            den = (jnp.sum(w, axis=1, keepdims=True)
                   + jnp.sum(qf * nvec, axis=1, keepdims=True) * decay_col)
            hh = num / jnp.maximum(jnp.abs(den), jnp.exp(-m_col))

            mu = jnp.mean(hh, axis=1, keepdims=True)
            dd = hh - mu
            var = jnp.mean(dd * dd, axis=1, keepdims=True)
            hn = dd * lax.rsqrt(var + LN_EPS) * ng_ref[:, hs] * _sigmoid(o_ref[b][:, hs])
            h_ref[b, :, hs] = hn.astype(h_ref.dtype)

            kw = kf * w_end_col
            c_scr[b, h] = carry * cmat + _dot_tn(kw.astype(BF16), vb)
            n_scr[b, h] = carry * nvec + jnp.sum(kw, axis=0, keepdims=True)
            m_scr[b, h] = f_end + g_end

    @pl.when(ci == pl.num_programs(1) - 1)
    def _():
        c_out[...] = c_scr[...]
        n_out[...] = n_scr[...]
        m_out[...] = m_scr[...]


def mlstm(proj, b_gate, norm_g, c0, n0, m0, *, nb, t, valid):
    bsz, length, _ = proj.shape
    nchunk = length // t
    wide = lambda cb: pl.BlockSpec((nb, t, ML_W), lambda i, c: (i, c, cb))
    state4 = lambda s: pl.BlockSpec((nb,) + s, lambda i, c: (i, 0, 0, 0))
    kern = functools.partial(_mlstm_kernel, nb=nb, t=t, valid=valid)
    h, c1, n1, m1 = pl.pallas_call(
        kern,
        grid=(bsz // nb, nchunk),
        in_specs=[pl.BlockSpec(memory_space=pltpu.SMEM),
                  wide(C_MQ // ML_W), wide(C_MK // ML_W), wide(C_MV // ML_W), wide(C_MO // ML_W),
                  pl.BlockSpec((nb, t, LANES), lambda i, c: (i, c, C_GATE // LANES)),
                  pl.BlockSpec((1, ML_W), lambda i, c: (0, 0)),
                  state4((ML_HEADS, ML_HD, ML_HD)), state4((ML_HEADS, 1, ML_HD)), state4((ML_HEADS, 1, 1))],
        out_specs=[pl.BlockSpec((nb, t, ML_W), lambda i, c: (i, c, 0)),
                   state4((ML_HEADS, ML_HD, ML_HD)), state4((ML_HEADS, 1, ML_HD)), state4((ML_HEADS, 1, 1))],
        out_shape=[jax.ShapeDtypeStruct((bsz, length, ML_W), BF16),
                   jax.ShapeDtypeStruct((bsz, ML_HEADS, ML_HD, ML_HD), F32),
                   jax.ShapeDtypeStruct((bsz, ML_HEADS, 1, ML_HD), F32),
                   jax.ShapeDtypeStruct((bsz, ML_HEADS, 1, 1), F32)],
        scratch_shapes=[pltpu.VMEM((nb, ML_HEADS, ML_HD, ML_HD), F32),
                        pltpu.VMEM((nb, ML_HEADS, 1, ML_HD), F32),
                        pltpu.VMEM((nb, ML_HEADS, 1, 1), F32)],
        compiler_params=_cparams(("parallel", "arbitrary")),
        name="mlstm",
    )(b_gate, proj, proj, proj, proj, proj, norm_g.reshape(1, ML_W),
      c0, n0.reshape(bsz, ML_HEADS, 1, ML_HD), m0.reshape(bsz, ML_HEADS, 1, 1))
    return h, c1, n1.reshape(bsz, ML_HEADS, ML_HD), m1.reshape(bsz, ML_HEADS)


def _gelu_tanh(x):
    return 0.5 * x * (1.0 + jnp.tanh(math.sqrt(2.0 / math.pi) * (x + 0.044715 * (x * x * x))))


def _compress_kernel(*refs, npages, rows, transposed):
    jp = rows // CMP_STRIDE
    j = npages * jp
    lane = lax.broadcasted_iota(jnp.int32, (j, LANES), 1)
    low = lane < NSA_HD
    if transposed:
        pe_ref, w1_ref, w2_ref, o_ref, xk_scr, xv_scr = refs[-6:]
        page_refs = refs[-6 - npages:-6]
        for p in range(npages):
            x = page_refs[p][0, 0].T
            xk_scr[p * rows:(p + 1) * rows, :] = x[:, :KV_W]
            xv_scr[p * rows:(p + 1) * rows, :] = x[:, KV_W:]

        def chunk_phase(c, s):
            return (xk_scr, xv_scr)[s][pl.ds(c, j, stride=CMP_STRIDE), :]
    else:
        pe_ref, w1_ref, w2_ref, o_ref = refs[-4:]
        page_refs = refs[-4 - 2 * npages:-4]

        def chunk_phase(c, s):
            parts = [page_refs[2 * p + s][0, pl.ds(c, jp, stride=CMP_STRIDE), :] for p in range(npages)]
            return parts[0] if npages == 1 else jnp.concatenate(parts, axis=0)

    out = jnp.zeros((j, 2 * KV_HEADS * NSA_HD), F32)
    for s in range(2):
        halves = [[], []]
        for c in range(0, CMP_STRIDE, 2):
            va = chunk_phase(c, s)
            vb = chunk_phase(c + 1, s)
            halves[0].append(jnp.where(low, va, pltpu.roll(vb, NSA_HD, axis=1)).astype(BF16))
            halves[1].append(jnp.where(low, pltpu.roll(va, NSA_HD, axis=1), vb).astype(BF16))
        pe_c = _dot(pe_ref[s], w1_ref[s])
        pe_const = pe_c[0:1, :CMP_HIDDEN] + pe_c[1:2, CMP_HIDDEN:]
        for g in range(KV_HEADS):
            lhs = jnp.concatenate(halves[g], axis=1)
            acc = _dot(lhs, w1_ref[s])
            hid = acc[:, :CMP_HIDDEN] + pltpu.roll(acc[:, CMP_HIDDEN:], j - 1, axis=0) + pe_const
            out = out + _dot(_gelu_tanh(hid).astype(BF16), w2_ref[s * KV_HEADS + g])
    o_ref[0] = out


def _compress_weights(pe, w1, w2):
    w1r = w1.reshape(2, CMP_BLOCK, NSA_HD, CMP_HIDDEN)
    wa = w1r[:, :CMP_STRIDE].reshape(2, 4, 4 * NSA_HD, CMP_HIDDEN)
    wb = w1r[:, CMP_STRIDE:].reshape(2, 4, 4 * NSA_HD, CMP_HIDDEN)
    w1p = jnp.concatenate([wa, wb], axis=-1).astype(BF16).reshape(2, CMP_STRIDE * NSA_HD, 2 * CMP_HIDDEN)
    pea = pe[:, :CMP_STRIDE].reshape(2, 1, CMP_STRIDE * NSA_HD)
    peb = pe[:, CMP_STRIDE:].reshape(2, 1, CMP_STRIDE * NSA_HD)
    pep = jnp.concatenate([pea, peb, jnp.zeros((2, 14, CMP_STRIDE * NSA_HD), F32)], axis=1).astype(BF16)
    w2p = jnp.zeros((2, KV_HEADS, CMP_HIDDEN, 2, KV_HEADS, NSA_HD), F32)
    for s in range(2):
        for g in range(KV_HEADS):
            w2p = w2p.at[s, g, :, s, g, :].set(w2[s])
    w2p = w2p.reshape(2 * KV_HEADS, CMP_HIDDEN, 2 * KV_W).astype(BF16)
    return pep, w1p, w2p


def compress(pages, page_index_maps, num_scalar_prefetch, prefetch, grid_n, npages, rows, cw, transposed=False):
    pep, w1p, w2p = cw
    j = npages * rows // CMP_STRIDE
    const = lambda nd: (lambda *a: (0,) * nd)
    if transposed:
        page_specs = [pl.BlockSpec((1, 1, 2 * KV_W, rows), im) for im in page_index_maps]
        scratch = [pltpu.VMEM((npages * rows, KV_W), F32)] * 2
    else:
        page_specs = [pl.BlockSpec((1, rows, KV_W), im(s)) for im in page_index_maps for s in range(2)]
        scratch = []
    grid_spec = pltpu.PrefetchScalarGridSpec(
        num_scalar_prefetch=num_scalar_prefetch,
        grid=(grid_n,),
        in_specs=page_specs
        + [pl.BlockSpec(pep.shape, const(3)), pl.BlockSpec(w1p.shape, const(3)), pl.BlockSpec(w2p.shape, const(3))],
        out_specs=pl.BlockSpec((1, j, 2 * KV_W), lambda b, *a: (b, 0, 0)),
        scratch_shapes=scratch,
    )
    return pl.pallas_call(
        functools.partial(_compress_kernel, npages=npages, rows=rows, transposed=transposed),
        grid_spec=grid_spec,
        out_shape=jax.ShapeDtypeStruct((grid_n, j, 2 * KV_W), F32),
        compiler_params=_cparams(("parallel",)),
        name="compress",
    )(*prefetch, *([pages] * len(page_specs)), pep, w1p, w2p)


def _rel_bucket(dist):
    n = jnp.maximum(dist, 0)
    max_exact = NUM_BUCKETS // 2
    nf = jnp.maximum(n, 1).astype(F32)
    large = max_exact + (jnp.log(nf / max_exact) / math.log(REL_MAX_DIST / max_exact)
                         * (NUM_BUCKETS - max_exact)).astype(jnp.int32)
    large = jnp.minimum(large, NUM_BUCKETS - 1)
    return jnp.where(n < max_exact, n, large)


def _delta_bias(rel_bias, dist):
    far = rel_bias[NUM_BUCKETS - 1]
    val = rel_bias[_rel_bucket(dist)] - far
    return jnp.where(((dist >= 0) & (dist < REL_MAX_DIST))[..., None], val, 0.0)


def _lanes_ri(tab, g):
    rows = tab.shape[0]
    return tab[:, :, g * Q_PER_KV:(g + 1) * Q_PER_KV].transpose(0, 2, 1).reshape(rows, Q_PER_KV * QT)


def _prompt_tables(rel_bias):
    i = jnp.arange(QT)[None, :]
    j = jnp.arange(KT)[:, None]
    d_diag = i - j
    diag = jnp.where((d_diag >= 0)[..., None], _delta_bias(rel_bias, d_diag), NEG)
    sub = _delta_bias(rel_bias, i + KT - j)
    n2 = jnp.arange(NEAR_ROWS)[:, None]
    near = _delta_bias(rel_bias, i + (CMP_STRIDE * NC_PAD_FRONT - CMP_BLOCK + 1) - CMP_STRIDE * n2)
    stack = lambda tab: jnp.stack([_lanes_ri(tab, g) for g in range(KV_HEADS)]) * LOG2E
    return stack(sub), stack(diag), stack(near)


def _cover_t(n_sel, n_cmp, ncp):
    n = np.arange(ncp) - NC_PAD_FRONT
    c_start = n * CMP_STRIDE
    s_start = np.arange(n_sel)[:, None] * SEL_BLOCK
    cov = (c_start[None, :] < s_start + SEL_BLOCK) & (c_start[None, :] + CMP_BLOCK > s_start)
    cov &= ((n >= 0) & (n < n_cmp))[None, :]
    return jnp.asarray(cov, BF16)


def _nsa_prompt_kernel(q_ref, gt_ref, kc_ref, vct_ref, cov_ref, bct_ref, ksa_ref, vst_ref, kw_ref, vwt_ref,
                       tsub_ref, tdiag_ref, o_ref, s_scr, p_scr, q_scr, sa_scr, sb_scr, m_scr, l_scr, acc_scr,
                       ocmp_scr, osel_scr, ot_scr, *, nsel, ncp, ncv):
    blk = pl.program_id(1)
    t0 = blk * QT
    lanes4 = Q_PER_KV * QT
    q_t = (q_ref[0] * (NSA_HD ** -0.5 * LOG2E)).T

    def tile4(x):
        return jnp.concatenate([x] * Q_PER_KV, axis=1)

    def weighted_values(scores, m, values):
        probs = [jnp.exp2(s - m).astype(BF16) for s in scores]
        p_all = probs[0] if len(probs) == 1 else jnp.concatenate(probs, axis=0)
        v_all = values[0] if len(values) == 1 else jnp.concatenate(values, axis=1)
        return _dot(v_all, p_all)

    def tiles_max(scores, m):
        for s in scores:
            m = jnp.maximum(m, jnp.max(s, axis=0, keepdims=True))
        return m

    def normalise(acc, m):
        inv = jnp.where(m > 0.5 * NEG, 1.0 / acc[NSA_HD:NSA_HD + 1, :], 0.0)
        return acc[0:NSA_HD, :] * inv

    def update(g, scores, values):
        m_old = m_scr[g]
        m_new = tiles_max(scores, m_old)
        acc_scr[g] = jnp.exp2(m_old - m_new) * acc_scr[g] + weighted_values(scores, m_new, values)
        m_scr[g] = m_new

    row_k = lax.broadcasted_iota(jnp.int32, (KT, QT), 0)
    lane_q = lax.broadcasted_iota(jnp.int32, (KT, QT), 1)

    importance = []
    for g in range(KV_HEADS):
        gs = slice(g * NSA_HD, (g + 1) * NSA_HD)
        qg = jnp.concatenate([q_t[(g * Q_PER_KV + r) * NSA_HD:(g * Q_PER_KV + r + 1) * NSA_HD, :]
                              for r in range(Q_PER_KV)], axis=1)
        zero = jnp.zeros_like(qg)
        qpad = jnp.concatenate([qg, zero] if g == 0 else [zero, qg], axis=0).astype(BF16)
        q_scr[g, 0:KV_W, :] = qpad

        s_scr[...] = _dot(kc_ref[0], qpad)
        near0 = pl.multiple_of(blk * (QT // CMP_STRIDE), 8)
        s_scr[pl.ds(near0, NEAR_ROWS), :] = s_scr[pl.ds(near0, NEAR_ROWS), :] + bct_ref[g]
        n_all = blk * (QT // CMP_STRIDE) - (CMP_BLOCK // CMP_STRIDE)
        n_any = n_all + (QT - 1) // CMP_STRIDE + 1
        m_scr[g] = jnp.full((1, lanes4), NEG, F32)
        l_scr[g] = jnp.zeros((1, lanes4), F32)
        nchunk = ncp // KT
        chunk_kind = []
        for c in range(nchunk):
            cs = slice(c * KT, (c + 1) * KT)
            lo, hi = c * KT - NC_PAD_FRONT, (c + 1) * KT - NC_PAD_FRONT - 1
            skip = (lo > n_any) if lo < ncv else True
            full = (hi <= n_all) if (lo >= 0 and hi < ncv) else False
            chunk_kind.append((cs, skip, full))
            if skip is True:
                continue

            @pl.when(full)
            def _(cs=cs):
                m_scr[g] = jnp.maximum(m_scr[g], jnp.max(s_scr[cs, :], axis=0, keepdims=True))

            @pl.when(jnp.logical_not(full) & jnp.logical_not(skip))
            def _(cs=cs, lo=lo):
                n = row_k + lo
                ok = (n >= 0) & (n < ncv) & (CMP_STRIDE * n + (CMP_BLOCK - 1) <= t0 + lane_q)
                sc = s_scr[cs, :] + tile4(jnp.where(ok, 0.0, NEG))
                s_scr[cs, :] = sc
                m_scr[g] = jnp.maximum(m_scr[g], jnp.max(sc, axis=0, keepdims=True))

        mx = m_scr[g]
        for cs, skip, _ in chunk_kind:
            if skip is True:
                p_scr[cs, :] = jnp.zeros((KT, lanes4), BF16)
                continue

            @pl.when(jnp.logical_not(skip))
            def _(cs=cs):
                p = jnp.exp2(s_scr[cs, :] - mx)
                l_scr[g] = l_scr[g] + jnp.sum(p, axis=0, keepdims=True)
                p_scr[cs, :] = p.astype(BF16)

            @pl.when(skip)
            def _(cs=cs):
                p_scr[cs, :] = jnp.zeros((KT, lanes4), BF16)

        inv_c = jnp.where(mx > 0.5 * NEG, 1.0 / l_scr[g], 0.0)
        ocmp_scr[g] = _dot(vct_ref[0, gs, :], p_scr[...]) * inv_c
        imp4 = _dot(cov_ref[...], p_scr[...]) * inv_c
        imp = imp4[:, 0:QT]
        for r in range(1, Q_PER_KV):
            imp = imp + imp4[:, r * QT:(r + 1) * QT]
        importance.append(imp)

    jb = lax.broadcasted_iota(jnp.int32, (nsel, QT), 0)
    tq = t0 + lax.broadcasted_iota(jnp.int32, (nsel, QT), 1)
    cur = tq // SEL_BLOCK
    forced = (jb == 0) | (jb == cur) | (jb == cur - 1)
    future = jb * SEL_BLOCK > tq
    score = [jnp.where(forced, FORCE, jnp.where(future, -FORCE, imp)) for imp in importance]
    chosen_any = [jnp.zeros((nsel, QT), jnp.bool_) for _ in range(KV_HEADS)]
    for _ in range(min(N_SELECT, nsel)):
        for g in range(KV_HEADS):
            best = jnp.max(score[g], axis=0, keepdims=True)
            first = jnp.min(jnp.where(score[g] == best, jb, nsel), axis=0, keepdims=True)
            chosen = jb == first
            chosen_any[g] = chosen_any[g] | chosen
            score[g] = jnp.where(chosen, REMOVED, score[g])
    for g in range(KV_HEADS):
        q_scr[g, KV_W:KV_W + nsel, :] = tile4(jnp.where(chosen_any[g], 0.0, NEG)).astype(BF16)
        if nsel < LANES:
            q_scr[g, KV_W + nsel:, :] = jnp.zeros((LANES - nsel, lanes4), BF16)

    def values_of(ref, kt, g):
        return ref[0, kt, g * V_ROWS:(g + 1) * V_ROWS, :]

    def qk_pair(k, dst):
        for g in range(KV_HEADS):
            dst[g, 0:KT, :] = _dot(ksa_ref[0, 2 * k], q_scr[g])
            dst[g, KT:2 * KT, :] = _dot(ksa_ref[0, 2 * k + 1], q_scr[g])

    def softmax_pair(k, src):
        for g in range(KV_HEADS):
            update(g, [src[g, 0:KT, :], src[g, KT:2 * KT, :]],
                   [values_of(vst_ref, 2 * k, g), values_of(vst_ref, 2 * k + 1, g)])

    m_scr[...] = jnp.full(m_scr.shape, NEG, F32)
    acc_scr[...] = jnp.zeros(acc_scr.shape, F32)
    n_far = jnp.maximum(blk - 1, 0)
    npairs = n_far // 2

    @pl.when(npairs > 0)
    def _():
        qk_pair(0, sa_scr)

    def far_body(j, carry):
        k0 = 2 * j
        qk_pair(jnp.minimum(k0 + 1, npairs - 1), sb_scr)
        softmax_pair(k0, sa_scr)
        qk_pair(jnp.minimum(k0 + 2, npairs - 1), sa_scr)
        softmax_pair(k0 + 1, sb_scr)
        return carry

    lax.fori_loop(0, npairs // 2, far_body, 0)

    @pl.when(npairs % 2 == 1)
    def _():
        softmax_pair(npairs - 1, sa_scr)

    def gate(valid):
        return jnp.where(valid, 0.0, NEG).astype(F32)

    anti = tile4(jnp.where(row_k > lane_q, 0.0, NEG))
    sel_tiles = ((blk - 2, lambda g: gate(n_far % 2 == 1)),
                 (blk - 1, lambda g: tsub_ref[g] + gate(blk >= 1)),
                 (blk, lambda g: tdiag_ref[g]))
    win_tiles = ((blk - 4, lambda g: anti + gate(blk >= 4)),
                 (blk - 3, lambda g: gate(blk >= 3)),
                 (blk - 2, lambda g: gate(blk >= 2)),
                 (blk - 1, lambda g: tsub_ref[g] + gate(blk >= 1)),
                 (blk, lambda g: tdiag_ref[g]))
    sel_scores, win_scores = [], []
    for g in range(KV_HEADS):
        sel_scores.append([_dot(ksa_ref[0, jnp.maximum(kt, 0)], q_scr[g]) + add(g) for kt, add in sel_tiles])
        win_scores.append([_dot(kw_ref[0, jnp.maximum(kt, 0)], q_scr[g, 0:KV_W, :]) + add(g)
                           for kt, add in win_tiles])
    o_win = []
    for g in range(KV_HEADS):
        update(g, sel_scores[g], [values_of(vst_ref, jnp.maximum(kt, 0), g) for kt, _ in sel_tiles])
        osel_scr[g] = normalise(acc_scr[g], m_scr[g])
        m_win = tiles_max(win_scores[g], jnp.full((1, lanes4), NEG, F32))
        acc_win = weighted_values(win_scores[g], m_win,
                                  [values_of(vwt_ref, jnp.maximum(kt, 0), g) for kt, _ in win_tiles])
        o_win.append(normalise(acc_win, m_win))

    gl = gt_ref[0, 0]
    for g in range(KV_HEADS):
        o_g = (_sigmoid(gl[3 * g:3 * g + 1, :]) * ocmp_scr[g] + _sigmoid(gl[3 * g + 1:3 * g + 2, :]) * osel_scr[g]
               + _sigmoid(gl[3 * g + 2:3 * g + 3, :]) * o_win[g])
        for r in range(Q_PER_KV):
            h = g * Q_PER_KV + r
            ot_scr[h * NSA_HD:(h + 1) * NSA_HD, :] = o_g[:, r * QT:(r + 1) * QT]

    o_ref[0] = ot_scr[...].T.astype(o_ref.dtype)


def nsa_prompt(proj, ckv, rel_tabs):
    bsz, seq, _ = proj.shape
    nq = seq // QT
    n_cmp = (seq - CMP_BLOCK) // CMP_STRIDE + 1
    nsel = seq // SEL_BLOCK
    ncp = -(-(NC_PAD_FRONT + seq // CMP_STRIDE) // KT) * KT
    tsub, tdiag, bct = rel_tabs

    def tiles(cols):
        return proj[:, :, cols:cols + KV_W].astype(BF16).reshape(bsz, nq, KT, KV_W)

    assert nsel <= LANES
    blk_of_key = np.arange(seq) // SEL_BLOCK
    onehot = jnp.asarray(blk_of_key[:, None] == np.arange(LANES)[None, :], BF16).reshape(nq, KT, LANES)
    ksa = jnp.concatenate([tiles(C_SKV), jnp.broadcast_to(onehot, (bsz, nq, KT, LANES))], axis=-1)

    def values_t(cols):
        vt = tiles(cols).reshape(bsz, nq, KT, KV_HEADS, NSA_HD).transpose(0, 1, 3, 4, 2)
        ones = jnp.ones((bsz, nq, KV_HEADS, 1, KT), BF16)
        zeros = jnp.zeros((bsz, nq, KV_HEADS, V_ROWS - NSA_HD - 1, KT), BF16)
        return jnp.concatenate([vt, ones, zeros], axis=3).reshape(bsz, nq, KV_HEADS * V_ROWS, KT)

    vst = values_t(C_SKV + KV_W)
    kw = tiles(C_WKV)
    vwt = values_t(C_WKV + KV_W)
    back = ncp - NC_PAD_FRONT - ckv.shape[1]
    kc = jnp.pad(ckv[:, :, :KV_W], ((0, 0), (NC_PAD_FRONT, back), (0, 0))).astype(BF16)
    vct = jnp.pad(ckv[:, :, KV_W:], ((0, 0), (NC_PAD_FRONT, back), (0, 0))).astype(BF16).transpose(0, 2, 1)
    cov = _cover_t(nsel, n_cmp, ncp)
    ng = proj[:, :, C_NG:C_NG + 3 * NSA_HEADS].reshape(bsz, nq, QT, KV_HEADS, Q_PER_KV, 3)
    gt = ng.transpose(0, 1, 3, 5, 4, 2).reshape(bsz, nq, KV_HEADS * 3, Q_PER_KV * QT)
    gt = jnp.pad(gt, ((0, 0), (0, 0), (0, 2), (0, 0)))

    whole = lambda a: pl.BlockSpec((1,) + a.shape[1:], lambda b, i: (b,) + (0,) * (a.ndim - 1))
    const = lambda a: pl.BlockSpec(a.shape, lambda b, i: (0,) * a.ndim)
    lanes4 = Q_PER_KV * QT
    return pl.pallas_call(
        functools.partial(_nsa_prompt_kernel, nsel=nsel, ncp=ncp, ncv=n_cmp),
        grid=(bsz, nq),
        in_specs=[pl.BlockSpec((1, QT, NSA_W), lambda b, i: (b, i, C_NQ // NSA_W)),
                  pl.BlockSpec((1, 1, 8, lanes4), lambda b, i: (b, i, 0, 0)),
                  whole(kc), whole(vct), const(cov), const(bct),
                  whole(ksa), whole(vst), whole(kw), whole(vwt), const(tsub), const(tdiag)],
        out_specs=pl.BlockSpec((1, QT, NSA_W), lambda b, i: (b, i, 0)),
        out_shape=jax.ShapeDtypeStruct((bsz, seq, NSA_W), BF16),
        scratch_shapes=[pltpu.VMEM((ncp, lanes4), F32), pltpu.VMEM((ncp, lanes4), BF16),
                        pltpu.VMEM((KV_HEADS, KV_W + LANES, lanes4), BF16),
                        pltpu.VMEM((KV_HEADS, 2 * KT, lanes4), F32), pltpu.VMEM((KV_HEADS, 2 * KT, lanes4), F32),
                        pltpu.VMEM((KV_HEADS, 1, lanes4), F32), pltpu.VMEM((KV_HEADS, 1, lanes4), F32),
                        pltpu.VMEM((KV_HEADS, V_ROWS, lanes4), F32), pltpu.VMEM((KV_HEADS, NSA_HD, lanes4), F32),
                        pltpu.VMEM((KV_HEADS, NSA_HD, lanes4), F32), pltpu.VMEM((NSA_W, QT), F32)],
        compiler_params=_cparams(("parallel", "arbitrary")),
        name="nsa_prompt",
    )(proj, gt, kc, vct, cov, bct, ksa, vst, kw, vwt, tsub, tdiag)


S_ROWS = Q_PER_KV * KV_HEADS * 4
NEW_PAD = 16
SAMPLE_NB = 4


def _sample_tables(rel_bias, past, t_new, n_cmp, win_len):
    row = jnp.arange(S_ROWS)
    i = (row % t_new)[:, None]
    g = (row // t_new) % KV_HEADS
    r = row // (t_new * KV_HEADS)
    head = g * Q_PER_KV + r

    def pick(tab):
        return jnp.take_along_axis(tab, head[:, None, None], axis=2)[..., 0]

    n = jnp.arange(LANES)[None, :]
    bcs = jnp.where(n < n_cmp, pick(_delta_bias(rel_bias, past + i - (CMP_STRIDE * n + CMP_BLOCK - 1))), NEG)
    jj = jnp.arange(LANES)[None, :]
    last = pick(_delta_bias(rel_bias, LANES + i - jj))
    w_old = jnp.where(jj > i, 0.0, NEG)
    j2 = jnp.arange(NEW_PAD)[None, :]
    d_new = i - j2
    new = jnp.where((j2 < t_new) & (d_new >= 0), pick(_delta_bias(rel_bias, d_new)), NEG)
    return bcs, jnp.stack([last, w_old]), new


def _nsa_sample_kernel(*refs, nb, npages, past, t_new):
    pt_ref, q_ref, g_ref, kc_ref = refs[:4]
    page_refs = refs[4:4 + nb * npages]
    win_ref, news_ref, neww_ref, bcs_ref, tab_ref, tnew_ref, cov_ref, o_ref, s_scr = refs[4 + nb * npages:]
    del pt_ref
    elems = range(nb)
    qb = [(q_ref[e] * (NSA_HD ** -0.5)).astype(BF16) for e in elems]

    def softmax_pv(tiles_of):
        mx = []
        for e in elems:
            m = jnp.full((S_ROWS, 1), NEG, F32)
            for c, (score_fn, _, add_fn, width) in enumerate(tiles_of(e)):
                s = score_fn()
                if add_fn is not None:
                    s = s + add_fn()
                s_scr[e, :, c * LANES:c * LANES + width] = s
                m = jnp.maximum(m, s.max(axis=1, keepdims=True))
            mx.append(m)
        out = []
        for e in elems:
            lsum = jnp.zeros((S_ROWS, 1), F32)
            acc = jnp.zeros((S_ROWS, KV_W), F32)
            for c, (_, pv_fn, _, width) in enumerate(tiles_of(e)):
                p = jnp.exp(s_scr[e, :, c * LANES:c * LANES + width] - mx[e])
                lsum = lsum + p.sum(axis=1, keepdims=True)
                acc = acc + pv_fn(p.astype(BF16))
            out.append(acc / lsum)
        return out

    def cached_tile(e, tile_ref, lanes, add_fn):
        return (lambda: _dot(qb[e], tile_ref[:KV_W, lanes].astype(BF16)),
                lambda p: _dot_nt(p, tile_ref[KV_W:, lanes].astype(BF16)), add_fn, LANES)

    def new_tile(e, ref, add_fn):
        return (lambda: _dot_nt(qb[e], ref[e, :, :KV_W].astype(BF16)),
                lambda p: _dot(p, ref[e, :, KV_W:].astype(BF16)), add_fn, NEW_PAD)

    rows_gi = KV_HEADS * t_new
    o_cmp, imp = [], []
    for e in elems:
        kc = kc_ref[e]
        s_c = _dot_nt(qb[e], kc[:, :KV_W].astype(BF16)) + bcs_ref[...]
        p_c = jnp.exp(s_c - s_c.max(axis=1, keepdims=True))
        p_c = (p_c / p_c.sum(axis=1, keepdims=True)).astype(BF16)
        o_cmp.append(_dot(p_c, kc[:, KV_W:].astype(BF16)))
        imp_r = _dot(p_c, cov_ref[...])
        tot = imp_r
        for r in range(1, Q_PER_KV):
            tot = tot + pltpu.roll(imp_r, r * rows_gi, axis=0)
        imp.append(tot)

    jb = lax.broadcasted_iota(jnp.int32, (S_ROWS, LANES), 1)
    qpos = past + lax.broadcasted_iota(jnp.int32, (S_ROWS, LANES), 0) % t_new
    cur = qpos // SEL_BLOCK
    forced = (jb == 0) | (jb == cur) | (jb == cur - 1)
    future = jb * SEL_BLOCK > qpos
    score = [jnp.where(forced, FORCE, jnp.where(future, -FORCE, imp[e])) for e in elems]
    chosen_any = [jnp.zeros((S_ROWS, LANES), jnp.bool_) for _ in elems]
    for _ in range(N_SELECT):
        for e in elems:
            best = jnp.max(score[e], axis=1, keepdims=True)
            first = jnp.min(jnp.where(score[e] == best, jb, LANES), axis=1, keepdims=True)
            chosen = jb == first
            chosen_any[e] = chosen_any[e] | chosen
            score[e] = jnp.where(chosen, REMOVED, score[e])
    sel = [jnp.where(chosen_any[e], 1.0, 0.0).astype(BF16) for e in elems]

    def block_mask(e, j0, nkeys=LANES):
        jrow = lax.broadcasted_iota(jnp.int32, (LANES, nkeys), 0)
        kcol = lax.broadcasted_iota(jnp.int32, (LANES, nkeys), 1)
        expand = jnp.where(jrow == j0 + kcol // SEL_BLOCK, 1.0, 0.0).astype(BF16)
        return jnp.where(_dot(sel[e], expand) > 0.5, 0.0, NEG)

    jn = past // SEL_BLOCK

    def sel_tiles(e):
        tiles = []
        for p in range(npages):
            if p == npages - 1:
                add = lambda p=p: block_mask(e, 2 * p) + tab_ref[0]
            else:
                add = lambda p=p: block_mask(e, 2 * p)
            tiles.append(cached_tile(e, page_refs[e * npages + p].at[0, 0], slice(None), add))
        tiles.append(new_tile(e, news_ref, lambda: tnew_ref[...] + block_mask(e, jn, NEW_PAD)))
        return tiles

    o_sel = softmax_pv(sel_tiles)

    nwin = win_ref.shape[3] // LANES

    def win_tiles(e):
        tiles = []
        for c in range(nwin):
            if c == 0:
                add = lambda: tab_ref[1]
            elif c == nwin - 1:
                add = lambda: tab_ref[0]
            else:
                add = None
            tiles.append(cached_tile(e, win_ref.at[0, e], slice(c * LANES, (c + 1) * LANES), add))
        tiles.append(new_tile(e, neww_ref, lambda: tnew_ref[...]))
        return tiles

    o_win = softmax_pv(win_tiles)

    for e in elems:
        gl = g_ref[e]
        o_ref[e] = (_sigmoid(gl[:, 0:1]) * o_cmp[e] + _sigmoid(gl[:, 1:2]) * o_sel[e]
                    + _sigmoid(gl[:, 2:3]) * o_win[e])


def nsa_sample(proj, ckv, pool_s, win_buf, layer, page_table, rel_bias):
    bsz, t_new, _ = proj.shape
    npages = page_table.shape[1]
    past = npages * PAGE
    n_cmp = (past + t_new - CMP_BLOCK) // CMP_STRIDE + 1
    n_sel = -(-(past + t_new) // SEL_BLOCK)
    assert n_cmp <= LANES and n_sel <= LANES and win_buf.shape[3] == WINDOW and t_new == 4
    assert (past + t_new - 1) // SEL_BLOCK == past // SEL_BLOCK
    bcs, tab, tnew = _sample_tables(rel_bias, past, t_new, n_cmp, win_buf.shape[3])
    n = np.arange(LANES)
    jsel = np.arange(LANES)
    cov = ((n[:, None] * CMP_STRIDE < jsel[None, :] * SEL_BLOCK + SEL_BLOCK)
           & (n[:, None] * CMP_STRIDE + CMP_BLOCK > jsel[None, :] * SEL_BLOCK)
           & (n[:, None] < n_cmp) & (jsel[None, :] < n_sel))
    cov = jnp.asarray(cov, BF16)

    eye = jnp.eye(KV_HEADS, dtype=F32)
    q5 = proj[:, :, C_NQ:C_NQ + NSA_W].reshape(bsz, t_new, KV_HEADS, Q_PER_KV, NSA_HD).transpose(0, 3, 2, 1, 4)
    qpad = (q5[:, :, :, :, None, :] * eye[None, None, :, None, :, None]).reshape(bsz, S_ROWS, KV_W)
    ng = proj[:, :, C_NG:C_NG + 3 * NSA_HEADS].reshape(bsz, t_new, KV_HEADS, Q_PER_KV, 3).transpose(0, 3, 2, 1, 4)
    gl = jnp.pad(ng.reshape(bsz, S_ROWS, 3), ((0, 0), (0, 0), (0, LANES - 3)))
    pad_new = lambda c: jnp.pad(proj[:, :, c:c + 2 * KV_W], ((0, 0), (0, NEW_PAD - t_new), (0, 0)))

    nb = SAMPLE_NB
    assert bsz % nb == 0
    per_b = lambda a: pl.BlockSpec((nb,) + a.shape[1:], lambda b, pt: (b,) + (0,) * (a.ndim - 1))
    const = lambda a: pl.BlockSpec(a.shape, lambda b, pt: (0,) * a.ndim)
    new_s, new_w = pad_new(C_SKV), pad_new(C_WKV)
    grid_spec = pltpu.PrefetchScalarGridSpec(
        num_scalar_prefetch=1,
        grid=(bsz // nb,),
        in_specs=[per_b(qpad), per_b(gl), per_b(ckv)]
        + [pl.BlockSpec((1, 1, 2 * KV_W, PAGE), (lambda b, pt, e=e, p=p: (layer, pt[b * nb + e, p], 0, 0)))
           for e in range(nb) for p in range(npages)]
        + [pl.BlockSpec((1, nb) + win_buf.shape[2:], lambda b, pt: (layer, b, 0, 0)),
           per_b(new_s), per_b(new_w), const(bcs), const(tab), const(tnew), const(cov)],
        out_specs=pl.BlockSpec((nb, S_ROWS, KV_W), lambda b, pt: (b, 0, 0)),
        scratch_shapes=[pltpu.VMEM((nb, S_ROWS, (npages + 1) * LANES), F32)],
    )
    out = pl.pallas_call(
        functools.partial(_nsa_sample_kernel, nb=nb, npages=npages, past=past, t_new=t_new),
        grid_spec=grid_spec,
        out_shape=jax.ShapeDtypeStruct((bsz, S_ROWS, KV_W), F32),
        compiler_params=_cparams(("parallel",)),
        name="nsa_sample",
    )(page_table, qpad, gl, ckv, *([pool_s] * (nb * npages)), win_buf, new_s, new_w, bcs, tab, tnew, cov)
    o6 = out.reshape(bsz, Q_PER_KV, KV_HEADS, t_new, KV_HEADS, NSA_HD)
    o5 = jnp.stack([o6[:, :, g, :, g, :] for g in range(KV_HEADS)], axis=2)
    return o5.transpose(0, 3, 2, 1, 4).reshape(bsz, t_new, NSA_W)


def _in_proj_perm():
    sizes = (ML_W, ML_W, ML_W, ML_W, ML_HEADS, ML_HEADS, NSA_W, KV_W, KV_W, KV_W, KV_W, KV_W, KV_W, 3 * NSA_HEADS)
    off = np.concatenate([[0], np.cumsum(sizes)])
    order = (0, 1, 2, 3, 6, 7, 8, 9, 10, 11, 12, 4, 5, 13)
    return np.concatenate([np.arange(off[k], off[k + 1]) for k in order])


def _layer(x, mods, proj_fn_args, lw, tm, tiles_per_group, mixer):
    sh1, sc1, g1, sh2, sc2, g2 = mods
    w_in_bf, w_out_bf, w_up_bf, w_down_bf, ln_g, ln_b = lw
    proj = in_proj(x, sc1, sh1, w_in_bf, tm, tiles_per_group)
    h_ml, o_nsa, extras = mixer(proj)
    x1 = out_proj_ln(x, h_ml, o_nsa, w_out_bf, g1, ln_g[0], ln_b[0], tm, tiles_per_group)
    x2 = ffn_ln(x1, sc2, sh2, g2, w_up_bf, w_down_bf, ln_g[1], ln_b[1], tm, tiles_per_group)
    return x2, proj, extras


def kernel(x_prompt, x_sample, cache_cmp_kv, cache_slc_kv, cache_win_kv, state_mlstm_C, state_mlstm_n,
           state_mlstm_m, page_table, c_prompt, c_sample, w_ada, b_ada, w_in, b_gate, ml_norm_g, cmp_pe,
           cmp_w1, cmp_w2, rel_bias, w_out, ln_g, ln_b, w_up, w_down):
    bsz, seq, d = x_prompt.shape
    bs, ts, _ = x_sample.shape
    depth = w_in.shape[0]
    n_phys = cache_cmp_kv.shape[1]
    npages = page_table.shape[1]
    win_len = cache_win_kv.shape[2]
    tm = 512
    t_ml = 128
    t_pad = 16

    rows_last = lambda a: a.transpose(0, 1, 3, 4, 5, 2).reshape(a.shape[0], a.shape[1], 2 * KV_W, a.shape[2])
    pool_c, pool_s, win_t = rows_last(cache_cmp_kv), rows_last(cache_slc_kv), rows_last(cache_win_kv)

    perm = _in_proj_perm()
    prompt_tabs = _prompt_tables(rel_bias)
    nc_rows = bsz + bs
    c_all = jnp.pad(jnp.concatenate([c_prompt, c_sample], axis=0), ((0, -nc_rows % 8), (0, 0)))

    xp = x_prompt.reshape(bsz * seq, d)
    xs = x_sample.reshape(bs * ts, d)
    outs = {k: [] for k in ("cmp_p", "cmp_s", "slc_p", "slc_s", "win_p", "win_s",
                            "C_p", "C_s", "n_p", "n_s", "m_p", "m_s")}
    kv_shape = lambda b, t: (b, t, 2, KV_HEADS, NSA_HD)

    for l in range(depth):
        ada = ada_mod(c_all, w_ada[l].astype(BF16), b_ada[l]).reshape(c_all.shape[0], 6, d)
        mods_p = [ada[:bsz, k][:, None, :] for k in range(6)]
        mods_s = [jnp.repeat(ada[bsz:nc_rows, k], ts, axis=0)[None] for k in range(6)]
        w_in_bf = jnp.pad(w_in[l][:, perm], ((0, 0), (0, N_IN_PAD - perm.size))).astype(BF16)
        lw = (w_in_bf, w_out[l].astype(BF16), w_up[l].astype(BF16), w_down[l].astype(BF16), ln_g[l], ln_b[l])
        cw = _compress_weights(cmp_pe[l], cmp_w1[l], cmp_w2[l])

        def prompt_mixer(proj):
            proj = proj.reshape(bsz, seq, N_IN_PAD)
            h_ml, c1, n1, m1 = mlstm(proj, b_gate[l], ml_norm_g[l],
                                     jnp.zeros((bsz, ML_HEADS, ML_HD, ML_HD), F32),
                                     jnp.zeros((bsz, ML_HEADS, ML_HD), F32), jnp.zeros((bsz, ML_HEADS), F32),
                                     nb=bsz, t=t_ml, valid=t_ml)
            ckv = compress(proj, [lambda s: (lambda b: (b, 0, C_CKV // KV_W + s))], 0, (), bsz, 1, seq, cw)
            o_nsa = nsa_prompt(proj, ckv, prompt_tabs)
            return h_ml.reshape(bsz * seq, ML_W), o_nsa.reshape(bsz * seq, NSA_W), (c1, n1, m1)

        def sample_mixer(proj):
            proj = proj.reshape(bs, ts, N_IN_PAD)
            proj_pad = jnp.pad(proj, ((0, 0), (0, t_pad - ts), (0, 0)))
            h_ml, c1, n1, m1 = mlstm(proj_pad, b_gate[l], ml_norm_g[l], state_mlstm_C[l], state_mlstm_n[l],
                                     state_mlstm_m[l], nb=8, t=t_pad, valid=ts)
            pages = [(lambda b, pt, p=p: (l, pt[b, p], 0, 0)) for p in range(npages)]
            ckv = compress(pool_c, pages, 1, (page_table,), bs, npages, PAGE, cw, transposed=True)
            o_nsa = nsa_sample(proj, ckv, pool_s, win_t, l, page_table, rel_bias)
            return (h_ml[:, :ts].reshape(bs * ts, ML_W), o_nsa.reshape(bs * ts, NSA_W).astype(BF16),
                    (c1, n1, m1))

        xp, proj_p, st_p = _layer(xp, mods_p, None, lw, tm, seq // tm, prompt_mixer)
        xs, proj_s, st_s = _layer(xs, mods_s, None, lw, bs * ts, 1, sample_mixer)

        proj_p = proj_p.reshape(bsz, seq, N_IN_PAD)
        proj_s = proj_s.reshape(bs, ts, N_IN_PAD)
        rows = lambda proj, c: proj[:, :, c:c + 2 * KV_W]
        win = min(WINDOW, seq)
        outs["cmp_p"].append(rows(proj_p, C_CKV).reshape(kv_shape(bsz, seq)))
        outs["slc_p"].append(rows(proj_p, C_SKV).reshape(kv_shape(bsz, seq)))
        outs["win_p"].append(rows(proj_p, C_WKV)[:, seq - win:].reshape(kv_shape(bsz, win)))
        outs["cmp_s"].append(rows(proj_s, C_CKV).reshape(kv_shape(bs, ts)))
        outs["slc_s"].append(rows(proj_s, C_SKV).reshape(kv_shape(bs, ts)))
        win_all = jnp.concatenate([cache_win_kv[l], rows(proj_s, C_WKV).reshape(kv_shape(bs, ts))], axis=1)
        outs["win_s"].append(win_all[:, -win_len:])
        for tag, st in (("p", st_p), ("s", st_s)):
            outs["C_" + tag].append(st[0])
            outs["n_" + tag].append(st[1])
            outs["m_" + tag].append(st[2])

    stk = lambda k: jnp.stack(outs[k])
    return (xp.reshape(bsz, seq, d), xs.reshape(bs, ts, d),
            stk("cmp_p"), stk("cmp_s"), stk("slc_p"), stk("slc_s"), stk("win_p"), stk("win_s"),
            stk("C_p"), stk("C_s"), stk("n_p"), stk("n_s"), stk("m_p"), stk("m_s"))
```

```python
import functools
import math

import numpy as np
import jax
import jax.numpy as jnp
from jax import lax
from jax.experimental import pallas as pl
from jax.experimental.pallas import tpu as pltpu

F32 = jnp.float32
BF16 = jnp.bfloat16

D_MODEL = 1024
DEPTH = 2
PAGE = 128
ML_HEADS = 4
ML_HD = 128
ML_W = ML_HEADS * ML_HD
NSA_HEADS = 8
NSA_HD = 64
NSA_W = NSA_HEADS * NSA_HD
KV_HEADS = 2
Q_PER_KV = 4
KV_W = KV_HEADS * NSA_HD
CMP_BLOCK = 32
CMP_STRIDE = 16
CMP_HIDDEN = 256
SEL_BLOCK = 64
N_SELECT = 16
WINDOW = 512
NUM_BUCKETS = 32
REL_MAX_DIST = 128
D_FF = 4 * D_MODEL
ALPHA = (2 * DEPTH) ** 0.25
LN_EPS = 1e-5
NEG = -1e30
FORCE = 1e9
REMOVED = -3e38
LOG2E = math.log2(math.e)

LANES = 128
QT = 128
KT = 128
V_ROWS = 80
NC_PAD_FRONT = 16
NEAR_ROWS = 24

C_MQ, C_MK, C_MV, C_MO = 0, 512, 1024, 1536
C_NQ = 2048
C_CKV, C_SKV, C_WKV = 2560, 2816, 3072
C_GATE = 3328
C_NG = 3336
N_IN_PAD = 3456
VMEM_LIMIT = 56 * 1024 * 1024


def _cparams(sem):
    return pltpu.CompilerParams(dimension_semantics=sem, vmem_limit_bytes=VMEM_LIMIT)


def _dot(a, b):
    return jnp.dot(a, b, preferred_element_type=F32)


def _dot_nt(a, b):
    return lax.dot_general(a, b, (((1,), (1,)), ((), ())), preferred_element_type=F32)


def _dot_tn(a, b):
    return lax.dot_general(a, b, (((0,), (0,)), ((), ())), preferred_element_type=F32)


def _sigmoid(x):
    return 1.0 / (1.0 + jnp.exp(-x))


def _layer_norm(y, g, b):
    mu = jnp.mean(y, axis=-1, keepdims=True)
    d = y - mu
    var = jnp.mean(d * d, axis=-1, keepdims=True)
    return d * lax.rsqrt(var + LN_EPS) * g + b


def _ada_kernel(c_ref, w_ref, b_ref, o_ref):
    c = c_ref[...]
    a = (c * _sigmoid(c)).astype(BF16)
    o_ref[...] = _dot(a, w_ref[...]) + b_ref[...]


def ada_mod(c, w_bf, b):
    m, d = c.shape
    n = w_bf.shape[1]
    tn = 1536
    return pl.pallas_call(
        _ada_kernel,
        grid=(n // tn,),
        in_specs=[pl.BlockSpec((m, d), lambda j: (0, 0)),
                  pl.BlockSpec((d, tn), lambda j: (0, j)),
                  pl.BlockSpec((1, tn), lambda j: (0, j))],
        out_specs=pl.BlockSpec((m, tn), lambda j: (0, j)),
        out_shape=jax.ShapeDtypeStruct((m, n), F32),
        compiler_params=_cparams(("arbitrary",)),
        name="ada_mod",
    )(c, w_bf, b.reshape(1, n))


def _inproj_kernel(x_ref, sc_ref, sh_ref, w_ref, o_ref, u_scr):
    @pl.when(pl.program_id(1) == 0)
    def _():
        u_scr[...] = (x_ref[...] * (1.0 + sc_ref[0]) + sh_ref[0]).astype(BF16)

    o_ref[...] = _dot(u_scr[...], w_ref[...])


def _mod_spec(mod, tm, tiles_per_group):
    r = mod.shape[1]
    return pl.BlockSpec((1, r, D_MODEL), lambda i, j: (i // tiles_per_group, 0, 0))


def in_proj(x, sc, sh, w_bf, tm, tiles_per_group):
    m = x.shape[0]
    n = w_bf.shape[1]
    tn = n
    return pl.pallas_call(
        _inproj_kernel,
        grid=(m // tm, n // tn),
        in_specs=[pl.BlockSpec((tm, D_MODEL), lambda i, j: (i, 0)),
                  _mod_spec(sc, tm, tiles_per_group),
                  _mod_spec(sh, tm, tiles_per_group),
                  pl.BlockSpec((D_MODEL, tn), lambda i, j: (0, j))],
        out_specs=pl.BlockSpec((tm, tn), lambda i, j: (i, j)),
        out_shape=jax.ShapeDtypeStruct((m, n), F32),
        scratch_shapes=[pltpu.VMEM((tm, D_MODEL), BF16)],
        compiler_params=_cparams(("parallel", "arbitrary")),
        name="in_proj",
    )(x, sc, sh, w_bf)


def _outproj_kernel(x_ref, a1_ref, a2_ref, w1_ref, w2_ref, g_ref, lg_ref, lb_ref, o_ref):
    mixed = _dot(a1_ref[...], w1_ref[...]) + _dot(a2_ref[...], w2_ref[...])
    y = ALPHA * x_ref[...] + g_ref[0] * mixed
    o_ref[...] = _layer_norm(y, lg_ref[...], lb_ref[...])


def out_proj_ln(x, a1, a2, w_bf, gate, ln_g, ln_b, tm, tiles_per_group):
    m = x.shape[0]
    k1 = a1.shape[1]
    r = gate.shape[1]
    return pl.pallas_call(
        _outproj_kernel,
        grid=(m // tm,),
        in_specs=[pl.BlockSpec((tm, D_MODEL), lambda i: (i, 0)),
                  pl.BlockSpec((tm, k1), lambda i: (i, 0)),
                  pl.BlockSpec((tm, k1), lambda i: (i, 0)),
                  pl.BlockSpec((k1, D_MODEL), lambda i: (0, 0)),
                  pl.BlockSpec((k1, D_MODEL), lambda i: (1, 0)),
                  pl.BlockSpec((1, r, D_MODEL), lambda i: (i // tiles_per_group, 0, 0)),
                  pl.BlockSpec((1, D_MODEL), lambda i: (0, 0)),
                  pl.BlockSpec((1, D_MODEL), lambda i: (0, 0))],
        out_specs=pl.BlockSpec((tm, D_MODEL), lambda i: (i, 0)),
        out_shape=jax.ShapeDtypeStruct((m, D_MODEL), F32),
        compiler_params=_cparams(("parallel",)),
        name="out_proj_ln",
    )(x, a1, a2, w_bf, w_bf, gate, ln_g.reshape(1, -1), ln_b.reshape(1, -1))


def _ffn_kernel(x_ref, sc_ref, sh_ref, g_ref, wu_ref, wd_ref, lg_ref, lb_ref, o_ref, u_scr, acc_scr):
    f = pl.program_id(1)

    @pl.when(f == 0)
    def _():
        u_scr[...] = (x_ref[...] * (1.0 + sc_ref[0]) + sh_ref[0]).astype(BF16)
        acc_scr[...] = jnp.zeros_like(acc_scr)

    h = jnp.maximum(_dot(u_scr[...], wu_ref[...]), 0.0)
    acc_scr[...] += _dot((h * h).astype(BF16), wd_ref[...])

    @pl.when(f == pl.num_programs(1) - 1)
    def _():
        y = ALPHA * x_ref[...] + g_ref[0] * acc_scr[...]
        o_ref[...] = _layer_norm(y, lg_ref[...], lb_ref[...])


def ffn_ln(x, sc, sh, gate, wu_bf, wd_bf, ln_g, ln_b, tm, tiles_per_group):
    m = x.shape[0]
    tf = 512
    r = sc.shape[1]
    mod = pl.BlockSpec((1, r, D_MODEL), lambda i, f: (i // tiles_per_group, 0, 0))
    return pl.pallas_call(
        _ffn_kernel,
        grid=(m // tm, D_FF // tf),
        in_specs=[pl.BlockSpec((tm, D_MODEL), lambda i, f: (i, 0)),
                  mod, mod, mod,
                  pl.BlockSpec((D_MODEL, tf), lambda i, f: (0, f)),
                  pl.BlockSpec((tf, D_MODEL), lambda i, f: (f, 0)),
                  pl.BlockSpec((1, D_MODEL), lambda i, f: (0, 0)),
                  pl.BlockSpec((1, D_MODEL), lambda i, f: (0, 0))],
        out_specs=pl.BlockSpec((tm, D_MODEL), lambda i, f: (i, 0)),
        out_shape=jax.ShapeDtypeStruct((m, D_MODEL), F32),
        scratch_shapes=[pltpu.VMEM((tm, D_MODEL), BF16), pltpu.VMEM((tm, D_MODEL), F32)],
        compiler_params=_cparams(("parallel", "arbitrary")),
        name="ffn_ln",
    )(x, sc, sh, gate, wu_bf, wd_bf, ln_g.reshape(1, -1), ln_b.reshape(1, -1))


def _mlstm_kernel(bg_ref, q_ref, k_ref, v_ref, o_ref, g_ref, ng_ref, c0_ref, n0_ref, m0_ref,
                  h_ref, c_out, n_out, m_out, c_scr, n_scr, m_scr, *, nb, t, valid):
    ci = pl.program_id(1)

    @pl.when(ci == 0)
    def _():
        c_scr[...] = c0_ref[...]
        n_scr[...] = n0_ref[...]
        m_scr[...] = m0_ref[...]

    row = lax.broadcasted_iota(jnp.int32, (t, t), 0)
    col = lax.broadcasted_iota(jnp.int32, (t, t), 1)
    tri = col <= row
    tri_t = row <= col
    eye = row == col
    row1 = lax.broadcasted_iota(jnp.int32, (t, 1), 0)
    scale = ML_HD ** -0.5

    def gate_scans(b, h):
        gates = g_ref[b]
        ig_col = gates[:, h:h + 1] + bg_ref[h]
        fr = gates[:, ML_HEADS + h:ML_HEADS + h + 1] + bg_ref[ML_HEADS + h]
        lf_col = jnp.minimum(fr, 0.0) - jnp.log1p(jnp.exp(-jnp.abs(fr)))
        if valid < t:
            ig_col = jnp.where(row1 < valid, ig_col, NEG)
            lf_col = jnp.where(row1 < valid, lf_col, 0.0)
        lf_row = jnp.sum(jnp.where(eye, lf_col, 0.0), axis=0, keepdims=True)
        ig_row = jnp.sum(jnp.where(eye, ig_col, 0.0), axis=0, keepdims=True)
        f_col = jnp.sum(jnp.where(tri, lf_row, 0.0), axis=1, keepdims=True)
        f_row = jnp.sum(jnp.where(tri_t, lf_col, 0.0), axis=0, keepdims=True)
        a_row = ig_row - f_row
        a_col = ig_col - f_col
        m0 = m_scr[b, h]
        cm_col = jnp.max(jnp.where(tri, a_row, NEG), axis=1, keepdims=True)
        g_col = jnp.maximum(m0, cm_col)
        g_end = jnp.maximum(m0, jnp.max(a_row, axis=1, keepdims=True))
        return dict(dmat=jnp.exp(jnp.where(tri, a_row - g_col, NEG)), decay_col=jnp.exp(m0 - g_col),
                    m_col=f_col + g_col, m_end=jnp.sum(lf_row, axis=1, keepdims=True) + g_end,
                    w_end_col=jnp.exp(a_col - g_end), carry=jnp.exp(m0 - g_end))

    streams = [(b, h) for b in range(nb) for h in range(ML_HEADS)]
    group = 2 * ML_HEADS
    for g0 in range(0, len(streams), group):
        grp = streams[g0:g0 + group]
        hs = [slice(h * ML_HD, (h + 1) * ML_HD) for _, h in grp]
        sc = [gate_scans(b, h) for b, h in grp]
        qf = [q_ref[b][:, s] for (b, _), s in zip(grp, hs)]
        qb = [x.astype(BF16) for x in qf]
        kf = [k_ref[b][:, s] * scale for (b, _), s in zip(grp, hs)]
        kb = [x.astype(BF16) for x in kf]
        vb = [v_ref[b][:, s].astype(BF16) for (b, _), s in zip(grp, hs)]
        cmat = [c_scr[b, h] for b, h in grp]
        nvec = [n_scr[b, h] for b, h in grp]
        n_st = range(len(grp))

        w = [_dot_nt(qb[i], kb[i]) * sc[i]["dmat"] for i in n_st]
        qc = [_dot(qb[i], cmat[i].astype(BF16)) for i in n_st]
        num = [_dot(w[i].astype(BF16), vb[i]) + qc[i] * sc[i]["decay_col"] for i in n_st]
        den = [jnp.sum(w[i], axis=1, keepdims=True)
               + jnp.sum(qf[i] * nvec[i], axis=1, keepdims=True) * sc[i]["decay_col"] for i in n_st]
        hh = [num[i] / jnp.maximum(jnp.abs(den[i]), jnp.exp(-sc[i]["m_col"])) for i in n_st]
        mu = [jnp.mean(x, axis=1, keepdims=True) for x in hh]
        dd = [hh[i] - mu[i] for i in n_st]
        var = [jnp.mean(x * x, axis=1, keepdims=True) for x in dd]
        for i, (b, h) in enumerate(grp):
            hn = dd[i] * lax.rsqrt(var[i] + LN_EPS) * ng_ref[:, hs[i]] * _sigmoid(o_ref[b][:, hs[i]])
            h_ref[b, :, hs[i]] = hn.astype(h_ref.dtype)

        kw = [kf[i] * sc[i]["w_end_col"] for i in n_st]
        upd = [_dot_tn(kw[i].astype(BF16), vb[i]) for i in n_st]
        for i, (b, h) in enumerate(grp):
            c_scr[b, h] = sc[i]["carry"] * cmat[i] + upd[i]
            n_scr[b, h] = sc[i]["carry"] * nvec[i] + jnp.sum(kw[i], axis=0, keepdims=True)
            m_scr[b, h] = sc[i]["m_end"]

    @pl.when(ci == pl.num_programs(1) - 1)
    def _():
        c_out[...] = c_scr[...]
        n_out[...] = n_scr[...]
        m_out[...] = m_scr[...]


def mlstm(proj, b_gate, norm_g, c0, n0, m0, *, nb, t, valid):
    bsz, length, _ = proj.shape
    nchunk = length // t
    wide = lambda cb: pl.BlockSpec((nb, t, ML_W), lambda i, c: (i, c, cb))
    state4 = lambda s: pl.BlockSpec((nb,) + s, lambda i, c: (i, 0, 0, 0))
    kern = functools.partial(_mlstm_kernel, nb=nb, t=t, valid=valid)
    h, c1, n1, m1 = pl.pallas_call(
        kern,
        grid=(bsz // nb, nchunk),
        in_specs=[pl.BlockSpec(memory_space=pltpu.SMEM),
                  wide(C_MQ // ML_W), wide(C_MK // ML_W), wide(C_MV // ML_W), wide(C_MO // ML_W),
                  pl.BlockSpec((nb, t, LANES), lambda i, c: (i, c, C_GATE // LANES)),
                  pl.BlockSpec((1, ML_W), lambda i, c: (0, 0)),
                  state4((ML_HEADS, ML_HD, ML_HD)), state4((ML_HEADS, 1, ML_HD)), state4((ML_HEADS, 1, 1))],
        out_specs=[pl.BlockSpec((nb, t, ML_W), lambda i, c: (i, c, 0)),
                   state4((ML_HEADS, ML_HD, ML_HD)), state4((ML_HEADS, 1, ML_HD)), state4((ML_HEADS, 1, 1))],
        out_shape=[jax.ShapeDtypeStruct((bsz, length, ML_W), BF16),
                   jax.ShapeDtypeStruct((bsz, ML_HEADS, ML_HD, ML_HD), F32),
                   jax.ShapeDtypeStruct((bsz, ML_HEADS, 1, ML_HD), F32),
                   jax.ShapeDtypeStruct((bsz, ML_HEADS, 1, 1), F32)],
        scratch_shapes=[pltpu.VMEM((nb, ML_HEADS, ML_HD, ML_HD), F32),
                        pltpu.VMEM((nb, ML_HEADS, 1, ML_HD), F32),
                        pltpu.VMEM((nb, ML_HEADS, 1, 1), F32)],
        compiler_params=_cparams(("parallel", "arbitrary")),
        name="mlstm",
    )(b_gate, proj, proj, proj, proj, proj, norm_g.reshape(1, ML_W),
      c0, n0.reshape(bsz, ML_HEADS, 1, ML_HD), m0.reshape(bsz, ML_HEADS, 1, 1))
    return h, c1, n1.reshape(bsz, ML_HEADS, ML_HD), m1.reshape(bsz, ML_HEADS)


def _gelu_tanh(x):
    return 0.5 * x * (1.0 + jnp.tanh(math.sqrt(2.0 / math.pi) * (x + 0.044715 * (x * x * x))))


def _compress_kernel(*refs, npages, rows, transposed, nb):
    jp = rows // CMP_STRIDE
    j = npages * jp
    lane = lax.broadcasted_iota(jnp.int32, (j, LANES), 1)
    low = lane < NSA_HD
    if transposed:
        pe_ref, w1_ref, w2_ref, o_ref, xk_scr, xv_scr = refs[-6:]
        page_refs = refs[-6 - npages:-6]
        for p in range(npages):
            x = page_refs[p][0, 0].T
            xk_scr[p * rows:(p + 1) * rows, :] = x[:, :KV_W]
            xv_scr[p * rows:(p + 1) * rows, :] = x[:, KV_W:]

        def chunk_phase(c, s):
            return (xk_scr, xv_scr)[s][pl.ds(c, j, stride=CMP_STRIDE), :]
    else:
        pe_ref, w1_ref, w2_ref, o_ref = refs[-4:]
        page_refs = refs[-4 - 2 * npages:-4]

        def chunk_phase(c, s):
            parts = [page_refs[2 * p + s][0, pl.ds(c, jp, stride=CMP_STRIDE), :] for p in range(npages)]
            return parts[0] if npages == 1 else jnp.concatenate(parts, axis=0)

    out = jnp.zeros((j, 2 * KV_HEADS * NSA_HD), F32)
    for s in range(2):
        halves = [[], []]
        for c in range(0, CMP_STRIDE, 2):
            va = chunk_phase(c, s)
            vb = chunk_phase(c + 1, s)
            halves[0].append(jnp.where(low, va, pltpu.roll(vb, NSA_HD, axis=1)).astype(BF16))
            halves[1].append(jnp.where(low, pltpu.roll(va, NSA_HD, axis=1), vb).astype(BF16))
        pe_c = _dot(pe_ref[s], w1_ref[s])
        pe_const = pe_c[0:1, :CMP_HIDDEN] + pe_c[1:2, CMP_HIDDEN:]
        for g in range(KV_HEADS):
            lhs = jnp.concatenate(halves[g], axis=1)
            acc = _dot(lhs, w1_ref[s])
            hid = acc[:, :CMP_HIDDEN] + pltpu.roll(acc[:, CMP_HIDDEN:], j - 1, axis=0) + pe_const
            out = out + _dot(_gelu_tanh(hid).astype(BF16), w2_ref[s * KV_HEADS + g])
    je = j // nb
    for e in range(nb):
        o_ref[e] = out[e * je:(e + 1) * je, :]


def _compress_weights(pe, w1, w2):
    w1r = w1.reshape(2, CMP_BLOCK, NSA_HD, CMP_HIDDEN)
    wa = w1r[:, :CMP_STRIDE].reshape(2, 4, 4 * NSA_HD, CMP_HIDDEN)
    wb = w1r[:, CMP_STRIDE:].reshape(2, 4, 4 * NSA_HD, CMP_HIDDEN)
    w1p = jnp.concatenate([wa, wb], axis=-1).astype(BF16).reshape(2, CMP_STRIDE * NSA_HD, 2 * CMP_HIDDEN)
    pea = pe[:, :CMP_STRIDE].reshape(2, 1, CMP_STRIDE * NSA_HD)
    peb = pe[:, CMP_STRIDE:].reshape(2, 1, CMP_STRIDE * NSA_HD)
    pep = jnp.concatenate([pea, peb, jnp.zeros((2, 14, CMP_STRIDE * NSA_HD), F32)], axis=1).astype(BF16)
    w2p = jnp.zeros((2, KV_HEADS, CMP_HIDDEN, 2, KV_HEADS, NSA_HD), F32)
    for s in range(2):
        for g in range(KV_HEADS):
            w2p = w2p.at[s, g, :, s, g, :].set(w2[s])
    w2p = w2p.reshape(2 * KV_HEADS, CMP_HIDDEN, 2 * KV_W).astype(BF16)
    return pep, w1p, w2p


def compress(pages, page_index_maps, num_scalar_prefetch, prefetch, grid_n, npages, rows, cw, transposed=False,
             nb=1):
    pep, w1p, w2p = cw
    j = npages * rows // CMP_STRIDE
    const = lambda nd: (lambda *a: (0,) * nd)
    if transposed:
        page_specs = [pl.BlockSpec((1, 1, 2 * KV_W, rows), im) for im in page_index_maps]
        scratch = [pltpu.VMEM((npages * rows, KV_W), F32)] * 2
    else:
        page_specs = [pl.BlockSpec((1, rows, KV_W), im(s)) for im in page_index_maps for s in range(2)]
        scratch = []
    grid_spec = pltpu.PrefetchScalarGridSpec(
        num_scalar_prefetch=num_scalar_prefetch,
        grid=(grid_n,),
        in_specs=page_specs
        + [pl.BlockSpec(pep.shape, const(3)), pl.BlockSpec(w1p.shape, const(3)), pl.BlockSpec(w2p.shape, const(3))],
        out_specs=pl.BlockSpec((nb, j // nb, 2 * KV_W), lambda b, *a: (b, 0, 0)),
        scratch_shapes=scratch,
    )
    return pl.pallas_call(
        functools.partial(_compress_kernel, npages=npages, rows=rows, transposed=transposed, nb=nb),
        grid_spec=grid_spec,
        out_shape=jax.ShapeDtypeStruct((grid_n * nb, j // nb, 2 * KV_W), F32),
        compiler_params=_cparams(("parallel",)),
        name="compress",
    )(*prefetch, *([pages] * len(page_specs)), pep, w1p, w2p)


def _rel_bucket(dist):
    n = jnp.maximum(dist, 0)
    max_exact = NUM_BUCKETS // 2
    nf = jnp.maximum(n, 1).astype(F32)
    large = max_exact + (jnp.log(nf / max_exact) / math.log(REL_MAX_DIST / max_exact)
                         * (NUM_BUCKETS - max_exact)).astype(jnp.int32)
    large = jnp.minimum(large, NUM_BUCKETS - 1)
    return jnp.where(n < max_exact, n, large)


def _delta_bias(rel_bias, dist):
    far = rel_bias[NUM_BUCKETS - 1]
    val = rel_bias[_rel_bucket(dist)] - far
    return jnp.where(((dist >= 0) & (dist < REL_MAX_DIST))[..., None], val, 0.0)


def _lanes_ri(tab, g):
    rows = tab.shape[0]
    return tab[:, :, g * Q_PER_KV:(g + 1) * Q_PER_KV].transpose(0, 2, 1).reshape(rows, Q_PER_KV * QT)


def _prompt_tables(rel_bias):
    i = jnp.arange(QT)[None, :]
    j = jnp.arange(KT)[:, None]
    d_diag = i - j
    diag = jnp.where((d_diag >= 0)[..., None], _delta_bias(rel_bias, d_diag), NEG)
    sub = _delta_bias(rel_bias, i + KT - j)
    n2 = jnp.arange(NEAR_ROWS)[:, None]
    near = _delta_bias(rel_bias, i + (CMP_STRIDE * NC_PAD_FRONT - CMP_BLOCK + 1) - CMP_STRIDE * n2)
    stack = lambda tab: jnp.stack([_lanes_ri(tab, g) for g in range(KV_HEADS)]) * LOG2E
    return stack(sub), stack(diag), stack(near)


def _cover_t(n_sel, n_cmp, ncp):
    n = np.arange(ncp) - NC_PAD_FRONT
    c_start = n * CMP_STRIDE
    s_start = np.arange(n_sel)[:, None] * SEL_BLOCK
    cov = (c_start[None, :] < s_start + SEL_BLOCK) & (c_start[None, :] + CMP_BLOCK > s_start)
    cov &= ((n >= 0) & (n < n_cmp))[None, :]
    return jnp.asarray(cov, BF16)


def _nsa_prompt_kernel(q_ref, gt_ref, kc_ref, vct_ref, cov_ref, bct_ref, ksa_ref, vst_ref, kw_ref, vwt_ref,
                       tsub_ref, tdiag_ref, o_ref, s_scr, p_scr, q_scr, sa_scr, sb_scr, m_scr, l_scr, acc_scr,
                       ocmp_scr, osel_scr, ot_scr, *, nsel, ncp, ncv):
    blk = pl.program_id(1)
    t0 = blk * QT
    lanes4 = Q_PER_KV * QT
    q_t = (q_ref[0] * (NSA_HD ** -0.5 * LOG2E)).T

    def tile4(x):
        return jnp.concatenate([x] * Q_PER_KV, axis=1)

    def weighted_values(scores, m, values):
        probs = [jnp.exp2(s - m).astype(BF16) for s in scores]
        p_all = probs[0] if len(probs) == 1 else jnp.concatenate(probs, axis=0)
        v_all = values[0] if len(values) == 1 else jnp.concatenate(values, axis=1)
        return _dot(v_all, p_all)

    def tiles_max(scores, m):
        for s in scores:
            m = jnp.maximum(m, jnp.max(s, axis=0, keepdims=True))
        return m

    def normalise(acc, m):
        inv = jnp.where(m > 0.5 * NEG, 1.0 / acc[NSA_HD:NSA_HD + 1, :], 0.0)
        return acc[0:NSA_HD, :] * inv

    def update(g, scores, values):
        m_old = m_scr[g]
        m_new = tiles_max(scores, m_old)
        acc_scr[g] = jnp.exp2(m_old - m_new) * acc_scr[g] + weighted_values(scores, m_new, values)
        m_scr[g] = m_new

    row_k = lax.broadcasted_iota(jnp.int32, (KT, QT), 0)
    lane_q = lax.broadcasted_iota(jnp.int32, (KT, QT), 1)

    importance = []
    for g in range(KV_HEADS):
        gs = slice(g * NSA_HD, (g + 1) * NSA_HD)
        qg = jnp.concatenate([q_t[(g * Q_PER_KV + r) * NSA_HD:(g * Q_PER_KV + r + 1) * NSA_HD, :]
                              for r in range(Q_PER_KV)], axis=1)
        zero = jnp.zeros_like(qg)
        qpad = jnp.concatenate([qg, zero] if g == 0 else [zero, qg], axis=0).astype(BF16)
        q_scr[g, 0:KV_W, :] = qpad

        s_scr[...] = _dot(kc_ref[0], qpad)
        near0 = pl.multiple_of(blk * (QT // CMP_STRIDE), 8)
        s_scr[pl.ds(near0, NEAR_ROWS), :] = s_scr[pl.ds(near0, NEAR_ROWS), :] + bct_ref[g]
        n_all = blk * (QT // CMP_STRIDE) - (CMP_BLOCK // CMP_STRIDE)
        n_any = n_all + (QT - 1) // CMP_STRIDE + 1
        m_scr[g] = jnp.full((1, lanes4), NEG, F32)
        l_scr[g] = jnp.zeros((1, lanes4), F32)
        nchunk = ncp // KT
        chunk_kind = []
        for c in range(nchunk):
            cs = slice(c * KT, (c + 1) * KT)
            lo, hi = c * KT - NC_PAD_FRONT, (c + 1) * KT - NC_PAD_FRONT - 1
            skip = (lo > n_any) if lo < ncv else True
            full = (hi <= n_all) if (lo >= 0 and hi < ncv) else False
            chunk_kind.append((cs, skip, full))
            if skip is True:
                continue

            @pl.when(full)
            def _(cs=cs):
                m_scr[g] = jnp.maximum(m_scr[g], jnp.max(s_scr[cs, :], axis=0, keepdims=True))

            @pl.when(jnp.logical_not(full) & jnp.logical_not(skip))
            def _(cs=cs, lo=lo):
                n = row_k + lo
                ok = (n >= 0) & (n < ncv) & (CMP_STRIDE * n + (CMP_BLOCK - 1) <= t0 + lane_q)
                sc = s_scr[cs, :] + tile4(jnp.where(ok, 0.0, NEG))
                s_scr[cs, :] = sc
                m_scr[g] = jnp.maximum(m_scr[g], jnp.max(sc, axis=0, keepdims=True))

        mx = m_scr[g]
        for cs, skip, _ in chunk_kind:
            if skip is True:
                p_scr[cs, :] = jnp.zeros((KT, lanes4), BF16)
                continue

            @pl.when(jnp.logical_not(skip))
            def _(cs=cs):
                p = jnp.exp2(s_scr[cs, :] - mx)
                l_scr[g] = l_scr[g] + jnp.sum(p, axis=0, keepdims=True)
                p_scr[cs, :] = p.astype(BF16)

            @pl.when(skip)
            def _(cs=cs):
                p_scr[cs, :] = jnp.zeros((KT, lanes4), BF16)

        inv_c = jnp.where(mx > 0.5 * NEG, 1.0 / l_scr[g], 0.0)
        ocmp_scr[g] = _dot(vct_ref[0, gs, :], p_scr[...]) * inv_c
        imp4 = _dot(cov_ref[...], p_scr[...]) * inv_c
        imp = imp4[:, 0:QT]
        for r in range(1, Q_PER_KV):
            imp = imp + imp4[:, r * QT:(r + 1) * QT]
        importance.append(imp)

    jb = lax.broadcasted_iota(jnp.int32, (nsel, QT), 0)
    tq = t0 + lax.broadcasted_iota(jnp.int32, (nsel, QT), 1)
    cur = tq // SEL_BLOCK
    forced = (jb == 0) | (jb == cur) | (jb == cur - 1)
    future = jb * SEL_BLOCK > tq
    score = [jnp.where(forced, FORCE, jnp.where(future, -FORCE, imp)) for imp in importance]
    chosen_any = [jnp.zeros((nsel, QT), jnp.bool_) for _ in range(KV_HEADS)]
    for _ in range(min(N_SELECT, nsel)):
        for g in range(KV_HEADS):
            best = jnp.max(score[g], axis=0, keepdims=True)
            first = jnp.min(jnp.where(score[g] == best, jb, nsel), axis=0, keepdims=True)
            chosen = jb == first
            chosen_any[g] = chosen_any[g] | chosen
            score[g] = jnp.where(chosen, REMOVED, score[g])
    for g in range(KV_HEADS):
        q_scr[g, KV_W:KV_W + nsel, :] = tile4(jnp.where(chosen_any[g], 0.0, NEG)).astype(BF16)
        if nsel < LANES:
            q_scr[g, KV_W + nsel:, :] = jnp.zeros((LANES - nsel, lanes4), BF16)

    def values_of(ref, kt, g):
        return ref[0, kt, g * V_ROWS:(g + 1) * V_ROWS, :]

    def qk_pair(k, dst):
        for g in range(KV_HEADS):
            dst[g, 0:KT, :] = _dot(ksa_ref[0, 2 * k], q_scr[g])
            dst[g, KT:2 * KT, :] = _dot(ksa_ref[0, 2 * k + 1], q_scr[g])

    def softmax_pair(k, src):
        for g in range(KV_HEADS):
            update(g, [src[g, 0:KT, :], src[g, KT:2 * KT, :]],
                   [values_of(vst_ref, 2 * k, g), values_of(vst_ref, 2 * k + 1, g)])

    m_scr[...] = jnp.full(m_scr.shape, NEG, F32)
    acc_scr[...] = jnp.zeros(acc_scr.shape, F32)
    n_far = jnp.maximum(blk - 1, 0)
    npairs = n_far // 2

    @pl.when(npairs > 0)
    def _():
        qk_pair(0, sa_scr)

    def far_body(j, carry):
        k0 = 2 * j
        qk_pair(jnp.minimum(k0 + 1, npairs - 1), sb_scr)
        softmax_pair(k0, sa_scr)
        qk_pair(jnp.minimum(k0 + 2, npairs - 1), sa_scr)
        softmax_pair(k0 + 1, sb_scr)
        return carry

    lax.fori_loop(0, npairs // 2, far_body, 0)

    @pl.when(npairs % 2 == 1)
    def _():
        softmax_pair(npairs - 1, sa_scr)

    def gate(valid):
        return jnp.where(valid, 0.0, NEG).astype(F32)

    anti = tile4(jnp.where(row_k > lane_q, 0.0, NEG))
    sel_tiles = ((blk - 2, lambda g: gate(n_far % 2 == 1)),
                 (blk - 1, lambda g: tsub_ref[g] + gate(blk >= 1)),
                 (blk, lambda g: tdiag_ref[g]))
    win_tiles = ((blk - 4, lambda g: anti + gate(blk >= 4)),
                 (blk - 3, lambda g: gate(blk >= 3)),
                 (blk - 2, lambda g: gate(blk >= 2)),
                 (blk - 1, lambda g: tsub_ref[g] + gate(blk >= 1)),
                 (blk, lambda g: tdiag_ref[g]))
    sel_scores, win_scores = [], []
    for g in range(KV_HEADS):
        sel_scores.append([_dot(ksa_ref[0, jnp.maximum(kt, 0)], q_scr[g]) + add(g) for kt, add in sel_tiles])
        win_scores.append([_dot(kw_ref[0, jnp.maximum(kt, 0)], q_scr[g, 0:KV_W, :]) + add(g)
                           for kt, add in win_tiles])
    o_win = []
    for g in range(KV_HEADS):
        update(g, sel_scores[g], [values_of(vst_ref, jnp.maximum(kt, 0), g) for kt, _ in sel_tiles])
        osel_scr[g] = normalise(acc_scr[g], m_scr[g])
        m_win = tiles_max(win_scores[g], jnp.full((1, lanes4), NEG, F32))
        acc_win = weighted_values(win_scores[g], m_win,
                                  [values_of(vwt_ref, jnp.maximum(kt, 0), g) for kt, _ in win_tiles])
        o_win.append(normalise(acc_win, m_win))

    gl = gt_ref[0, 0]
    for g in range(KV_HEADS):
        o_g = (_sigmoid(gl[3 * g:3 * g + 1, :]) * ocmp_scr[g] + _sigmoid(gl[3 * g + 1:3 * g + 2, :]) * osel_scr[g]
               + _sigmoid(gl[3 * g + 2:3 * g + 3, :]) * o_win[g])
        for r in range(Q_PER_KV):
            h = g * Q_PER_KV + r
            ot_scr[h * NSA_HD:(h + 1) * NSA_HD, :] = o_g[:, r * QT:(r + 1) * QT]

    o_ref[0] = ot_scr[...].T.astype(o_ref.dtype)


def nsa_prompt(proj, ckv, rel_tabs):
    bsz, seq, _ = proj.shape
    nq = seq // QT
    n_cmp = (seq - CMP_BLOCK) // CMP_STRIDE + 1
    nsel = seq // SEL_BLOCK
    ncp = -(-(NC_PAD_FRONT + seq // CMP_STRIDE) // KT) * KT
    tsub, tdiag, bct = rel_tabs

    def tiles(cols):
        return proj[:, :, cols:cols + KV_W].astype(BF16).reshape(bsz, nq, KT, KV_W)

    assert nsel <= LANES
    blk_of_key = np.arange(seq) // SEL_BLOCK
    onehot = jnp.asarray(blk_of_key[:, None] == np.arange(LANES)[None, :], BF16).reshape(nq, KT, LANES)
    ksa = jnp.concatenate([tiles(C_SKV), jnp.broadcast_to(onehot, (bsz, nq, KT, LANES))], axis=-1)

    def values_t(cols):
        vt = tiles(cols).reshape(bsz, nq, KT, KV_HEADS, NSA_HD).transpose(0, 1, 3, 4, 2)
        ones = jnp.ones((bsz, nq, KV_HEADS, 1, KT), BF16)
        zeros = jnp.zeros((bsz, nq, KV_HEADS, V_ROWS - NSA_HD - 1, KT), BF16)
        return jnp.concatenate([vt, ones, zeros], axis=3).reshape(bsz, nq, KV_HEADS * V_ROWS, KT)

    vst = values_t(C_SKV + KV_W)
    kw = tiles(C_WKV)
    vwt = values_t(C_WKV + KV_W)
    back = ncp - NC_PAD_FRONT - ckv.shape[1]
    kc = jnp.pad(ckv[:, :, :KV_W], ((0, 0), (NC_PAD_FRONT, back), (0, 0))).astype(BF16)
    vct = jnp.pad(ckv[:, :, KV_W:], ((0, 0), (NC_PAD_FRONT, back), (0, 0))).astype(BF16).transpose(0, 2, 1)
    cov = _cover_t(nsel, n_cmp, ncp)
    ng = proj[:, :, C_NG:C_NG + 3 * NSA_HEADS].reshape(bsz, nq, QT, KV_HEADS, Q_PER_KV, 3)
    gt = ng.transpose(0, 1, 3, 5, 4, 2).reshape(bsz, nq, KV_HEADS * 3, Q_PER_KV * QT)
    gt = jnp.pad(gt, ((0, 0), (0, 0), (0, 2), (0, 0)))

    whole = lambda a: pl.BlockSpec((1,) + a.shape[1:], lambda b, i: (b,) + (0,) * (a.ndim - 1))
    const = lambda a: pl.BlockSpec(a.shape, lambda b, i: (0,) * a.ndim)
    lanes4 = Q_PER_KV * QT
    return pl.pallas_call(
        functools.partial(_nsa_prompt_kernel, nsel=nsel, ncp=ncp, ncv=n_cmp),
        grid=(bsz, nq),
        in_specs=[pl.BlockSpec((1, QT, NSA_W), lambda b, i: (b, i, C_NQ // NSA_W)),
                  pl.BlockSpec((1, 1, 8, lanes4), lambda b, i: (b, i, 0, 0)),
                  whole(kc), whole(vct), const(cov), const(bct),
                  whole(ksa), whole(vst), whole(kw), whole(vwt), const(tsub), const(tdiag)],
        out_specs=pl.BlockSpec((1, QT, NSA_W), lambda b, i: (b, i, 0)),
        out_shape=jax.ShapeDtypeStruct((bsz, seq, NSA_W), BF16),
        scratch_shapes=[pltpu.VMEM((ncp, lanes4), F32), pltpu.VMEM((ncp, lanes4), BF16),
                        pltpu.VMEM((KV_HEADS, KV_W + LANES, lanes4), BF16),
                        pltpu.VMEM((KV_HEADS, 2 * KT, lanes4), F32), pltpu.VMEM((KV_HEADS, 2 * KT, lanes4), F32),
                        pltpu.VMEM((KV_HEADS, 1, lanes4), F32), pltpu.VMEM((KV_HEADS, 1, lanes4), F32),
                        pltpu.VMEM((KV_HEADS, V_ROWS, lanes4), F32), pltpu.VMEM((KV_HEADS, NSA_HD, lanes4), F32),
                        pltpu.VMEM((KV_HEADS, NSA_HD, lanes4), F32), pltpu.VMEM((NSA_W, QT), F32)],
        compiler_params=_cparams(("parallel", "arbitrary")),
        name="nsa_prompt",
    )(proj, gt, kc, vct, cov, bct, ksa, vst, kw, vwt, tsub, tdiag)


S_ROWS = Q_PER_KV * KV_HEADS * 4
NEW_PAD = 16
SAMPLE_NB = 4


def _sample_tables(rel_bias, past, t_new, n_cmp, win_len):
    row = jnp.arange(S_ROWS)
    i = (row % t_new)[:, None]
    g = (row // t_new) % KV_HEADS
    r = row // (t_new * KV_HEADS)
    head = g * Q_PER_KV + r

    def pick(tab):
        return jnp.take_along_axis(tab, head[:, None, None], axis=2)[..., 0]

    n = jnp.arange(LANES)[None, :]
    bcs = jnp.where(n < n_cmp, pick(_delta_bias(rel_bias, past + i - (CMP_STRIDE * n + CMP_BLOCK - 1))), NEG)
    jj = jnp.arange(LANES)[None, :]
    last = pick(_delta_bias(rel_bias, LANES + i - jj))
    w_old = jnp.where(jj > i, 0.0, NEG)
    j2 = jnp.arange(NEW_PAD)[None, :]
    d_new = i - j2
    new = jnp.where((j2 < t_new) & (d_new >= 0), pick(_delta_bias(rel_bias, d_new)), NEG)
    return bcs, jnp.stack([last, w_old]), new


def _nsa_sample_kernel(*refs, nb, npages, past, t_new):
    pt_ref, q_ref, g_ref, kc_ref = refs[:4]
    page_refs = refs[4:4 + nb * npages]
    win_ref, news_ref, neww_ref, bcs_ref, tab_ref, tnew_ref, cov_ref, o_ref, s_scr = refs[4 + nb * npages:]
    del pt_ref
    elems = range(nb)
    qb = [(q_ref[e] * (NSA_HD ** -0.5)).astype(BF16) for e in elems]

    def softmax_pv(tiles_of):
        mx = []
        for e in elems:
            m = jnp.full((S_ROWS, 1), NEG, F32)
            for c, (score_fn, _, add_fn, width) in enumerate(tiles_of(e)):
                s = score_fn()
                if add_fn is not None:
                    s = s + add_fn()
                s_scr[e, :, c * LANES:c * LANES + width] = s
                m = jnp.maximum(m, s.max(axis=1, keepdims=True))
            mx.append(m)
        out = []
        for e in elems:
            lsum = jnp.zeros((S_ROWS, 1), F32)
            acc = jnp.zeros((S_ROWS, KV_W), F32)
            for c, (_, pv_fn, _, width) in enumerate(tiles_of(e)):
                p = jnp.exp(s_scr[e, :, c * LANES:c * LANES + width] - mx[e])
                lsum = lsum + p.sum(axis=1, keepdims=True)
                acc = acc + pv_fn(p.astype(BF16))
            out.append(acc / lsum)
        return out

    def cached_tile(e, tile_ref, lanes, add_fn):
        return (lambda: _dot(qb[e], tile_ref[:KV_W, lanes].astype(BF16)),
                lambda p: _dot_nt(p, tile_ref[KV_W:, lanes].astype(BF16)), add_fn, LANES)

    def new_tile(e, ref, add_fn):
        return (lambda: _dot_nt(qb[e], ref[e, :, :KV_W].astype(BF16)),
                lambda p: _dot(p, ref[e, :, KV_W:].astype(BF16)), add_fn, NEW_PAD)

    rows_gi = KV_HEADS * t_new
    o_cmp, imp = [], []
    for e in elems:
        kc = kc_ref[e]
        s_c = _dot_nt(qb[e], kc[:, :KV_W].astype(BF16)) + bcs_ref[...]
        p_c = jnp.exp(s_c - s_c.max(axis=1, keepdims=True))
        p_c = (p_c / p_c.sum(axis=1, keepdims=True)).astype(BF16)
        o_cmp.append(_dot(p_c, kc[:, KV_W:].astype(BF16)))
        imp_r = _dot(p_c, cov_ref[...])
        tot = imp_r
        for r in range(1, Q_PER_KV):
            tot = tot + pltpu.roll(imp_r, r * rows_gi, axis=0)
        imp.append(tot)

    jb = lax.broadcasted_iota(jnp.int32, (S_ROWS, LANES), 1)
    qpos = past + lax.broadcasted_iota(jnp.int32, (S_ROWS, LANES), 0) % t_new
    cur = qpos // SEL_BLOCK
    forced = (jb == 0) | (jb == cur) | (jb == cur - 1)
    future = jb * SEL_BLOCK > qpos
    score = [jnp.where(forced, FORCE, jnp.where(future, -FORCE, imp[e])) for e in elems]
    chosen_any = [jnp.zeros((S_ROWS, LANES), jnp.bool_) for _ in elems]
    for _ in range(N_SELECT):
        for e in elems:
            best = jnp.max(score[e], axis=1, keepdims=True)
            first = jnp.min(jnp.where(score[e] == best, jb, LANES), axis=1, keepdims=True)
            chosen = jb == first
            chosen_any[e] = chosen_any[e] | chosen
            score[e] = jnp.where(chosen, REMOVED, score[e])
    sel = [jnp.where(chosen_any[e], 1.0, 0.0).astype(BF16) for e in elems]

    def block_mask(e, j0, nkeys=LANES):
        jrow = lax.broadcasted_iota(jnp.int32, (LANES, nkeys), 0)
        kcol = lax.broadcasted_iota(jnp.int32, (LANES, nkeys), 1)
        expand = jnp.where(jrow == j0 + kcol // SEL_BLOCK, 1.0, 0.0).astype(BF16)
        return jnp.where(_dot(sel[e], expand) > 0.5, 0.0, NEG)

    jn = past // SEL_BLOCK

    def sel_tiles(e):
        tiles = []
        for p in range(npages):
            if p == npages - 1:
                add = lambda p=p: block_mask(e, 2 * p) + tab_ref[0]
            else:
                add = lambda p=p: block_mask(e, 2 * p)
            tiles.append(cached_tile(e, page_refs[e * npages + p].at[0, 0], slice(None), add))
        tiles.append(new_tile(e, news_ref, lambda: tnew_ref[...] + block_mask(e, jn, NEW_PAD)))
        return tiles

    o_sel = softmax_pv(sel_tiles)

    nwin = win_ref.shape[3] // LANES

    def win_tiles(e):
        tiles = []
        for c in range(nwin):
            if c == 0:
                add = lambda: tab_ref[1]
            elif c == nwin - 1:
                add = lambda: tab_ref[0]
            else:
                add = None
            tiles.append(cached_tile(e, win_ref.at[0, e], slice(c * LANES, (c + 1) * LANES), add))
        tiles.append(new_tile(e, neww_ref, lambda: tnew_ref[...]))
        return tiles

    o_win = softmax_pv(win_tiles)

    for e in elems:
        gl = g_ref[e]
        o_ref[e] = (_sigmoid(gl[:, 0:1]) * o_cmp[e] + _sigmoid(gl[:, 1:2]) * o_sel[e]
                    + _sigmoid(gl[:, 2:3]) * o_win[e])


def nsa_sample(proj, ckv, pool_s, win_buf, layer, page_table, rel_bias):
    bsz, t_new, _ = proj.shape
    npages = page_table.shape[1]
    past = npages * PAGE
    n_cmp = (past + t_new - CMP_BLOCK) // CMP_STRIDE + 1
    n_sel = -(-(past + t_new) // SEL_BLOCK)
    assert n_cmp <= LANES and n_sel <= LANES and win_buf.shape[3] == WINDOW and t_new == 4
    assert (past + t_new - 1) // SEL_BLOCK == past // SEL_BLOCK
    bcs, tab, tnew = _sample_tables(rel_bias, past, t_new, n_cmp, win_buf.shape[3])
    n = np.arange(LANES)
    jsel = np.arange(LANES)
    cov = ((n[:, None] * CMP_STRIDE < jsel[None, :] * SEL_BLOCK + SEL_BLOCK)
           & (n[:, None] * CMP_STRIDE + CMP_BLOCK > jsel[None, :] * SEL_BLOCK)
           & (n[:, None] < n_cmp) & (jsel[None, :] < n_sel))
    cov = jnp.asarray(cov, BF16)

    eye = jnp.eye(KV_HEADS, dtype=F32)
    q5 = proj[:, :, C_NQ:C_NQ + NSA_W].reshape(bsz, t_new, KV_HEADS, Q_PER_KV, NSA_HD).transpose(0, 3, 2, 1, 4)
    qpad = (q5[:, :, :, :, None, :] * eye[None, None, :, None, :, None]).reshape(bsz, S_ROWS, KV_W)
    ng = proj[:, :, C_NG:C_NG + 3 * NSA_HEADS].reshape(bsz, t_new, KV_HEADS, Q_PER_KV, 3).transpose(0, 3, 2, 1, 4)
    gl = jnp.pad(ng.reshape(bsz, S_ROWS, 3), ((0, 0), (0, 0), (0, LANES - 3)))
    pad_new = lambda c: jnp.pad(proj[:, :, c:c + 2 * KV_W], ((0, 0), (0, NEW_PAD - t_new), (0, 0)))

    nb = SAMPLE_NB
    assert bsz % nb == 0
    per_b = lambda a: pl.BlockSpec((nb,) + a.shape[1:], lambda b, pt: (b,) + (0,) * (a.ndim - 1))
    const = lambda a: pl.BlockSpec(a.shape, lambda b, pt: (0,) * a.ndim)
    new_s, new_w = pad_new(C_SKV), pad_new(C_WKV)
    grid_spec = pltpu.PrefetchScalarGridSpec(
        num_scalar_prefetch=1,
        grid=(bsz // nb,),
        in_specs=[per_b(qpad), per_b(gl), per_b(ckv)]
        + [pl.BlockSpec((1, 1, 2 * KV_W, PAGE), (lambda b, pt, e=e, p=p: (layer, pt[b * nb + e, p], 0, 0)))
           for e in range(nb) for p in range(npages)]
        + [pl.BlockSpec((1, nb) + win_buf.shape[2:], lambda b, pt: (layer, b, 0, 0)),
           per_b(new_s), per_b(new_w), const(bcs), const(tab), const(tnew), const(cov)],
        out_specs=pl.BlockSpec((nb, S_ROWS, KV_W), lambda b, pt: (b, 0, 0)),
        scratch_shapes=[pltpu.VMEM((nb, S_ROWS, (npages + 1) * LANES), F32)],
    )
    out = pl.pallas_call(
        functools.partial(_nsa_sample_kernel, nb=nb, npages=npages, past=past, t_new=t_new),
        grid_spec=grid_spec,
        out_shape=jax.ShapeDtypeStruct((bsz, S_ROWS, KV_W), F32),
        compiler_params=_cparams(("parallel",)),
        name="nsa_sample",
    )(page_table, qpad, gl, ckv, *([pool_s] * (nb * npages)), win_buf, new_s, new_w, bcs, tab, tnew, cov)
    o6 = out.reshape(bsz, Q_PER_KV, KV_HEADS, t_new, KV_HEADS, NSA_HD)
    o5 = jnp.stack([o6[:, :, g, :, g, :] for g in range(KV_HEADS)], axis=2)
    return o5.transpose(0, 3, 2, 1, 4).reshape(bsz, t_new, NSA_W)


def _in_proj_perm():
    sizes = (ML_W, ML_W, ML_W, ML_W, ML_HEADS, ML_HEADS, NSA_W, KV_W, KV_W, KV_W, KV_W, KV_W, KV_W, 3 * NSA_HEADS)
    off = np.concatenate([[0], np.cumsum(sizes)])
    order = (0, 1, 2, 3, 6, 7, 8, 9, 10, 11, 12, 4, 5, 13)
    return np.concatenate([np.arange(off[k], off[k + 1]) for k in order])


def _layer(x, mods, rows_per_group, lw, tm, tm_ffn, mixer):
    sh1, sc1, g1, sh2, sc2, g2 = mods
    w_in_bf, w_out_bf, w_up_bf, w_down_bf, ln_g, ln_b = lw
    proj = in_proj(x, sc1, sh1, w_in_bf, tm, rows_per_group // tm)
    h_ml, o_nsa, extras = mixer(proj)
    x1 = out_proj_ln(x, h_ml, o_nsa, w_out_bf, g1, ln_g[0], ln_b[0], tm, rows_per_group // tm)
    x2 = ffn_ln(x1, sc2, sh2, g2, w_up_bf, w_down_bf, ln_g[1], ln_b[1], tm_ffn, rows_per_group // tm_ffn)
    return x2, proj, extras


def kernel(x_prompt, x_sample, cache_cmp_kv, cache_slc_kv, cache_win_kv, state_mlstm_C, state_mlstm_n,
           state_mlstm_m, page_table, c_prompt, c_sample, w_ada, b_ada, w_in, b_gate, ml_norm_g, cmp_pe,
           cmp_w1, cmp_w2, rel_bias, w_out, ln_g, ln_b, w_up, w_down):
    bsz, seq, d = x_prompt.shape
    bs, ts, _ = x_sample.shape
    depth = w_in.shape[0]
    n_phys = cache_cmp_kv.shape[1]
    npages = page_table.shape[1]
    win_len = cache_win_kv.shape[2]
    tm = 512
    tm_ffn = 1024
    t_ml = 256
    t_pad = 16

    rows_last = lambda a: a.transpose(0, 1, 3, 4, 5, 2).reshape(a.shape[0], a.shape[1], 2 * KV_W, a.shape[2])
    pool_c, pool_s, win_t = rows_last(cache_cmp_kv), rows_last(cache_slc_kv), rows_last(cache_win_kv)

    perm = _in_proj_perm()
    prompt_tabs = _prompt_tables(rel_bias)
    nc_rows = bsz + bs
    c_all = jnp.pad(jnp.concatenate([c_prompt, c_sample], axis=0), ((0, -nc_rows % 8), (0, 0)))

    xp = x_prompt.reshape(bsz * seq, d)
    xs = x_sample.reshape(bs * ts, d)
    outs = {k: [] for k in ("cmp_p", "cmp_s", "slc_p", "slc_s", "win_p", "win_s",
                            "C_p", "C_s", "n_p", "n_s", "m_p", "m_s")}
    kv_shape = lambda b, t: (b, t, 2, KV_HEADS, NSA_HD)

    for l in range(depth):
        ada = ada_mod(c_all, w_ada[l].astype(BF16), b_ada[l]).reshape(c_all.shape[0], 6, d)
        mods_p = [ada[:bsz, k][:, None, :] for k in range(6)]
        mods_s = [jnp.repeat(ada[bsz:nc_rows, k], ts, axis=0)[None] for k in range(6)]
        w_in_bf = jnp.pad(w_in[l][:, perm], ((0, 0), (0, N_IN_PAD - perm.size))).astype(BF16)
        lw = (w_in_bf, w_out[l].astype(BF16), w_up[l].astype(BF16), w_down[l].astype(BF16), ln_g[l], ln_b[l])
        cw = _compress_weights(cmp_pe[l], cmp_w1[l], cmp_w2[l])

        def prompt_mixer(proj):
            proj = proj.reshape(bsz, seq, N_IN_PAD)
            h_ml, c1, n1, m1 = mlstm(proj, b_gate[l], ml_norm_g[l],
                                     jnp.zeros((bsz, ML_HEADS, ML_HD, ML_HD), F32),
                                     jnp.zeros((bsz, ML_HEADS, ML_HD), F32), jnp.zeros((bsz, ML_HEADS), F32),
                                     nb=bsz, t=t_ml, valid=t_ml)
            ckv = compress(proj, [lambda s: (lambda b: (b, 0, C_CKV // KV_W + s))], 0, (), bsz, 1, seq, cw)
            o_nsa = nsa_prompt(proj, ckv, prompt_tabs)
            return h_ml.reshape(bsz * seq, ML_W), o_nsa.reshape(bsz * seq, NSA_W), (c1, n1, m1)

        def sample_mixer(proj):
            proj = proj.reshape(bs, ts, N_IN_PAD)
            proj_pad = jnp.pad(proj, ((0, 0), (0, t_pad - ts), (0, 0)))
            h_ml, c1, n1, m1 = mlstm(proj_pad, b_gate[l], ml_norm_g[l], state_mlstm_C[l], state_mlstm_n[l],
                                     state_mlstm_m[l], nb=8, t=t_pad, valid=ts)
            pages = [(lambda b, pt, e=e, p=p: (l, pt[b * SAMPLE_NB + e, p], 0, 0))
                     for e in range(SAMPLE_NB) for p in range(npages)]
            ckv = compress(pool_c, pages, 1, (page_table,), bs // SAMPLE_NB, SAMPLE_NB * npages, PAGE, cw,
                           transposed=True, nb=SAMPLE_NB)
            o_nsa = nsa_sample(proj, ckv, pool_s, win_t, l, page_table, rel_bias)
            return (h_ml[:, :ts].reshape(bs * ts, ML_W), o_nsa.reshape(bs * ts, NSA_W).astype(BF16),
                    (c1, n1, m1))

        xp, proj_p, st_p = _layer(xp, mods_p, seq, lw, tm, tm_ffn, prompt_mixer)
        xs, proj_s, st_s = _layer(xs, mods_s, bs * ts, lw, bs * ts, bs * ts, sample_mixer)

        proj_p = proj_p.reshape(bsz, seq, N_IN_PAD)
        proj_s = proj_s.reshape(bs, ts, N_IN_PAD)
        rows = lambda proj, c: proj[:, :, c:c + 2 * KV_W]
        win = min(WINDOW, seq)
        outs["cmp_p"].append(rows(proj_p, C_CKV).reshape(kv_shape(bsz, seq)))
        outs["slc_p"].append(rows(proj_p, C_SKV).reshape(kv_shape(bsz, seq)))
        outs["win_p"].append(rows(proj_p, C_WKV)[:, seq - win:].reshape(kv_shape(bsz, win)))
        outs["cmp_s"].append(rows(proj_s, C_CKV).reshape(kv_shape(bs, ts)))
        outs["slc_s"].append(rows(proj_s, C_SKV).reshape(kv_shape(bs, ts)))
        win_all = jnp.concatenate([cache_win_kv[l], rows(proj_s, C_WKV).reshape(kv_shape(bs, ts))], axis=1)
        outs["win_s"].append(win_all[:, -win_len:])
        for tag, st in (("p", st_p), ("s", st_s)):
            outs["C_" + tag].append(st[0])
            outs["n_" + tag].append(st[1])
            outs["m_" + tag].append(st[2])

    stk = lambda k: jnp.stack(outs[k])
    return (xp.reshape(bsz, seq, d), xs.reshape(bs, ts, d),
            stk("cmp_p"), stk("cmp_s"), stk("slc_p"), stk("slc_s"), stk("win_p"), stk("win_s"),
            stk("C_p"), stk("C_s"), stk("n_p"), stk("n_s"), stk("m_p"), stk("m_s"))
```

```python
import functools
import math

import numpy as np
import jax
import jax.numpy as jnp
from jax import lax
from jax.experimental import pallas as pl
from jax.experimental.pallas import tpu as pltpu

F32 = jnp.float32
BF16 = jnp.bfloat16

D_MODEL = 1024
DEPTH = 2
PAGE = 128
ML_HEADS = 4
ML_HD = 128
ML_W = ML_HEADS * ML_HD
NSA_HEADS = 8
NSA_HD = 64
NSA_W = NSA_HEADS * NSA_HD
KV_HEADS = 2
Q_PER_KV = 4
KV_W = KV_HEADS * NSA_HD
CMP_BLOCK = 32
CMP_STRIDE = 16
CMP_HIDDEN = 256
SEL_BLOCK = 64
N_SELECT = 16
WINDOW = 512
NUM_BUCKETS = 32
REL_MAX_DIST = 128
D_FF = 4 * D_MODEL
ALPHA = (2 * DEPTH) ** 0.25
LN_EPS = 1e-5
NEG = -1e30
FORCE = 1e9
REMOVED = -3e38
LOG2E = math.log2(math.e)

LANES = 128
QT = 128
KT = 128
V_ROWS = 80
NC_PAD_FRONT = 16
NEAR_ROWS = 24

C_MQ, C_MK, C_MV, C_MO = 0, 512, 1024, 1536
C_NQ = 2048
C_CKV, C_SKV, C_WKV = 2560, 2816, 3072
C_GATE = 3328
C_NG = 3336
N_IN_PAD = 3456
VMEM_LIMIT = 56 * 1024 * 1024


def _cparams(sem):
    return pltpu.CompilerParams(dimension_semantics=sem, vmem_limit_bytes=VMEM_LIMIT)


def _dot(a, b):
    return jnp.dot(a, b, preferred_element_type=F32)


def _dot_nt(a, b):
    return lax.dot_general(a, b, (((1,), (1,)), ((), ())), preferred_element_type=F32)


def _dot_tn(a, b):
    return lax.dot_general(a, b, (((0,), (0,)), ((), ())), preferred_element_type=F32)


def _sigmoid(x):
    return 1.0 / (1.0 + jnp.exp(-x))


def _layer_norm(y, g, b):
    mu = jnp.mean(y, axis=-1, keepdims=True)
    d = y - mu
    var = jnp.mean(d * d, axis=-1, keepdims=True)
    return d * lax.rsqrt(var + LN_EPS) * g + b


def _ada_kernel(c_ref, w_ref, b_ref, o_ref):
    c = c_ref[...]
    a = (c * _sigmoid(c)).astype(BF16)
    o_ref[...] = _dot(a, w_ref[...]) + b_ref[...]


def ada_mod(c, w_bf, b):
    m, d = c.shape
    n = w_bf.shape[1]
    tn = 1536
    return pl.pallas_call(
        _ada_kernel,
        grid=(n // tn,),
        in_specs=[pl.BlockSpec((m, d), lambda j: (0, 0)),
                  pl.BlockSpec((d, tn), lambda j: (0, j)),
                  pl.BlockSpec((1, tn), lambda j: (0, j))],
        out_specs=pl.BlockSpec((m, tn), lambda j: (0, j)),
        out_shape=jax.ShapeDtypeStruct((m, n), F32),
        compiler_params=_cparams(("arbitrary",)),
        name="ada_mod",
    )(c, w_bf, b.reshape(1, n))


def _inproj_kernel(x_ref, sc_ref, sh_ref, w_ref, o_ref, kv_ref):
    u = (x_ref[...] * (1.0 + sc_ref[0]) + sh_ref[0]).astype(BF16)
    y = _dot(u, w_ref[...])
    o_ref[...] = y
    kv_ref[...] = y[:, C_SKV:C_WKV + 2 * KV_W].astype(BF16)


def in_proj(x, sc, sh, w_bf, tm, tiles_per_group):
    m = x.shape[0]
    n = w_bf.shape[1]
    r = sc.shape[1]
    mod = pl.BlockSpec((1, r, D_MODEL), lambda i: (i // tiles_per_group, 0, 0))
    return pl.pallas_call(
        _inproj_kernel,
        grid=(m // tm,),
        in_specs=[pl.BlockSpec((tm, D_MODEL), lambda i: (i, 0)), mod, mod,
                  pl.BlockSpec((D_MODEL, n), lambda i: (0, 0))],
        out_specs=[pl.BlockSpec((tm, n), lambda i: (i, 0)), pl.BlockSpec((tm, 4 * KV_W), lambda i: (i, 0))],
        out_shape=[jax.ShapeDtypeStruct((m, n), F32), jax.ShapeDtypeStruct((m, 4 * KV_W), BF16)],
        compiler_params=_cparams(("parallel",)),
        name="in_proj",
    )(x, sc, sh, w_bf)


def _outproj_kernel(x_ref, a1_ref, a2_ref, w1_ref, w2_ref, g_ref, lg_ref, lb_ref, o_ref):
    mixed = _dot(a1_ref[...], w1_ref[...]) + _dot(a2_ref[...], w2_ref[...])
    y = ALPHA * x_ref[...] + g_ref[0] * mixed
    o_ref[...] = _layer_norm(y, lg_ref[...], lb_ref[...])


def out_proj_ln(x, a1, a2, w_bf, gate, ln_g, ln_b, tm, tiles_per_group):
    m = x.shape[0]
    k1 = a1.shape[1]
    r = gate.shape[1]
    return pl.pallas_call(
        _outproj_kernel,
        grid=(m // tm,),
        in_specs=[pl.BlockSpec((tm, D_MODEL), lambda i: (i, 0)),
                  pl.BlockSpec((tm, k1), lambda i: (i, 0)),
                  pl.BlockSpec((tm, k1), lambda i: (i, 0)),
                  pl.BlockSpec((k1, D_MODEL), lambda i: (0, 0)),
                  pl.BlockSpec((k1, D_MODEL), lambda i: (1, 0)),
                  pl.BlockSpec((1, r, D_MODEL), lambda i: (i // tiles_per_group, 0, 0)),
                  pl.BlockSpec((1, D_MODEL), lambda i: (0, 0)),
                  pl.BlockSpec((1, D_MODEL), lambda i: (0, 0))],
        out_specs=pl.BlockSpec((tm, D_MODEL), lambda i: (i, 0)),
        out_shape=jax.ShapeDtypeStruct((m, D_MODEL), F32),
        compiler_params=_cparams(("parallel",)),
        name="out_proj_ln",
    )(x, a1, a2, w_bf, w_bf, gate, ln_g.reshape(1, -1), ln_b.reshape(1, -1))


def _ffn_kernel(x_ref, sc_ref, sh_ref, g_ref, wu_ref, wd_ref, lg_ref, lb_ref, o_ref, u_scr, acc_scr):
    f = pl.program_id(1)

    @pl.when(f == 0)
    def _():
        u_scr[...] = (x_ref[...] * (1.0 + sc_ref[0]) + sh_ref[0]).astype(BF16)
        acc_scr[...] = jnp.zeros_like(acc_scr)

    h = jnp.maximum(_dot(u_scr[...], wu_ref[...]), 0.0)
    acc_scr[...] += _dot((h * h).astype(BF16), wd_ref[...])

    @pl.when(f == pl.num_programs(1) - 1)
    def _():
        y = ALPHA * x_ref[...] + g_ref[0] * acc_scr[...]
        o_ref[...] = _layer_norm(y, lg_ref[...], lb_ref[...])


def ffn_ln(x, sc, sh, gate, wu_bf, wd_bf, ln_g, ln_b, tm, tiles_per_group):
    m = x.shape[0]
    tf = 512
    r = sc.shape[1]
    mod = pl.BlockSpec((1, r, D_MODEL), lambda i, f: (i // tiles_per_group, 0, 0))
    return pl.pallas_call(
        _ffn_kernel,
        grid=(m // tm, D_FF // tf),
        in_specs=[pl.BlockSpec((tm, D_MODEL), lambda i, f: (i, 0)),
                  mod, mod, mod,
                  pl.BlockSpec((D_MODEL, tf), lambda i, f: (0, f)),
                  pl.BlockSpec((tf, D_MODEL), lambda i, f: (f, 0)),
                  pl.BlockSpec((1, D_MODEL), lambda i, f: (0, 0)),
                  pl.BlockSpec((1, D_MODEL), lambda i, f: (0, 0))],
        out_specs=pl.BlockSpec((tm, D_MODEL), lambda i, f: (i, 0)),
        out_shape=jax.ShapeDtypeStruct((m, D_MODEL), F32),
        scratch_shapes=[pltpu.VMEM((tm, D_MODEL), BF16), pltpu.VMEM((tm, D_MODEL), F32)],
        compiler_params=_cparams(("parallel", "arbitrary")),
        name="ffn_ln",
    )(x, sc, sh, gate, wu_bf, wd_bf, ln_g.reshape(1, -1), ln_b.reshape(1, -1))


def _mlstm_kernel(bg_ref, q_ref, k_ref, v_ref, o_ref, g_ref, ng_ref, c0_ref, n0_ref, m0_ref,
                  h_ref, c_out, n_out, m_out, c_scr, n_scr, m_scr, *, nb, t, valid):
    ci = pl.program_id(1)

    @pl.when(ci == 0)
    def _():
        c_scr[...] = c0_ref[...]
        n_scr[...] = n0_ref[...]
        m_scr[...] = m0_ref[...]

    row = lax.broadcasted_iota(jnp.int32, (t, t), 0)
    col = lax.broadcasted_iota(jnp.int32, (t, t), 1)
    tri = col <= row
    tri_t = row <= col
    eye = row == col
    row1 = lax.broadcasted_iota(jnp.int32, (t, 1), 0)
    scale = ML_HD ** -0.5

    def gate_scans(b, h):
        gates = g_ref[b]
        ig_col = gates[:, h:h + 1] + bg_ref[h]
        fr = gates[:, ML_HEADS + h:ML_HEADS + h + 1] + bg_ref[ML_HEADS + h]
        lf_col = jnp.minimum(fr, 0.0) - jnp.log1p(jnp.exp(-jnp.abs(fr)))
        if valid < t:
            ig_col = jnp.where(row1 < valid, ig_col, NEG)
            lf_col = jnp.where(row1 < valid, lf_col, 0.0)
        lf_row = jnp.sum(jnp.where(eye, lf_col, 0.0), axis=0, keepdims=True)
        ig_row = jnp.sum(jnp.where(eye, ig_col, 0.0), axis=0, keepdims=True)
        f_col = jnp.sum(jnp.where(tri, lf_row, 0.0), axis=1, keepdims=True)
        f_row = jnp.sum(jnp.where(tri_t, lf_col, 0.0), axis=0, keepdims=True)
        a_row = ig_row - f_row
        a_col = ig_col - f_col
        m0 = m_scr[b, h]
        cm_col = jnp.max(jnp.where(tri, a_row, NEG), axis=1, keepdims=True)
        g_col = jnp.maximum(m0, cm_col)
        g_end = jnp.maximum(m0, jnp.max(a_row, axis=1, keepdims=True))
        return dict(dmat=jnp.exp(jnp.where(tri, a_row - g_col, NEG)), decay_col=jnp.exp(m0 - g_col),
                    m_col=f_col + g_col, m_end=jnp.sum(lf_row, axis=1, keepdims=True) + g_end,
                    w_end_col=jnp.exp(a_col - g_end), carry=jnp.exp(m0 - g_end))

    streams = [(b, h) for b in range(nb) for h in range(ML_HEADS)]
    group = 2 * ML_HEADS
    for g0 in range(0, len(streams), group):
        grp = streams[g0:g0 + group]
        hs = [slice(h * ML_HD, (h + 1) * ML_HD) for _, h in grp]
        sc = [gate_scans(b, h) for b, h in grp]
        qf = [q_ref[b][:, s] for (b, _), s in zip(grp, hs)]
        qb = [x.astype(BF16) for x in qf]
        kf = [k_ref[b][:, s] * scale for (b, _), s in zip(grp, hs)]
        kb = [x.astype(BF16) for x in kf]
        vb = [v_ref[b][:, s].astype(BF16) for (b, _), s in zip(grp, hs)]
        cmat = [c_scr[b, h] for b, h in grp]
        nvec = [n_scr[b, h] for b, h in grp]
        n_st = range(len(grp))

        w = [_dot_nt(qb[i], kb[i]) * sc[i]["dmat"] for i in n_st]
        qc = [_dot(qb[i], cmat[i].astype(BF16)) for i in n_st]
        num = [_dot(w[i].astype(BF16), vb[i]) + qc[i] * sc[i]["decay_col"] for i in n_st]
        den = [jnp.sum(w[i], axis=1, keepdims=True)
               + jnp.sum(qf[i] * nvec[i], axis=1, keepdims=True) * sc[i]["decay_col"] for i in n_st]
        hh = [num[i] / jnp.maximum(jnp.abs(den[i]), jnp.exp(-sc[i]["m_col"])) for i in n_st]
        mu = [jnp.mean(x, axis=1, keepdims=True) for x in hh]
        dd = [hh[i] - mu[i] for i in n_st]
        var = [jnp.mean(x * x, axis=1, keepdims=True) for x in dd]
        for i, (b, h) in enumerate(grp):
            hn = dd[i] * lax.rsqrt(var[i] + LN_EPS) * ng_ref[:, hs[i]] * _sigmoid(o_ref[b][:, hs[i]])
            h_ref[b, :, hs[i]] = hn.astype(h_ref.dtype)

        kw = [kf[i] * sc[i]["w_end_col"] for i in n_st]
        upd = [_dot_tn(kw[i].astype(BF16), vb[i]) for i in n_st]
        for i, (b, h) in enumerate(grp):
            c_scr[b, h] = sc[i]["carry"] * cmat[i] + upd[i]
            n_scr[b, h] = sc[i]["carry"] * nvec[i] + jnp.sum(kw[i], axis=0, keepdims=True)
            m_scr[b, h] = sc[i]["m_end"]

    @pl.when(ci == pl.num_programs(1) - 1)
    def _():
        c_out[...] = c_scr[...]
        n_out[...] = n_scr[...]
        m_out[...] = m_scr[...]


def mlstm(proj, b_gate, norm_g, c0, n0, m0, *, nb, t, valid, layer=None):
    bsz, length, _ = proj.shape
    nchunk = length // t
    wide = lambda cb: pl.BlockSpec((nb, t, ML_W), lambda i, c: (i, c, cb))
    state4 = lambda s: pl.BlockSpec((nb,) + s, lambda i, c: (i, 0, 0, 0))
    if layer is None:
        state_in = state4
        lead = (bsz,)
    else:
        state_in = lambda s: pl.BlockSpec((None, nb) + s, lambda i, c: (layer, i, 0, 0, 0))
        lead = (c0.shape[0], bsz)
    kern = functools.partial(_mlstm_kernel, nb=nb, t=t, valid=valid)
    h, c1, n1, m1 = pl.pallas_call(
        kern,
        grid=(bsz // nb, nchunk),
        in_specs=[pl.BlockSpec(memory_space=pltpu.SMEM),
                  wide(C_MQ // ML_W), wide(C_MK // ML_W), wide(C_MV // ML_W), wide(C_MO // ML_W),
                  pl.BlockSpec((nb, t, LANES), lambda i, c: (i, c, C_GATE // LANES)),
                  pl.BlockSpec((1, ML_W), lambda i, c: (0, 0)),
                  state_in((ML_HEADS, ML_HD, ML_HD)), state_in((ML_HEADS, 1, ML_HD)), state_in((ML_HEADS, 1, 1))],
        out_specs=[pl.BlockSpec((nb, t, ML_W), lambda i, c: (i, c, 0)),
                   state4((ML_HEADS, ML_HD, ML_HD)), state4((ML_HEADS, 1, ML_HD)), state4((ML_HEADS, 1, 1))],
        out_shape=[jax.ShapeDtypeStruct((bsz, length, ML_W), BF16),
                   jax.ShapeDtypeStruct((bsz, ML_HEADS, ML_HD, ML_HD), F32),
                   jax.ShapeDtypeStruct((bsz, ML_HEADS, 1, ML_HD), F32),
                   jax.ShapeDtypeStruct((bsz, ML_HEADS, 1, 1), F32)],
        scratch_shapes=[pltpu.VMEM((nb, ML_HEADS, ML_HD, ML_HD), F32),
                        pltpu.VMEM((nb, ML_HEADS, 1, ML_HD), F32),
                        pltpu.VMEM((nb, ML_HEADS, 1, 1), F32)],
        compiler_params=_cparams(("parallel", "arbitrary")),
        name="mlstm",
    )(b_gate, proj, proj, proj, proj, proj, norm_g.reshape(1, ML_W),
      c0, n0.reshape(lead + (ML_HEADS, 1, ML_HD)), m0.reshape(lead + (ML_HEADS, 1, 1)))
    return h, c1, n1.reshape(bsz, ML_HEADS, ML_HD), m1.reshape(bsz, ML_HEADS)


def _gelu_tanh(x):
    return 0.5 * x * (1.0 + jnp.tanh(math.sqrt(2.0 / math.pi) * (x + 0.044715 * (x * x * x))))


def _compress_kernel(*refs, npages, rows, transposed, nb):
    jp = rows // CMP_STRIDE
    j = npages * jp
    lane = lax.broadcasted_iota(jnp.int32, (j, LANES), 1)
    low = lane < NSA_HD
    if transposed:
        pe_ref, w1_ref, w2_ref, o_ref, xk_scr, xv_scr = refs[-6:]
        page_refs = refs[-6 - npages:-6]
        for p in range(npages):
            x = page_refs[p][0, 0].T
            xk_scr[p * rows:(p + 1) * rows, :] = x[:, :KV_W]
            xv_scr[p * rows:(p + 1) * rows, :] = x[:, KV_W:]

        def chunk_phase(c, s):
            return (xk_scr, xv_scr)[s][pl.ds(c, j, stride=CMP_STRIDE), :]
    else:
        pe_ref, w1_ref, w2_ref, o_ref = refs[-4:]
        page_refs = refs[-4 - 2 * npages:-4]

        def chunk_phase(c, s):
            parts = [page_refs[2 * p + s][0, pl.ds(c, jp, stride=CMP_STRIDE), :] for p in range(npages)]
            return parts[0] if npages == 1 else jnp.concatenate(parts, axis=0)

    out = jnp.zeros((j, 2 * KV_HEADS * NSA_HD), F32)
    for s in range(2):
        halves = [[], []]
        for c in range(0, CMP_STRIDE, 2):
            va = chunk_phase(c, s)
            vb = chunk_phase(c + 1, s)
            halves[0].append(jnp.where(low, va, pltpu.roll(vb, NSA_HD, axis=1)).astype(BF16))
            halves[1].append(jnp.where(low, pltpu.roll(va, NSA_HD, axis=1), vb).astype(BF16))
        pe_c = _dot(pe_ref[s], w1_ref[s])
        pe_const = pe_c[0:1, :CMP_HIDDEN] + pe_c[1:2, CMP_HIDDEN:]
        for g in range(KV_HEADS):
            lhs = jnp.concatenate(halves[g], axis=1)
            acc = _dot(lhs, w1_ref[s])
            hid = acc[:, :CMP_HIDDEN] + pltpu.roll(acc[:, CMP_HIDDEN:], j - 1, axis=0) + pe_const
            out = out + _dot(_gelu_tanh(hid).astype(BF16), w2_ref[s * KV_HEADS + g])
    je = j // nb
    for e in range(nb):
        o_ref[e] = out[e * je:(e + 1) * je, :]


def _compress_weights(pe, w1, w2):
    w1r = w1.reshape(2, CMP_BLOCK, NSA_HD, CMP_HIDDEN)
    wa = w1r[:, :CMP_STRIDE].reshape(2, 4, 4 * NSA_HD, CMP_HIDDEN)
    wb = w1r[:, CMP_STRIDE:].reshape(2, 4, 4 * NSA_HD, CMP_HIDDEN)
    w1p = jnp.concatenate([wa, wb], axis=-1).astype(BF16).reshape(2, CMP_STRIDE * NSA_HD, 2 * CMP_HIDDEN)
    pea = pe[:, :CMP_STRIDE].reshape(2, 1, CMP_STRIDE * NSA_HD)
    peb = pe[:, CMP_STRIDE:].reshape(2, 1, CMP_STRIDE * NSA_HD)
    pep = jnp.concatenate([pea, peb, jnp.zeros((2, 14, CMP_STRIDE * NSA_HD), F32)], axis=1).astype(BF16)
    w2p = jnp.zeros((2, KV_HEADS, CMP_HIDDEN, 2, KV_HEADS, NSA_HD), F32)
    for s in range(2):
        for g in range(KV_HEADS):
            w2p = w2p.at[s, g, :, s, g, :].set(w2[s])
    w2p = w2p.reshape(2 * KV_HEADS, CMP_HIDDEN, 2 * KV_W).astype(BF16)
    return pep, w1p, w2p


def compress(pages, page_index_maps, num_scalar_prefetch, prefetch, grid_n, npages, rows, cw, transposed=False,
             nb=1):
    pep, w1p, w2p = cw
    j = npages * rows // CMP_STRIDE
    const = lambda nd: (lambda *a: (0,) * nd)
    if transposed:
        page_specs = [pl.BlockSpec((1, 1, 2 * KV_W, rows), im) for im in page_index_maps]
        scratch = [pltpu.VMEM((npages * rows, KV_W), F32)] * 2
    else:
        page_specs = [pl.BlockSpec((1, rows, KV_W), im(s)) for im in page_index_maps for s in range(2)]
        scratch = []
    grid_spec = pltpu.PrefetchScalarGridSpec(
        num_scalar_prefetch=num_scalar_prefetch,
        grid=(grid_n,),
        in_specs=page_specs
        + [pl.BlockSpec(pep.shape, const(3)), pl.BlockSpec(w1p.shape, const(3)), pl.BlockSpec(w2p.shape, const(3))],
        out_specs=pl.BlockSpec((nb, j // nb, 2 * KV_W), lambda b, *a: (b, 0, 0)),
        scratch_shapes=scratch,
    )
    return pl.pallas_call(
        functools.partial(_compress_kernel, npages=npages, rows=rows, transposed=transposed, nb=nb),
        grid_spec=grid_spec,
        out_shape=jax.ShapeDtypeStruct((grid_n * nb, j // nb, 2 * KV_W), F32),
        compiler_params=_cparams(("parallel",)),
        name="compress",
    )(*prefetch, *([pages] * len(page_specs)), pep, w1p, w2p)


def _rel_bucket(dist):
    n = jnp.maximum(dist, 0)
    max_exact = NUM_BUCKETS // 2
    nf = jnp.maximum(n, 1).astype(F32)
    large = max_exact + (jnp.log(nf / max_exact) / math.log(REL_MAX_DIST / max_exact)
                         * (NUM_BUCKETS - max_exact)).astype(jnp.int32)
    large = jnp.minimum(large, NUM_BUCKETS - 1)
    return jnp.where(n < max_exact, n, large)


def _delta_bias(rel_bias, dist):
    far = rel_bias[NUM_BUCKETS - 1]
    val = rel_bias[_rel_bucket(dist)] - far
    return jnp.where(((dist >= 0) & (dist < REL_MAX_DIST))[..., None], val, 0.0)


def _lanes_ri(tab, g):
    rows = tab.shape[0]
    return tab[:, :, g * Q_PER_KV:(g + 1) * Q_PER_KV].transpose(0, 2, 1).reshape(rows, Q_PER_KV * QT)


def _prompt_tables(rel_bias):
    i = jnp.arange(QT)[None, :]
    j = jnp.arange(KT)[:, None]
    d_diag = i - j
    diag = jnp.where((d_diag >= 0)[..., None], _delta_bias(rel_bias, d_diag), NEG)
    sub = _delta_bias(rel_bias, i + KT - j)
    n2 = jnp.arange(NEAR_ROWS)[:, None]
    near = _delta_bias(rel_bias, i + (CMP_STRIDE * NC_PAD_FRONT - CMP_BLOCK + 1) - CMP_STRIDE * n2)
    stack = lambda tab: jnp.stack([_lanes_ri(tab, g) for g in range(KV_HEADS)]) * LOG2E
    return stack(sub), stack(diag), stack(near)


def _cover_t(n_sel, n_cmp, ncp):
    n = np.arange(ncp) - NC_PAD_FRONT
    c_start = n * CMP_STRIDE
    s_start = np.arange(n_sel)[:, None] * SEL_BLOCK
    cov = (c_start[None, :] < s_start + SEL_BLOCK) & (c_start[None, :] + CMP_BLOCK > s_start)
    cov &= ((n >= 0) & (n < n_cmp))[None, :]
    return jnp.asarray(cov, BF16)


def _nsa_prompt_kernel(q_ref, gt_ref, kc_ref, vct_ref, cov_ref, bct_ref, ksa_ref, vst_ref, kw_ref, vwt_ref,
                       tsub_ref, tdiag_ref, o_ref, s_scr, p_scr, q_scr, sa_scr, sb_scr, m_scr, l_scr, acc_scr,
                       ocmp_scr, osel_scr, ot_scr, *, nsel, ncp, ncv):
    blk = pl.program_id(1)
    t0 = blk * QT
    lanes4 = Q_PER_KV * QT
    q_t = (q_ref[0] * (NSA_HD ** -0.5 * LOG2E)).T

    def tile4(x):
        return jnp.concatenate([x] * Q_PER_KV, axis=1)

    def weighted_values(scores, m, values):
        probs = [jnp.exp2(s - m).astype(BF16) for s in scores]
        p_all = probs[0] if len(probs) == 1 else jnp.concatenate(probs, axis=0)
        v_all = values[0] if len(values) == 1 else jnp.concatenate(values, axis=1)
        return _dot(v_all, p_all)

    def tiles_max(scores, m):
        for s in scores:
            m = jnp.maximum(m, jnp.max(s, axis=0, keepdims=True))
        return m

    def normalise(acc, m):
        inv = jnp.where(m > 0.5 * NEG, 1.0 / acc[NSA_HD:NSA_HD + 1, :], 0.0)
        return acc[0:NSA_HD, :] * inv

    def update(g, scores, values):
        m_old = m_scr[g]
        m_new = tiles_max(scores, m_old)
        acc_scr[g] = jnp.exp2(m_old - m_new) * acc_scr[g] + weighted_values(scores, m_new, values)
        m_scr[g] = m_new

    row_k = lax.broadcasted_iota(jnp.int32, (KT, QT), 0)
    lane_q = lax.broadcasted_iota(jnp.int32, (KT, QT), 1)

    importance = []
    for g in range(KV_HEADS):
        gs = slice(g * NSA_HD, (g + 1) * NSA_HD)
        qg = jnp.concatenate([q_t[(g * Q_PER_KV + r) * NSA_HD:(g * Q_PER_KV + r + 1) * NSA_HD, :]
                              for r in range(Q_PER_KV)], axis=1)
        zero = jnp.zeros_like(qg)
        qpad = jnp.concatenate([qg, zero] if g == 0 else [zero, qg], axis=0).astype(BF16)
        q_scr[g, 0:KV_W, :] = qpad

        s_scr[...] = _dot(kc_ref[0], qpad)
        near0 = pl.multiple_of(blk * (QT // CMP_STRIDE), 8)
        s_scr[pl.ds(near0, NEAR_ROWS), :] = s_scr[pl.ds(near0, NEAR_ROWS), :] + bct_ref[g]
        n_all = blk * (QT // CMP_STRIDE) - (CMP_BLOCK // CMP_STRIDE)
        n_any = n_all + (QT - 1) // CMP_STRIDE + 1
        m_scr[g] = jnp.full((1, lanes4), NEG, F32)
        l_scr[g] = jnp.zeros((1, lanes4), F32)
        nchunk = ncp // KT
        chunk_kind = []
        for c in range(nchunk):
            cs = slice(c * KT, (c + 1) * KT)
            lo, hi = c * KT - NC_PAD_FRONT, (c + 1) * KT - NC_PAD_FRONT - 1
            skip = (lo > n_any) if lo < ncv else True
            full = (hi <= n_all) if (lo >= 0 and hi < ncv) else False
            chunk_kind.append((cs, skip, full))
            if skip is True:
                continue

            @pl.when(full)
            def _(cs=cs):
                m_scr[g] = jnp.maximum(m_scr[g], jnp.max(s_scr[cs, :], axis=0, keepdims=True))

            @pl.when(jnp.logical_not(full) & jnp.logical_not(skip))
            def _(cs=cs, lo=lo):
                n = row_k + lo
                ok = (n >= 0) & (n < ncv) & (CMP_STRIDE * n + (CMP_BLOCK - 1) <= t0 + lane_q)
                sc = s_scr[cs, :] + tile4(jnp.where(ok, 0.0, NEG))
                s_scr[cs, :] = sc
                m_scr[g] = jnp.maximum(m_scr[g], jnp.max(sc, axis=0, keepdims=True))

        mx = m_scr[g]
        for cs, skip, _ in chunk_kind:
            if skip is True:
                p_scr[cs, :] = jnp.zeros((KT, lanes4), BF16)
                continue

            @pl.when(jnp.logical_not(skip))
            def _(cs=cs):
                p = jnp.exp2(s_scr[cs, :] - mx)
                l_scr[g] = l_scr[g] + jnp.sum(p, axis=0, keepdims=True)
                p_scr[cs, :] = p.astype(BF16)

            @pl.when(skip)
            def _(cs=cs):
                p_scr[cs, :] = jnp.zeros((KT, lanes4), BF16)

        inv_c = jnp.where(mx > 0.5 * NEG, 1.0 / l_scr[g], 0.0)
        ocmp_scr[g] = _dot(vct_ref[0, gs, :], p_scr[...]) * inv_c
        imp4 = _dot(cov_ref[...], p_scr[...]) * inv_c
        imp = imp4[:, 0:QT]
        for r in range(1, Q_PER_KV):
            imp = imp + imp4[:, r * QT:(r + 1) * QT]
        importance.append(imp)

    jb = lax.broadcasted_iota(jnp.int32, (nsel, QT), 0)
    tq = t0 + lax.broadcasted_iota(jnp.int32, (nsel, QT), 1)
    cur = tq // SEL_BLOCK
    forced = (jb == 0) | (jb == cur) | (jb == cur - 1)
    future = jb * SEL_BLOCK > tq
    score = [jnp.where(forced, FORCE, jnp.where(future, -FORCE, imp)) for imp in importance]
    chosen_any = [jnp.zeros((nsel, QT), jnp.bool_) for _ in range(KV_HEADS)]
    for _ in range(min(N_SELECT, nsel)):
        for g in range(KV_HEADS):
            best = jnp.max(score[g], axis=0, keepdims=True)
            first = jnp.min(jnp.where(score[g] == best, jb, nsel), axis=0, keepdims=True)
            chosen = jb == first
            chosen_any[g] = chosen_any[g] | chosen
            score[g] = jnp.where(chosen, REMOVED, score[g])
    for g in range(KV_HEADS):
        q_scr[g, KV_W:KV_W + nsel, :] = tile4(jnp.where(chosen_any[g], 0.0, NEG)).astype(BF16)
        if nsel < LANES:
            q_scr[g, KV_W + nsel:, :] = jnp.zeros((LANES - nsel, lanes4), BF16)

    def values_of(ref, kt, g):
        return ref[0, kt, g * V_ROWS:(g + 1) * V_ROWS, :]

    def qk_pair(k, dst):
        for g in range(KV_HEADS):
            dst[g, 0:KT, :] = _dot(ksa_ref[0, 2 * k], q_scr[g])
            dst[g, KT:2 * KT, :] = _dot(ksa_ref[0, 2 * k + 1], q_scr[g])

    def softmax_pair(k, src):
        for g in range(KV_HEADS):
            update(g, [src[g, 0:KT, :], src[g, KT:2 * KT, :]],
                   [values_of(vst_ref, 2 * k, g), values_of(vst_ref, 2 * k + 1, g)])

    m_scr[...] = jnp.full(m_scr.shape, NEG, F32)
    acc_scr[...] = jnp.zeros(acc_scr.shape, F32)
    n_far = jnp.maximum(blk - 1, 0)
    npairs = n_far // 2

    @pl.when(npairs > 0)
    def _():
        qk_pair(0, sa_scr)

    def two_pairs(k0):
        qk_pair(jnp.minimum(k0 + 1, npairs - 1), sb_scr)
        softmax_pair(k0, sa_scr)
        qk_pair(jnp.minimum(k0 + 2, npairs - 1), sa_scr)
        softmax_pair(k0 + 1, sb_scr)

    def far_body4(j, carry):
        two_pairs(4 * j)
        two_pairs(4 * j + 2)
        return carry

    def far_body2(j, carry):
        two_pairs(4 * (npairs // 4) + 2 * j)
        return carry

    lax.fori_loop(0, npairs // 4, far_body4, 0)
    lax.fori_loop(0, (npairs % 4) // 2, far_body2, 0)

    @pl.when(npairs % 2 == 1)
    def _():
        softmax_pair(npairs - 1, sa_scr)

    def gate(valid):
        return jnp.where(valid, 0.0, NEG).astype(F32)

    anti = tile4(jnp.where(row_k > lane_q, 0.0, NEG))
    sel_tiles = ((blk - 2, lambda g: gate(n_far % 2 == 1)),
                 (blk - 1, lambda g: tsub_ref[g] + gate(blk >= 1)),
                 (blk, lambda g: tdiag_ref[g]))
    win_tiles = ((blk - 4, lambda g: anti + gate(blk >= 4)),
                 (blk - 3, lambda g: gate(blk >= 3)),
                 (blk - 2, lambda g: gate(blk >= 2)),
                 (blk - 1, lambda g: tsub_ref[g] + gate(blk >= 1)),
                 (blk, lambda g: tdiag_ref[g]))
    sel_scores, win_scores = [], []
    for g in range(KV_HEADS):
        sel_scores.append([_dot(ksa_ref[0, jnp.maximum(kt, 0)], q_scr[g]) + add(g) for kt, add in sel_tiles])
        win_scores.append([_dot(kw_ref[0, jnp.maximum(kt, 0)], q_scr[g, 0:KV_W, :]) + add(g)
                           for kt, add in win_tiles])
    o_win = []
    for g in range(KV_HEADS):
        update(g, sel_scores[g], [values_of(vst_ref, jnp.maximum(kt, 0), g) for kt, _ in sel_tiles])
        osel_scr[g] = normalise(acc_scr[g], m_scr[g])
        m_win = tiles_max(win_scores[g], jnp.full((1, lanes4), NEG, F32))
        acc_win = weighted_values(win_scores[g], m_win,
                                  [values_of(vwt_ref, jnp.maximum(kt, 0), g) for kt, _ in win_tiles])
        o_win.append(normalise(acc_win, m_win))

    gl = gt_ref[0, 0]
    for g in range(KV_HEADS):
        o_g = (_sigmoid(gl[3 * g:3 * g + 1, :]) * ocmp_scr[g] + _sigmoid(gl[3 * g + 1:3 * g + 2, :]) * osel_scr[g]
               + _sigmoid(gl[3 * g + 2:3 * g + 3, :]) * o_win[g])
        for r in range(Q_PER_KV):
            h = g * Q_PER_KV + r
            ot_scr[h * NSA_HD:(h + 1) * NSA_HD, :] = o_g[:, r * QT:(r + 1) * QT]

    o_ref[0] = ot_scr[...].T.astype(o_ref.dtype)


def nsa_prompt(proj, kv_bf, ckv, rel_tabs):
    bsz, seq, _ = proj.shape
    nq = seq // QT
    n_cmp = (seq - CMP_BLOCK) // CMP_STRIDE + 1
    nsel = seq // SEL_BLOCK
    ncp = -(-(NC_PAD_FRONT + seq // CMP_STRIDE) // KT) * KT
    tsub, tdiag, bct = rel_tabs

    def tiles(cols):
        c0 = cols - C_SKV
        return kv_bf[:, :, c0:c0 + KV_W].reshape(bsz, nq, KT, KV_W)

    assert nsel <= LANES
    blk_of_key = np.arange(seq) // SEL_BLOCK
    onehot = jnp.asarray(blk_of_key[:, None] == np.arange(LANES)[None, :], BF16).reshape(nq, KT, LANES)
    ksa = jnp.concatenate([tiles(C_SKV), jnp.broadcast_to(onehot, (bsz, nq, KT, LANES))], axis=-1)

    def values_t(cols):
        vt = tiles(cols).reshape(bsz, nq, KT, KV_HEADS, NSA_HD).transpose(0, 1, 3, 4, 2)
        ones = jnp.ones((bsz, nq, KV_HEADS, 1, KT), BF16)
        zeros = jnp.zeros((bsz, nq, KV_HEADS, V_ROWS - NSA_HD - 1, KT), BF16)
        return jnp.concatenate([vt, ones, zeros], axis=3).reshape(bsz, nq, KV_HEADS * V_ROWS, KT)

    vst = values_t(C_SKV + KV_W)
    kw = tiles(C_WKV)
    vwt = values_t(C_WKV + KV_W)
    back = ncp - NC_PAD_FRONT - ckv.shape[1]
    kc = jnp.pad(ckv[:, :, :KV_W], ((0, 0), (NC_PAD_FRONT, back), (0, 0))).astype(BF16)
    vct = jnp.pad(ckv[:, :, KV_W:], ((0, 0), (NC_PAD_FRONT, back), (0, 0))).astype(BF16).transpose(0, 2, 1)
    cov = _cover_t(nsel, n_cmp, ncp)
    ng = proj[:, :, C_NG:C_NG + 3 * NSA_HEADS].reshape(bsz, nq, QT, KV_HEADS, Q_PER_KV, 3)
    gt = ng.transpose(0, 1, 3, 5, 4, 2).reshape(bsz, nq, KV_HEADS * 3, Q_PER_KV * QT)
    gt = jnp.pad(gt, ((0, 0), (0, 0), (0, 2), (0, 0)))

    whole = lambda a: pl.BlockSpec((1,) + a.shape[1:], lambda b, i: (b,) + (0,) * (a.ndim - 1))
    const = lambda a: pl.BlockSpec(a.shape, lambda b, i: (0,) * a.ndim)
    lanes4 = Q_PER_KV * QT
    return pl.pallas_call(
        functools.partial(_nsa_prompt_kernel, nsel=nsel, ncp=ncp, ncv=n_cmp),
        grid=(bsz, nq),
        in_specs=[pl.BlockSpec((1, QT, NSA_W), lambda b, i: (b, i, C_NQ // NSA_W)),
                  pl.BlockSpec((1, 1, 8, lanes4), lambda b, i: (b, i, 0, 0)),
                  whole(kc), whole(vct), const(cov), const(bct),
                  whole(ksa), whole(vst), whole(kw), whole(vwt), const(tsub), const(tdiag)],
        out_specs=pl.BlockSpec((1, QT, NSA_W), lambda b, i: (b, i, 0)),
        out_shape=jax.ShapeDtypeStruct((bsz, seq, NSA_W), BF16),
        scratch_shapes=[pltpu.VMEM((ncp, lanes4), F32), pltpu.VMEM((ncp, lanes4), BF16),
                        pltpu.VMEM((KV_HEADS, KV_W + LANES, lanes4), BF16),
                        pltpu.VMEM((KV_HEADS, 2 * KT, lanes4), F32), pltpu.VMEM((KV_HEADS, 2 * KT, lanes4), F32),
                        pltpu.VMEM((KV_HEADS, 1, lanes4), F32), pltpu.VMEM((KV_HEADS, 1, lanes4), F32),
                        pltpu.VMEM((KV_HEADS, V_ROWS, lanes4), F32), pltpu.VMEM((KV_HEADS, NSA_HD, lanes4), F32),
                        pltpu.VMEM((KV_HEADS, NSA_HD, lanes4), F32), pltpu.VMEM((NSA_W, QT), F32)],
        compiler_params=_cparams(("parallel", "arbitrary")),
        name="nsa_prompt",
    )(proj, gt, kc, vct, cov, bct, ksa, vst, kw, vwt, tsub, tdiag)


S_ROWS = Q_PER_KV * KV_HEADS * 4
NEW_PAD = 16
SAMPLE_NB = 4


def _sample_tables(rel_bias, past, t_new, n_cmp, win_len):
    row = jnp.arange(S_ROWS)
    i = (row % t_new)[:, None]
    g = (row // t_new) % KV_HEADS
    r = row // (t_new * KV_HEADS)
    head = g * Q_PER_KV + r

    def pick(tab):
        return jnp.take_along_axis(tab, head[:, None, None], axis=2)[..., 0]

    n = jnp.arange(LANES)[None, :]
    bcs = jnp.where(n < n_cmp, pick(_delta_bias(rel_bias, past + i - (CMP_STRIDE * n + CMP_BLOCK - 1))), NEG)
    jj = jnp.arange(LANES)[None, :]
    last = pick(_delta_bias(rel_bias, LANES + i - jj))
    w_old = jnp.where(jj > i, 0.0, NEG)
    j2 = jnp.arange(NEW_PAD)[None, :]
    d_new = i - j2
    new = jnp.where((j2 < t_new) & (d_new >= 0), pick(_delta_bias(rel_bias, d_new)), NEG)
    return bcs, jnp.stack([last, w_old]), new


def _nsa_sample_kernel(*refs, nb, npages, past, t_new):
    pt_ref, q_ref, g_ref, kc_ref = refs[:4]
    page_refs = refs[4:4 + nb * npages]
    win_ref, news_ref, neww_ref, bcs_ref, tab_ref, tnew_ref, cov_ref, o_ref, s_scr = refs[4 + nb * npages:]
    del pt_ref
    elems = range(nb)
    qb = [(q_ref[e] * (NSA_HD ** -0.5)).astype(BF16) for e in elems]

    def softmax_pv(tiles_of):
        mx = []
        for e in elems:
            m = jnp.full((S_ROWS, 1), NEG, F32)
            for c, (score_fn, _, add_fn, width) in enumerate(tiles_of(e)):
                s = score_fn()
                if add_fn is not None:
                    s = s + add_fn()
                s_scr[e, :, c * LANES:c * LANES + width] = s
                m = jnp.maximum(m, s.max(axis=1, keepdims=True))
            mx.append(m)
        out = []
        for e in elems:
            lsum = jnp.zeros((S_ROWS, 1), F32)
            acc = jnp.zeros((S_ROWS, KV_W), F32)
            for c, (_, pv_fn, _, width) in enumerate(tiles_of(e)):
                p = jnp.exp(s_scr[e, :, c * LANES:c * LANES + width] - mx[e])
                lsum = lsum + p.sum(axis=1, keepdims=True)
                acc = acc + pv_fn(p.astype(BF16))
            out.append(acc / lsum)
        return out

    def cached_tile(e, tile_ref, lanes, add_fn):
        return (lambda: _dot(qb[e], tile_ref[:KV_W, lanes].astype(BF16)),
                lambda p: _dot_nt(p, tile_ref[KV_W:, lanes].astype(BF16)), add_fn, LANES)

    def new_tile(e, ref, add_fn):
        return (lambda: _dot_nt(qb[e], ref[e, :, :KV_W].astype(BF16)),
                lambda p: _dot(p, ref[e, :, KV_W:].astype(BF16)), add_fn, NEW_PAD)

    rows_gi = KV_HEADS * t_new
    o_cmp, imp = [], []
    for e in elems:
        kc = kc_ref[e]
        s_c = _dot_nt(qb[e], kc[:, :KV_W].astype(BF16)) + bcs_ref[...]
        p_c = jnp.exp(s_c - s_c.max(axis=1, keepdims=True))
        p_c = (p_c / p_c.sum(axis=1, keepdims=True)).astype(BF16)
        o_cmp.append(_dot(p_c, kc[:, KV_W:].astype(BF16)))
        imp_r = _dot(p_c, cov_ref[...])
        tot = imp_r
        for r in range(1, Q_PER_KV):
            tot = tot + pltpu.roll(imp_r, r * rows_gi, axis=0)
        imp.append(tot)

    jb = lax.broadcasted_iota(jnp.int32, (S_ROWS, LANES), 1)
    qpos = past + lax.broadcasted_iota(jnp.int32, (S_ROWS, LANES), 0) % t_new
    cur = qpos // SEL_BLOCK
    forced = (jb == 0) | (jb == cur) | (jb == cur - 1)
    future = jb * SEL_BLOCK > qpos
    score = [jnp.where(forced, FORCE, jnp.where(future, -FORCE, imp[e])) for e in elems]
    chosen_any = [jnp.zeros((S_ROWS, LANES), jnp.bool_) for _ in elems]
    for _ in range(N_SELECT):
        for e in elems:
            best = jnp.max(score[e], axis=1, keepdims=True)
            first = jnp.min(jnp.where(score[e] == best, jb, LANES), axis=1, keepdims=True)
            chosen = jb == first
            chosen_any[e] = chosen_any[e] | chosen
            score[e] = jnp.where(chosen, REMOVED, score[e])
    sel = [jnp.where(chosen_any[e], 1.0, 0.0).astype(BF16) for e in elems]

    def block_mask(e, j0, nkeys=LANES):
        jrow = lax.broadcasted_iota(jnp.int32, (LANES, nkeys), 0)
        kcol = lax.broadcasted_iota(jnp.int32, (LANES, nkeys), 1)
        expand = jnp.where(jrow == j0 + kcol // SEL_BLOCK, 1.0, 0.0).astype(BF16)
        return jnp.where(_dot(sel[e], expand) > 0.5, 0.0, NEG)

    jn = past // SEL_BLOCK

    def sel_tiles(e):
        tiles = []
        for p in range(npages):
            if p == npages - 1:
                add = lambda p=p: block_mask(e, 2 * p) + tab_ref[0]
            else:
                add = lambda p=p: block_mask(e, 2 * p)
            tiles.append(cached_tile(e, page_refs[e * npages + p].at[0, 0], slice(None), add))
        tiles.append(new_tile(e, news_ref, lambda: tnew_ref[...] + block_mask(e, jn, NEW_PAD)))
        return tiles

    o_sel = softmax_pv(sel_tiles)

    nwin = win_ref.shape[3] // LANES

    def win_tiles(e):
        tiles = []
        for c in range(nwin):
            if c == 0:
                add = lambda: tab_ref[1]
            elif c == nwin - 1:
                add = lambda: tab_ref[0]
            else:
                add = None
            tiles.append(cached_tile(e, win_ref.at[0, e], slice(c * LANES, (c + 1) * LANES), add))
        tiles.append(new_tile(e, neww_ref, lambda: tnew_ref[...]))
        return tiles

    o_win = softmax_pv(win_tiles)

    for e in elems:
        gl = g_ref[e]
        o_ref[e] = (_sigmoid(gl[:, 0:1]) * o_cmp[e] + _sigmoid(gl[:, 1:2]) * o_sel[e]
                    + _sigmoid(gl[:, 2:3]) * o_win[e])


def nsa_sample(proj, ckv, pool_s, win_buf, layer, page_table, rel_bias):
    bsz, t_new, _ = proj.shape
    npages = page_table.shape[1]
    past = npages * PAGE
    n_cmp = (past + t_new - CMP_BLOCK) // CMP_STRIDE + 1
    n_sel = -(-(past + t_new) // SEL_BLOCK)
    assert n_cmp <= LANES and n_sel <= LANES and win_buf.shape[3] == WINDOW and t_new == 4
    assert (past + t_new - 1) // SEL_BLOCK == past // SEL_BLOCK
    bcs, tab, tnew = _sample_tables(rel_bias, past, t_new, n_cmp, win_buf.shape[3])
    n = np.arange(LANES)
    jsel = np.arange(LANES)
    cov = ((n[:, None] * CMP_STRIDE < jsel[None, :] * SEL_BLOCK + SEL_BLOCK)
           & (n[:, None] * CMP_STRIDE + CMP_BLOCK > jsel[None, :] * SEL_BLOCK)
           & (n[:, None] < n_cmp) & (jsel[None, :] < n_sel))
    cov = jnp.asarray(cov, BF16)

    eye = jnp.eye(KV_HEADS, dtype=F32)
    q5 = proj[:, :, C_NQ:C_NQ + NSA_W].reshape(bsz, t_new, KV_HEADS, Q_PER_KV, NSA_HD).transpose(0, 3, 2, 1, 4)
    qpad = (q5[:, :, :, :, None, :] * eye[None, None, :, None, :, None]).reshape(bsz, S_ROWS, KV_W)
    ng = proj[:, :, C_NG:C_NG + 3 * NSA_HEADS].reshape(bsz, t_new, KV_HEADS, Q_PER_KV, 3).transpose(0, 3, 2, 1, 4)
    gl = jnp.pad(ng.reshape(bsz, S_ROWS, 3), ((0, 0), (0, 0), (0, LANES - 3)))
    pad_new = lambda c: jnp.pad(proj[:, :, c:c + 2 * KV_W], ((0, 0), (0, NEW_PAD - t_new), (0, 0)))

    nb = SAMPLE_NB
    assert bsz % nb == 0
    per_b = lambda a: pl.BlockSpec((nb,) + a.shape[1:], lambda b, pt: (b,) + (0,) * (a.ndim - 1))
    const = lambda a: pl.BlockSpec(a.shape, lambda b, pt: (0,) * a.ndim)
    new_s, new_w = pad_new(C_SKV), pad_new(C_WKV)
    grid_spec = pltpu.PrefetchScalarGridSpec(
        num_scalar_prefetch=1,
        grid=(bsz // nb,),
        in_specs=[per_b(qpad), per_b(gl), per_b(ckv)]
        + [pl.BlockSpec((1, 1, 2 * KV_W, PAGE), (lambda b, pt, e=e, p=p: (layer, pt[b * nb + e, p], 0, 0)))
           for e in range(nb) for p in range(npages)]
        + [pl.BlockSpec((1, nb) + win_buf.shape[2:], lambda b, pt: (layer, b, 0, 0)),
           per_b(new_s), per_b(new_w), const(bcs), const(tab), const(tnew), const(cov)],
        out_specs=pl.BlockSpec((nb, S_ROWS, KV_W), lambda b, pt: (b, 0, 0)),
        scratch_shapes=[pltpu.VMEM((nb, S_ROWS, (npages + 1) * LANES), F32)],
    )
    out = pl.pallas_call(
        functools.partial(_nsa_sample_kernel, nb=nb, npages=npages, past=past, t_new=t_new),
        grid_spec=grid_spec,
        out_shape=jax.ShapeDtypeStruct((bsz, S_ROWS, KV_W), F32),
        compiler_params=_cparams(("parallel",)),
        name="nsa_sample",
    )(page_table, qpad, gl, ckv, *([pool_s] * (nb * npages)), win_buf, new_s, new_w, bcs, tab, tnew, cov)
    o6 = out.reshape(bsz, Q_PER_KV, KV_HEADS, t_new, KV_HEADS, NSA_HD)
    o5 = jnp.stack([o6[:, :, g, :, g, :] for g in range(KV_HEADS)], axis=2)
    return o5.transpose(0, 3, 2, 1, 4).reshape(bsz, t_new, NSA_W)


def _in_proj_perm():
    sizes = (ML_W, ML_W, ML_W, ML_W, ML_HEADS, ML_HEADS, NSA_W, KV_W, KV_W, KV_W, KV_W, KV_W, KV_W, 3 * NSA_HEADS)
    off = np.concatenate([[0], np.cumsum(sizes)])
    order = (0, 1, 2, 3, 6, 7, 8, 9, 10, 11, 12, 4, 5, 13)
    return np.concatenate([np.arange(off[k], off[k + 1]) for k in order])


def _layer(x, mods, rows_per_group, lw, tm, tm_ffn, mixer):
    sh1, sc1, g1, sh2, sc2, g2 = mods
    w_in_bf, w_out_bf, w_up_bf, w_down_bf, ln_g, ln_b = lw
    proj, kv_bf = in_proj(x, sc1, sh1, w_in_bf, tm, rows_per_group // tm)
    h_ml, o_nsa, extras = mixer(proj, kv_bf)
    x1 = out_proj_ln(x, h_ml, o_nsa, w_out_bf, g1, ln_g[0], ln_b[0], tm, rows_per_group // tm)
    x2 = ffn_ln(x1, sc2, sh2, g2, w_up_bf, w_down_bf, ln_g[1], ln_b[1], tm_ffn, rows_per_group // tm_ffn)
    return x2, proj, extras


def kernel(x_prompt, x_sample, cache_cmp_kv, cache_slc_kv, cache_win_kv, state_mlstm_C, state_mlstm_n,
           state_mlstm_m, page_table, c_prompt, c_sample, w_ada, b_ada, w_in, b_gate, ml_norm_g, cmp_pe,
           cmp_w1, cmp_w2, rel_bias, w_out, ln_g, ln_b, w_up, w_down):
    bsz, seq, d = x_prompt.shape
    bs, ts, _ = x_sample.shape
    depth = w_in.shape[0]
    n_phys = cache_cmp_kv.shape[1]
    npages = page_table.shape[1]
    win_len = cache_win_kv.shape[2]
    tm = 512
    tm_ffn = 1024
    t_ml = 256
    t_pad = 16

    rows_last = lambda a: a.transpose(0, 1, 3, 4, 5, 2).reshape(a.shape[0], a.shape[1], 2 * KV_W, a.shape[2])
    pool_c, pool_s, win_t = rows_last(cache_cmp_kv), rows_last(cache_slc_kv), rows_last(cache_win_kv)

    perm = _in_proj_perm()
    prompt_tabs = _prompt_tables(rel_bias)
    nc_rows = bsz + bs
    c_all = jnp.pad(jnp.concatenate([c_prompt, c_sample], axis=0), ((0, -nc_rows % 8), (0, 0)))

    xp = x_prompt.reshape(bsz * seq, d)
    xs = x_sample.reshape(bs * ts, d)
    outs = {k: [] for k in ("cmp_p", "cmp_s", "slc_p", "slc_s", "win_p", "win_s",
                            "C_p", "C_s", "n_p", "n_s", "m_p", "m_s")}
    kv_shape = lambda b, t: (b, t, 2, KV_HEADS, NSA_HD)

    for l in range(depth):
        ada = ada_mod(c_all, w_ada[l].astype(BF16), b_ada[l]).reshape(c_all.shape[0], 6, d)
        mods_p = [ada[:bsz, k][:, None, :] for k in range(6)]
        mods_s = [jnp.repeat(ada[bsz:nc_rows, k], ts, axis=0)[None] for k in range(6)]
        w_in_bf = jnp.pad(w_in[l][:, perm], ((0, 0), (0, N_IN_PAD - perm.size))).astype(BF16)
        lw = (w_in_bf, w_out[l].astype(BF16), w_up[l].astype(BF16), w_down[l].astype(BF16), ln_g[l], ln_b[l])
        cw = _compress_weights(cmp_pe[l], cmp_w1[l], cmp_w2[l])

        def prompt_mixer(proj, kv_bf):
            proj = proj.reshape(bsz, seq, N_IN_PAD)
            h_ml, c1, n1, m1 = mlstm(proj, b_gate[l], ml_norm_g[l],
                                     jnp.zeros((bsz, ML_HEADS, ML_HD, ML_HD), F32),
                                     jnp.zeros((bsz, ML_HEADS, ML_HD), F32), jnp.zeros((bsz, ML_HEADS), F32),
                                     nb=bsz, t=t_ml, valid=t_ml)
            ckv = compress(proj, [lambda s: (lambda b: (b, 0, C_CKV // KV_W + s))], 0, (), bsz, 1, seq, cw)
            o_nsa = nsa_prompt(proj, kv_bf.reshape(bsz, seq, 4 * KV_W), ckv, prompt_tabs)
            return h_ml.reshape(bsz * seq, ML_W), o_nsa.reshape(bsz * seq, NSA_W), (c1, n1, m1)

        def sample_mixer(proj, kv_bf):
            del kv_bf
            proj = proj.reshape(bs, ts, N_IN_PAD)
            proj_pad = jnp.pad(proj, ((0, 0), (0, t_pad - ts), (0, 0)))
            h_ml, c1, n1, m1 = mlstm(proj_pad, b_gate[l], ml_norm_g[l], state_mlstm_C, state_mlstm_n,
                                     state_mlstm_m, nb=8, t=t_pad, valid=ts, layer=l)
            pages = [(lambda b, pt, e=e, p=p: (l, pt[b * SAMPLE_NB + e, p], 0, 0))
                     for e in range(SAMPLE_NB) for p in range(npages)]
            ckv = compress(pool_c, pages, 1, (page_table,), bs // SAMPLE_NB, SAMPLE_NB * npages, PAGE, cw,
                           transposed=True, nb=SAMPLE_NB)
            o_nsa = nsa_sample(proj, ckv, pool_s, win_t, l, page_table, rel_bias)
            return (h_ml[:, :ts].reshape(bs * ts, ML_W), o_nsa.reshape(bs * ts, NSA_W).astype(BF16),
                    (c1, n1, m1))

        xp, proj_p, st_p = _layer(xp, mods_p, seq, lw, tm, tm_ffn, prompt_mixer)
        xs, proj_s, st_s = _layer(xs, mods_s, bs * ts, lw, bs * ts, bs * ts, sample_mixer)

        proj_p = proj_p.reshape(bsz, seq, N_IN_PAD)
        proj_s = proj_s.reshape(bs, ts, N_IN_PAD)
        rows = lambda proj, c: proj[:, :, c:c + 2 * KV_W]
        win = min(WINDOW, seq)
        outs["cmp_p"].append(rows(proj_p, C_CKV).reshape(kv_shape(bsz, seq)))
        outs["slc_p"].append(rows(proj_p, C_SKV).reshape(kv_shape(bsz, seq)))
        outs["win_p"].append(rows(proj_p, C_WKV)[:, seq - win:].reshape(kv_shape(bsz, win)))
        outs["cmp_s"].append(rows(proj_s, C_CKV).reshape(kv_shape(bs, ts)))
        outs["slc_s"].append(rows(proj_s, C_SKV).reshape(kv_shape(bs, ts)))
        outs["win_s"].append(rows(proj_s, C_WKV).reshape(kv_shape(bs, ts)))
        for tag, st in (("p", st_p), ("s", st_s)):
            outs["C_" + tag].append(st[0])
            outs["n_" + tag].append(st[1])
            outs["m_" + tag].append(st[2])

    stk = lambda k: jnp.stack(outs[k])
    assert ts <= win_len
    win_s = jnp.concatenate([cache_win_kv[:, :, ts:], stk("win_s")], axis=2)
    return (xp.reshape(bsz, seq, d), xs.reshape(bs, ts, d),
            stk("cmp_p"), stk("cmp_s"), stk("slc_p"), stk("slc_s"), stk("win_p"), win_s,
            stk("C_p"), stk("C_s"), stk("n_p"), stk("n_s"), stk("m_p"), stk("m_s"))
```

```python
import functools
import math

import numpy as np
import jax
import jax.numpy as jnp
from jax import lax
from jax.experimental import pallas as pl
from jax.experimental.pallas import tpu as pltpu

F32 = jnp.float32
BF16 = jnp.bfloat16

D_MODEL = 1024
DEPTH = 2
PAGE = 128
ML_HEADS = 4
ML_HD = 128
ML_W = ML_HEADS * ML_HD
NSA_HEADS = 8
NSA_HD = 64
NSA_W = NSA_HEADS * NSA_HD
KV_HEADS = 2
Q_PER_KV = 4
KV_W = KV_HEADS * NSA_HD
CMP_BLOCK = 32
CMP_STRIDE = 16
CMP_HIDDEN = 256
SEL_BLOCK = 64
N_SELECT = 16
WINDOW = 512
NUM_BUCKETS = 32
REL_MAX_DIST = 128
D_FF = 4 * D_MODEL
ALPHA = (2 * DEPTH) ** 0.25
LN_EPS = 1e-5
NEG = -1e30
FORCE = 1e9
REMOVED = -3e38
LOG2E = math.log2(math.e)

LANES = 128
QT = 128
KT = 128
V_ROWS = 80
NC_PAD_FRONT = 16
NEAR_ROWS = 24

C_MQ, C_MK, C_MV, C_MO = 0, 512, 1024, 1536
C_NQ = 2048
C_CKV, C_SKV, C_WKV = 2560, 2816, 3072
C_GATE = 3328
C_NG = 3336
N_IN_PAD = 3456
VMEM_LIMIT = 56 * 1024 * 1024


def _cparams(sem):
    return pltpu.CompilerParams(dimension_semantics=sem, vmem_limit_bytes=VMEM_LIMIT)


def _dot(a, b):
    return jnp.dot(a, b, preferred_element_type=F32)


def _dot_nt(a, b):
    return lax.dot_general(a, b, (((1,), (1,)), ((), ())), preferred_element_type=F32)


def _dot_tn(a, b):
    return lax.dot_general(a, b, (((0,), (0,)), ((), ())), preferred_element_type=F32)


def _sigmoid(x):
    return 1.0 / (1.0 + jnp.exp(-x))


def _layer_norm(y, g, b):
    mu = jnp.mean(y, axis=-1, keepdims=True)
    d = y - mu
    var = jnp.mean(d * d, axis=-1, keepdims=True)
    return d * lax.rsqrt(var + LN_EPS) * g + b


def _ada_kernel(c_ref, w_ref, b_ref, o_ref):
    c = c_ref[...]
    a = (c * _sigmoid(c)).astype(BF16)
    o_ref[...] = _dot(a, w_ref[...]) + b_ref[...]


def ada_mod(c, w_bf, b):
    m, d = c.shape
    n = w_bf.shape[1]
    tn = 1536
    return pl.pallas_call(
        _ada_kernel,
        grid=(n // tn,),
        in_specs=[pl.BlockSpec((m, d), lambda j: (0, 0)),
                  pl.BlockSpec((d, tn), lambda j: (0, j)),
                  pl.BlockSpec((1, tn), lambda j: (0, j))],
        out_specs=pl.BlockSpec((m, tn), lambda j: (0, j)),
        out_shape=jax.ShapeDtypeStruct((m, n), F32),
        compiler_params=_cparams(("arbitrary",)),
        name="ada_mod",
    )(c, w_bf, b.reshape(1, n))


def _inproj_kernel(x_ref, sc_ref, sh_ref, w_ref, o_ref, kv_ref):
    u = (x_ref[...] * (1.0 + sc_ref[0]) + sh_ref[0]).astype(BF16)
    y = _dot(u, w_ref[...])
    o_ref[...] = y
    kv_ref[...] = y[:, C_SKV:C_WKV + 2 * KV_W].astype(BF16)


def in_proj(x, sc, sh, w_bf, tm, tiles_per_group):
    m = x.shape[0]
    n = w_bf.shape[1]
    r = sc.shape[1]
    mod = pl.BlockSpec((1, r, D_MODEL), lambda i: (i // tiles_per_group, 0, 0))
    return pl.pallas_call(
        _inproj_kernel,
        grid=(m // tm,),
        in_specs=[pl.BlockSpec((tm, D_MODEL), lambda i: (i, 0)), mod, mod,
                  pl.BlockSpec((D_MODEL, n), lambda i: (0, 0))],
        out_specs=[pl.BlockSpec((tm, n), lambda i: (i, 0)), pl.BlockSpec((tm, 4 * KV_W), lambda i: (i, 0))],
        out_shape=[jax.ShapeDtypeStruct((m, n), F32), jax.ShapeDtypeStruct((m, 4 * KV_W), BF16)],
        compiler_params=_cparams(("parallel",)),
        name="in_proj",
    )(x, sc, sh, w_bf)


def _outproj_kernel(x_ref, a1_ref, a2_ref, w1_ref, w2_ref, g_ref, lg_ref, lb_ref, o_ref):
    mixed = _dot(a1_ref[...], w1_ref[...]) + _dot(a2_ref[...], w2_ref[...])
    y = ALPHA * x_ref[...] + g_ref[0] * mixed
    o_ref[...] = _layer_norm(y, lg_ref[...], lb_ref[...])


def out_proj_ln(x, a1, a2, w_bf, gate, ln_g, ln_b, tm, tiles_per_group):
    m = x.shape[0]
    k1 = a1.shape[1]
    r = gate.shape[1]
    return pl.pallas_call(
        _outproj_kernel,
        grid=(m // tm,),
        in_specs=[pl.BlockSpec((tm, D_MODEL), lambda i: (i, 0)),
                  pl.BlockSpec((tm, k1), lambda i: (i, 0)),
                  pl.BlockSpec((tm, k1), lambda i: (i, 0)),
                  pl.BlockSpec((k1, D_MODEL), lambda i: (0, 0)),
                  pl.BlockSpec((k1, D_MODEL), lambda i: (1, 0)),
                  pl.BlockSpec((1, r, D_MODEL), lambda i: (i // tiles_per_group, 0, 0)),
                  pl.BlockSpec((1, D_MODEL), lambda i: (0, 0)),
                  pl.BlockSpec((1, D_MODEL), lambda i: (0, 0))],
        out_specs=pl.BlockSpec((tm, D_MODEL), lambda i: (i, 0)),
        out_shape=jax.ShapeDtypeStruct((m, D_MODEL), F32),
        compiler_params=_cparams(("parallel",)),
        name="out_proj_ln",
    )(x, a1, a2, w_bf, w_bf, gate, ln_g.reshape(1, -1), ln_b.reshape(1, -1))


def _ffn_kernel(x_ref, sc_ref, sh_ref, g_ref, wu_ref, wd_ref, lg_ref, lb_ref, o_ref, u_scr, acc_scr):
    f = pl.program_id(1)

    @pl.when(f == 0)
    def _():
        u_scr[...] = (x_ref[...] * (1.0 + sc_ref[0]) + sh_ref[0]).astype(BF16)
        acc_scr[...] = jnp.zeros_like(acc_scr)

    h = jnp.maximum(_dot(u_scr[...], wu_ref[...]), 0.0)
    acc_scr[...] += _dot((h * h).astype(BF16), wd_ref[...])

    @pl.when(f == pl.num_programs(1) - 1)
    def _():
        y = ALPHA * x_ref[...] + g_ref[0] * acc_scr[...]
        o_ref[...] = _layer_norm(y, lg_ref[...], lb_ref[...])


def ffn_ln(x, sc, sh, gate, wu_bf, wd_bf, ln_g, ln_b, tm, tiles_per_group):
    m = x.shape[0]
    tf = 512
    r = sc.shape[1]
    mod = pl.BlockSpec((1, r, D_MODEL), lambda i, f: (i // tiles_per_group, 0, 0))
    return pl.pallas_call(
        _ffn_kernel,
        grid=(m // tm, D_FF // tf),
        in_specs=[pl.BlockSpec((tm, D_MODEL), lambda i, f: (i, 0)),
                  mod, mod, mod,
                  pl.BlockSpec((D_MODEL, tf), lambda i, f: (0, f)),
                  pl.BlockSpec((tf, D_MODEL), lambda i, f: (f, 0)),
                  pl.BlockSpec((1, D_MODEL), lambda i, f: (0, 0)),
                  pl.BlockSpec((1, D_MODEL), lambda i, f: (0, 0))],
        out_specs=pl.BlockSpec((tm, D_MODEL), lambda i, f: (i, 0)),
        out_shape=jax.ShapeDtypeStruct((m, D_MODEL), F32),
        scratch_shapes=[pltpu.VMEM((tm, D_MODEL), BF16), pltpu.VMEM((tm, D_MODEL), F32)],
        compiler_params=_cparams(("parallel", "arbitrary")),
        name="ffn_ln",
    )(x, sc, sh, gate, wu_bf, wd_bf, ln_g.reshape(1, -1), ln_b.reshape(1, -1))


def _mlstm_kernel(bg_ref, q_ref, k_ref, v_ref, o_ref, g_ref, ng_ref, c0_ref, n0_ref, m0_ref,
                  h_ref, c_out, n_out, m_out, c_scr, n_scr, m_scr, *, nb, t, valid):
    ci = pl.program_id(1)

    @pl.when(ci == 0)
    def _():
        c_scr[...] = c0_ref[...]
        n_scr[...] = n0_ref[...]
        m_scr[...] = m0_ref[...]

    row = lax.broadcasted_iota(jnp.int32, (t, t), 0)
    col = lax.broadcasted_iota(jnp.int32, (t, t), 1)
    tri = col <= row
    tri_t = row <= col
    eye = row == col
    row1 = lax.broadcasted_iota(jnp.int32, (t, 1), 0)
    scale = ML_HD ** -0.5

    def gate_scans(b, h):
        gates = g_ref[b]
        ig_col = gates[:, h:h + 1] + bg_ref[h]
        fr = gates[:, ML_HEADS + h:ML_HEADS + h + 1] + bg_ref[ML_HEADS + h]
        lf_col = jnp.minimum(fr, 0.0) - jnp.log1p(jnp.exp(-jnp.abs(fr)))
        if valid < t:
            ig_col = jnp.where(row1 < valid, ig_col, NEG)
            lf_col = jnp.where(row1 < valid, lf_col, 0.0)
        lf_row = jnp.sum(jnp.where(eye, lf_col, 0.0), axis=0, keepdims=True)
        ig_row = jnp.sum(jnp.where(eye, ig_col, 0.0), axis=0, keepdims=True)
        f_col = jnp.sum(jnp.where(tri, lf_row, 0.0), axis=1, keepdims=True)
        f_row = jnp.sum(jnp.where(tri_t, lf_col, 0.0), axis=0, keepdims=True)
        a_row = ig_row - f_row
        a_col = ig_col - f_col
        m0 = m_scr[b, h]
        cm_col = jnp.max(jnp.where(tri, a_row, NEG), axis=1, keepdims=True)
        g_col = jnp.maximum(m0, cm_col)
        g_end = jnp.maximum(m0, jnp.max(a_row, axis=1, keepdims=True))
        return dict(dmat=jnp.exp(jnp.where(tri, a_row - g_col, NEG)), decay_col=jnp.exp(m0 - g_col),
                    m_col=f_col + g_col, m_end=jnp.sum(lf_row, axis=1, keepdims=True) + g_end,
                    w_end_col=jnp.exp(a_col - g_end), carry=jnp.exp(m0 - g_end))

    streams = [(b, h) for b in range(nb) for h in range(ML_HEADS)]
    group = 2 * ML_HEADS
    for g0 in range(0, len(streams), group):
        grp = streams[g0:g0 + group]
        hs = [slice(h * ML_HD, (h + 1) * ML_HD) for _, h in grp]
        sc = [gate_scans(b, h) for b, h in grp]
        qf = [q_ref[b][:, s] for (b, _), s in zip(grp, hs)]
        qb = [x.astype(BF16) for x in qf]
        kf = [k_ref[b][:, s] * scale for (b, _), s in zip(grp, hs)]
        kb = [x.astype(BF16) for x in kf]
        vb = [v_ref[b][:, s].astype(BF16) for (b, _), s in zip(grp, hs)]
        cmat = [c_scr[b, h] for b, h in grp]
        nvec = [n_scr[b, h] for b, h in grp]
        n_st = range(len(grp))

        w = [_dot_nt(qb[i], kb[i]) * sc[i]["dmat"] for i in n_st]
        qc = [_dot(qb[i], cmat[i].astype(BF16)) for i in n_st]
        num = [_dot(w[i].astype(BF16), vb[i]) + qc[i] * sc[i]["decay_col"] for i in n_st]
        den = [jnp.sum(w[i], axis=1, keepdims=True)
               + jnp.sum(qf[i] * nvec[i], axis=1, keepdims=True) * sc[i]["decay_col"] for i in n_st]
        hh = [num[i] / jnp.maximum(jnp.abs(den[i]), jnp.exp(-sc[i]["m_col"])) for i in n_st]
        mu = [jnp.mean(x, axis=1, keepdims=True) for x in hh]
        dd = [hh[i] - mu[i] for i in n_st]
        var = [jnp.mean(x * x, axis=1, keepdims=True) for x in dd]
        for i, (b, h) in enumerate(grp):
            hn = dd[i] * lax.rsqrt(var[i] + LN_EPS) * ng_ref[:, hs[i]] * _sigmoid(o_ref[b][:, hs[i]])
            h_ref[b, :, hs[i]] = hn.astype(h_ref.dtype)

        kw = [kf[i] * sc[i]["w_end_col"] for i in n_st]
        upd = [_dot_tn(kw[i].astype(BF16), vb[i]) for i in n_st]
        for i, (b, h) in enumerate(grp):
            c_scr[b, h] = sc[i]["carry"] * cmat[i] + upd[i]
            n_scr[b, h] = sc[i]["carry"] * nvec[i] + jnp.sum(kw[i], axis=0, keepdims=True)
            m_scr[b, h] = sc[i]["m_end"]

    @pl.when(ci == pl.num_programs(1) - 1)
    def _():
        c_out[...] = c_scr[...]
        n_out[...] = n_scr[...]
        m_out[...] = m_scr[...]


def mlstm(proj, b_gate, norm_g, c0, n0, m0, *, nb, t, valid, layer=None):
    bsz, length, _ = proj.shape
    nchunk = length // t
    wide = lambda cb: pl.BlockSpec((nb, t, ML_W), lambda i, c: (i, c, cb))
    state4 = lambda s: pl.BlockSpec((nb,) + s, lambda i, c: (i, 0, 0, 0))
    if layer is None:
        state_in = state4
        lead = (bsz,)
    else:
        state_in = lambda s: pl.BlockSpec((None, nb) + s, lambda i, c: (layer, i, 0, 0, 0))
        lead = (c0.shape[0], bsz)
    kern = functools.partial(_mlstm_kernel, nb=nb, t=t, valid=valid)
    h, c1, n1, m1 = pl.pallas_call(
        kern,
        grid=(bsz // nb, nchunk),
        in_specs=[pl.BlockSpec(memory_space=pltpu.SMEM),
                  wide(C_MQ // ML_W), wide(C_MK // ML_W), wide(C_MV // ML_W), wide(C_MO // ML_W),
                  pl.BlockSpec((nb, t, LANES), lambda i, c: (i, c, C_GATE // LANES)),
                  pl.BlockSpec((1, ML_W), lambda i, c: (0, 0)),
                  state_in((ML_HEADS, ML_HD, ML_HD)), state_in((ML_HEADS, 1, ML_HD)), state_in((ML_HEADS, 1, 1))],
        out_specs=[pl.BlockSpec((nb, t, ML_W), lambda i, c: (i, c, 0)),
                   state4((ML_HEADS, ML_HD, ML_HD)), state4((ML_HEADS, 1, ML_HD)), state4((ML_HEADS, 1, 1))],
        out_shape=[jax.ShapeDtypeStruct((bsz, length, ML_W), BF16),
                   jax.ShapeDtypeStruct((bsz, ML_HEADS, ML_HD, ML_HD), F32),
                   jax.ShapeDtypeStruct((bsz, ML_HEADS, 1, ML_HD), F32),
                   jax.ShapeDtypeStruct((bsz, ML_HEADS, 1, 1), F32)],
        scratch_shapes=[pltpu.VMEM((nb, ML_HEADS, ML_HD, ML_HD), F32),
                        pltpu.VMEM((nb, ML_HEADS, 1, ML_HD), F32),
                        pltpu.VMEM((nb, ML_HEADS, 1, 1), F32)],
        compiler_params=_cparams(("parallel", "arbitrary")),
        name="mlstm",
    )(b_gate, proj, proj, proj, proj, proj, norm_g.reshape(1, ML_W),
      c0, n0.reshape(lead + (ML_HEADS, 1, ML_HD)), m0.reshape(lead + (ML_HEADS, 1, 1)))
    return h, c1, n1.reshape(bsz, ML_HEADS, ML_HD), m1.reshape(bsz, ML_HEADS)


def _gelu_tanh(x):
    return 0.5 * x * (1.0 + jnp.tanh(math.sqrt(2.0 / math.pi) * (x + 0.044715 * (x * x * x))))


def _compress_kernel(*refs, npages, rows, transposed, nb):
    jp = rows // CMP_STRIDE
    j = npages * jp
    lane = lax.broadcasted_iota(jnp.int32, (j, LANES), 1)
    low = lane < NSA_HD
    if transposed:
        pe_ref, w1_ref, w2_ref, o_ref, xk_scr, xv_scr = refs[-6:]
        page_refs = refs[-6 - npages:-6]
        for p in range(npages):
            x = page_refs[p][0, 0].T
            xk_scr[p * rows:(p + 1) * rows, :] = x[:, :KV_W]
            xv_scr[p * rows:(p + 1) * rows, :] = x[:, KV_W:]

        def chunk_phase(c, s):
            return (xk_scr, xv_scr)[s][pl.ds(c, j, stride=CMP_STRIDE), :]
    else:
        pe_ref, w1_ref, w2_ref, o_ref = refs[-4:]
        page_refs = refs[-4 - 2 * npages:-4]

        def chunk_phase(c, s):
            parts = [page_refs[2 * p + s][0, pl.ds(c, jp, stride=CMP_STRIDE), :] for p in range(npages)]
            return parts[0] if npages == 1 else jnp.concatenate(parts, axis=0)

    out = jnp.zeros((j, 2 * KV_HEADS * NSA_HD), F32)
    for s in range(2):
        halves = [[], []]
        for c in range(0, CMP_STRIDE, 2):
            va = chunk_phase(c, s)
            vb = chunk_phase(c + 1, s)
            halves[0].append(jnp.where(low, va, pltpu.roll(vb, NSA_HD, axis=1)).astype(BF16))
            halves[1].append(jnp.where(low, pltpu.roll(va, NSA_HD, axis=1), vb).astype(BF16))
        pe_c = _dot(pe_ref[s], w1_ref[s])
        pe_const = pe_c[0:1, :CMP_HIDDEN] + pe_c[1:2, CMP_HIDDEN:]
        for g in range(KV_HEADS):
            lhs = jnp.concatenate(halves[g], axis=1)
            acc = _dot(lhs, w1_ref[s])
            hid = acc[:, :CMP_HIDDEN] + pltpu.roll(acc[:, CMP_HIDDEN:], j - 1, axis=0) + pe_const
            out = out + _dot(_gelu_tanh(hid).astype(BF16), w2_ref[s * KV_HEADS + g])
    je = j // nb
    for e in range(nb):
        o_ref[e] = out[e * je:(e + 1) * je, :]


def _compress_weights(pe, w1, w2):
    w1r = w1.reshape(2, CMP_BLOCK, NSA_HD, CMP_HIDDEN)
    wa = w1r[:, :CMP_STRIDE].reshape(2, 4, 4 * NSA_HD, CMP_HIDDEN)
    wb = w1r[:, CMP_STRIDE:].reshape(2, 4, 4 * NSA_HD, CMP_HIDDEN)
    w1p = jnp.concatenate([wa, wb], axis=-1).astype(BF16).reshape(2, CMP_STRIDE * NSA_HD, 2 * CMP_HIDDEN)
    pea = pe[:, :CMP_STRIDE].reshape(2, 1, CMP_STRIDE * NSA_HD)
    peb = pe[:, CMP_STRIDE:].reshape(2, 1, CMP_STRIDE * NSA_HD)
    pep = jnp.concatenate([pea, peb, jnp.zeros((2, 14, CMP_STRIDE * NSA_HD), F32)], axis=1).astype(BF16)
    w2p = jnp.zeros((2, KV_HEADS, CMP_HIDDEN, 2, KV_HEADS, NSA_HD), F32)
    for s in range(2):
        for g in range(KV_HEADS):
            w2p = w2p.at[s, g, :, s, g, :].set(w2[s])
    w2p = w2p.reshape(2 * KV_HEADS, CMP_HIDDEN, 2 * KV_W).astype(BF16)
    return pep, w1p, w2p


def compress(pages, page_index_maps, num_scalar_prefetch, prefetch, grid_n, npages, rows, cw, transposed=False,
             nb=1):
    pep, w1p, w2p = cw
    j = npages * rows // CMP_STRIDE
    const = lambda nd: (lambda *a: (0,) * nd)
    if transposed:
        page_specs = [pl.BlockSpec((1, 1, 2 * KV_W, rows), im) for im in page_index_maps]
        scratch = [pltpu.VMEM((npages * rows, KV_W), F32)] * 2
    else:
        page_specs = [pl.BlockSpec((1, rows, KV_W), im(s)) for im in page_index_maps for s in range(2)]
        scratch = []
    grid_spec = pltpu.PrefetchScalarGridSpec(
        num_scalar_prefetch=num_scalar_prefetch,
        grid=(grid_n,),
        in_specs=page_specs
        + [pl.BlockSpec(pep.shape, const(3)), pl.BlockSpec(w1p.shape, const(3)), pl.BlockSpec(w2p.shape, const(3))],
        out_specs=pl.BlockSpec((nb, j // nb, 2 * KV_W), lambda b, *a: (b, 0, 0)),
        scratch_shapes=scratch,
    )
    return pl.pallas_call(
        functools.partial(_compress_kernel, npages=npages, rows=rows, transposed=transposed, nb=nb),
        grid_spec=grid_spec,
        out_shape=jax.ShapeDtypeStruct((grid_n * nb, j // nb, 2 * KV_W), F32),
        compiler_params=_cparams(("parallel",)),
        name="compress",
    )(*prefetch, *([pages] * len(page_specs)), pep, w1p, w2p)


def _rel_bucket(dist):
    n = jnp.maximum(dist, 0)
    max_exact = NUM_BUCKETS // 2
    nf = jnp.maximum(n, 1).astype(F32)
    large = max_exact + (jnp.log(nf / max_exact) / math.log(REL_MAX_DIST / max_exact)
                         * (NUM_BUCKETS - max_exact)).astype(jnp.int32)
    large = jnp.minimum(large, NUM_BUCKETS - 1)
    return jnp.where(n < max_exact, n, large)


def _delta_bias(rel_bias, dist):
    far = rel_bias[NUM_BUCKETS - 1]
    val = rel_bias[_rel_bucket(dist)] - far
    return jnp.where(((dist >= 0) & (dist < REL_MAX_DIST))[..., None], val, 0.0)


def _lanes_ri(tab, g):
    rows = tab.shape[0]
    return tab[:, :, g * Q_PER_KV:(g + 1) * Q_PER_KV].transpose(0, 2, 1).reshape(rows, Q_PER_KV * QT)


def _prompt_tables(rel_bias):
    i = jnp.arange(QT)[None, :]
    j = jnp.arange(KT)[:, None]
    d_diag = i - j
    diag = jnp.where((d_diag >= 0)[..., None], _delta_bias(rel_bias, d_diag), NEG)
    sub = _delta_bias(rel_bias, i + KT - j)
    n2 = jnp.arange(NEAR_ROWS)[:, None]
    near = _delta_bias(rel_bias, i + (CMP_STRIDE * NC_PAD_FRONT - CMP_BLOCK + 1) - CMP_STRIDE * n2)
    stack = lambda tab: jnp.stack([_lanes_ri(tab, g) for g in range(KV_HEADS)]) * LOG2E
    return stack(sub), stack(diag), stack(near)


def _cover_t(n_sel, n_cmp, ncp):
    n = np.arange(ncp) - NC_PAD_FRONT
    c_start = n * CMP_STRIDE
    s_start = np.arange(n_sel)[:, None] * SEL_BLOCK
    cov = (c_start[None, :] < s_start + SEL_BLOCK) & (c_start[None, :] + CMP_BLOCK > s_start)
    cov &= ((n >= 0) & (n < n_cmp))[None, :]
    return jnp.asarray(cov, BF16)


def _nsa_prompt_kernel(q_ref, gt_ref, kc_ref, vct_ref, cov_ref, bct_ref, ksa_ref, vst_ref, kw_ref, vwt_ref,
                       tsub_ref, tdiag_ref, o_ref, s_scr, p_scr, q_scr, sa_scr, sb_scr, m_scr, l_scr, acc_scr,
                       ocmp_scr, osel_scr, ot_scr, *, nsel, ncp, ncv):
    blk = pl.program_id(1)
    t0 = blk * QT
    lanes4 = Q_PER_KV * QT
    q_t = (q_ref[0] * (NSA_HD ** -0.5 * LOG2E)).T

    def tile4(x):
        return jnp.concatenate([x] * Q_PER_KV, axis=1)

    def weighted_values(scores, m, values):
        probs = [jnp.exp2(s - m).astype(BF16) for s in scores]
        p_all = probs[0] if len(probs) == 1 else jnp.concatenate(probs, axis=0)
        v_all = values[0] if len(values) == 1 else jnp.concatenate(values, axis=1)
        return _dot(v_all, p_all)

    def tiles_max(scores, m):
        for s in scores:
            m = jnp.maximum(m, jnp.max(s, axis=0, keepdims=True))
        return m

    def normalise(acc, m):
        inv = jnp.where(m > 0.5 * NEG, 1.0 / acc[NSA_HD:NSA_HD + 1, :], 0.0)
        return acc[0:NSA_HD, :] * inv

    def update(g, scores, values):
        m_old = m_scr[g]
        m_new = tiles_max(scores, m_old)
        acc_scr[g] = jnp.exp2(m_old - m_new) * acc_scr[g] + weighted_values(scores, m_new, values)
        m_scr[g] = m_new

    row_k = lax.broadcasted_iota(jnp.int32, (KT, QT), 0)
    lane_q = lax.broadcasted_iota(jnp.int32, (KT, QT), 1)

    importance = []
    for g in range(KV_HEADS):
        gs = slice(g * NSA_HD, (g + 1) * NSA_HD)
        qg = jnp.concatenate([q_t[(g * Q_PER_KV + r) * NSA_HD:(g * Q_PER_KV + r + 1) * NSA_HD, :]
                              for r in range(Q_PER_KV)], axis=1)
        zero = jnp.zeros_like(qg)
        qpad = jnp.concatenate([qg, zero] if g == 0 else [zero, qg], axis=0).astype(BF16)
        q_scr[g, 0:KV_W, :] = qpad

        s_scr[...] = _dot(kc_ref[0], qpad)
        near0 = pl.multiple_of(blk * (QT // CMP_STRIDE), 8)
        s_scr[pl.ds(near0, NEAR_ROWS), :] = s_scr[pl.ds(near0, NEAR_ROWS), :] + bct_ref[g]
        n_all = blk * (QT // CMP_STRIDE) - (CMP_BLOCK // CMP_STRIDE)
        n_any = n_all + (QT - 1) // CMP_STRIDE + 1
        m_scr[g] = jnp.full((1, lanes4), NEG, F32)
        l_scr[g] = jnp.zeros((1, lanes4), F32)
        nchunk = ncp // KT
        chunk_kind = []
        for c in range(nchunk):
            cs = slice(c * KT, (c + 1) * KT)
            lo, hi = c * KT - NC_PAD_FRONT, (c + 1) * KT - NC_PAD_FRONT - 1
            skip = (lo > n_any) if lo < ncv else True
            full = (hi <= n_all) if (lo >= 0 and hi < ncv) else False
            chunk_kind.append((cs, skip, full))
            if skip is True:
                continue

            @pl.when(full)
            def _(cs=cs):
                m_scr[g] = jnp.maximum(m_scr[g], jnp.max(s_scr[cs, :], axis=0, keepdims=True))

            @pl.when(jnp.logical_not(full) & jnp.logical_not(skip))
            def _(cs=cs, lo=lo):
                n = row_k + lo
                ok = (n >= 0) & (n < ncv) & (CMP_STRIDE * n + (CMP_BLOCK - 1) <= t0 + lane_q)
                sc = s_scr[cs, :] + tile4(jnp.where(ok, 0.0, NEG))
                s_scr[cs, :] = sc
                m_scr[g] = jnp.maximum(m_scr[g], jnp.max(sc, axis=0, keepdims=True))

        mx = m_scr[g]
        for cs, skip, _ in chunk_kind:
            if skip is True:
                p_scr[cs, :] = jnp.zeros((KT, lanes4), BF16)
                continue

            @pl.when(jnp.logical_not(skip))
            def _(cs=cs):
                p = jnp.exp2(s_scr[cs, :] - mx)
                l_scr[g] = l_scr[g] + jnp.sum(p, axis=0, keepdims=True)
                p_scr[cs, :] = p.astype(BF16)

            @pl.when(skip)
            def _(cs=cs):
                p_scr[cs, :] = jnp.zeros((KT, lanes4), BF16)

        inv_c = jnp.where(mx > 0.5 * NEG, 1.0 / l_scr[g], 0.0)
        ocmp_scr[g] = _dot(vct_ref[0, gs, :], p_scr[...]) * inv_c
        imp4 = _dot(cov_ref[...], p_scr[...]) * inv_c
        imp = imp4[:, 0:QT]
        for r in range(1, Q_PER_KV):
            imp = imp + imp4[:, r * QT:(r + 1) * QT]
        importance.append(imp)

    def select(rows):
        jb = lax.broadcasted_iota(jnp.int32, (rows, QT), 0)
        tq = t0 + lax.broadcasted_iota(jnp.int32, (rows, QT), 1)
        cur = tq // SEL_BLOCK
        forced = (jb == 0) | (jb == cur) | (jb == cur - 1)
        future = jb * SEL_BLOCK > tq
        score = [jnp.where(forced, FORCE, jnp.where(future, -FORCE, imp[0:rows, :])) for imp in importance]
        chosen_any = [jnp.zeros((rows, QT), jnp.bool_) for _ in range(KV_HEADS)]
        for _ in range(min(N_SELECT, rows)):
            for g in range(KV_HEADS):
                best = jnp.max(score[g], axis=0, keepdims=True)
                first = jnp.min(jnp.where(score[g] == best, jb, rows), axis=0, keepdims=True)
                chosen = jb == first
                chosen_any[g] = chosen_any[g] | chosen
                score[g] = jnp.where(chosen, REMOVED, score[g])
        for g in range(KV_HEADS):
            q_scr[g, KV_W:KV_W + rows, :] = tile4(jnp.where(chosen_any[g], 0.0, NEG)).astype(BF16)
            if rows < LANES:
                q_scr[g, KV_W + rows:, :] = jnp.full((LANES - rows, lanes4), NEG, BF16)

    last_block = (t0 + QT - 1) // SEL_BLOCK
    sizes = [r for r in (32, 64) if r < nsel] + [nsel]
    lower = 0
    for r in sizes:
        @pl.when((last_block >= lower) & ((last_block < r) | (r == nsel)))
        def _(r=r):
            select(r)
        lower = r

    def values_of(ref, kt, g):
        return ref[0, kt, g * V_ROWS:(g + 1) * V_ROWS, :]

    def qk_pair(k, dst):
        for g in range(KV_HEADS):
            dst[g, 0:KT, :] = _dot(ksa_ref[0, 2 * k], q_scr[g])
            dst[g, KT:2 * KT, :] = _dot(ksa_ref[0, 2 * k + 1], q_scr[g])

    def softmax_pair(k, src):
        for g in range(KV_HEADS):
            update(g, [src[g, 0:KT, :], src[g, KT:2 * KT, :]],
                   [values_of(vst_ref, 2 * k, g), values_of(vst_ref, 2 * k + 1, g)])

    m_scr[...] = jnp.full(m_scr.shape, NEG, F32)
    acc_scr[...] = jnp.zeros(acc_scr.shape, F32)
    n_far = jnp.maximum(blk - 1, 0)
    npairs = n_far // 2

    @pl.when(npairs > 0)
    def _():
        qk_pair(0, sa_scr)

    def two_pairs(k0):
        qk_pair(jnp.minimum(k0 + 1, npairs - 1), sb_scr)
        softmax_pair(k0, sa_scr)
        qk_pair(jnp.minimum(k0 + 2, npairs - 1), sa_scr)
        softmax_pair(k0 + 1, sb_scr)

    def far_body4(j, carry):
        two_pairs(4 * j)
        two_pairs(4 * j + 2)
        return carry

    def far_body2(j, carry):
        two_pairs(4 * (npairs // 4) + 2 * j)
        return carry

    lax.fori_loop(0, npairs // 4, far_body4, 0)
    lax.fori_loop(0, (npairs % 4) // 2, far_body2, 0)

    @pl.when(npairs % 2 == 1)
    def _():
        softmax_pair(npairs - 1, sa_scr)

    def gate(valid):
        return jnp.where(valid, 0.0, NEG).astype(F32)

    anti = tile4(jnp.where(row_k > lane_q, 0.0, NEG))
    sel_tiles = ((blk - 2, lambda g: gate(n_far % 2 == 1)),
                 (blk - 1, lambda g: tsub_ref[g] + gate(blk >= 1)),
                 (blk, lambda g: tdiag_ref[g]))
    win_tiles = ((blk - 4, lambda g: anti + gate(blk >= 4)),
                 (blk - 3, lambda g: gate(blk >= 3)),
                 (blk - 2, lambda g: gate(blk >= 2)),
                 (blk - 1, lambda g: tsub_ref[g] + gate(blk >= 1)),
                 (blk, lambda g: tdiag_ref[g]))
    sel_scores, win_scores = [], []
    for g in range(KV_HEADS):
        sel_scores.append([_dot(ksa_ref[0, jnp.maximum(kt, 0)], q_scr[g]) + add(g) for kt, add in sel_tiles])
        win_scores.append([_dot(kw_ref[0, jnp.maximum(kt, 0)], q_scr[g, 0:KV_W, :]) + add(g)
                           for kt, add in win_tiles])
    o_win = []
    for g in range(KV_HEADS):
        update(g, sel_scores[g], [values_of(vst_ref, jnp.maximum(kt, 0), g) for kt, _ in sel_tiles])
        osel_scr[g] = normalise(acc_scr[g], m_scr[g])
        m_win = tiles_max(win_scores[g], jnp.full((1, lanes4), NEG, F32))
        acc_win = weighted_values(win_scores[g], m_win,
                                  [values_of(vwt_ref, jnp.maximum(kt, 0), g) for kt, _ in win_tiles])
        o_win.append(normalise(acc_win, m_win))

    gl = gt_ref[0, 0]
    for g in range(KV_HEADS):
        o_g = (_sigmoid(gl[3 * g:3 * g + 1, :]) * ocmp_scr[g] + _sigmoid(gl[3 * g + 1:3 * g + 2, :]) * osel_scr[g]
               + _sigmoid(gl[3 * g + 2:3 * g + 3, :]) * o_win[g])
        for r in range(Q_PER_KV):
            h = g * Q_PER_KV + r
            ot_scr[h * NSA_HD:(h + 1) * NSA_HD, :] = o_g[:, r * QT:(r + 1) * QT]

    o_ref[0] = ot_scr[...].T.astype(o_ref.dtype)


def nsa_prompt(proj, kv_bf, ckv, rel_tabs):
    bsz, seq, _ = proj.shape
    nq = seq // QT
    n_cmp = (seq - CMP_BLOCK) // CMP_STRIDE + 1
    nsel = seq // SEL_BLOCK
    ncp = -(-(NC_PAD_FRONT + seq // CMP_STRIDE) // KT) * KT
    tsub, tdiag, bct = rel_tabs

    def tiles(cols):
        c0 = cols - C_SKV
        return kv_bf[:, :, c0:c0 + KV_W].reshape(bsz, nq, KT, KV_W)

    assert nsel <= LANES
    blk_of_key = np.arange(seq) // SEL_BLOCK
    onehot = jnp.asarray(blk_of_key[:, None] == np.arange(LANES)[None, :], BF16).reshape(nq, KT, LANES)
    ksa = jnp.concatenate([tiles(C_SKV), jnp.broadcast_to(onehot, (bsz, nq, KT, LANES))], axis=-1)

    def values_t(cols):
        vt = tiles(cols).reshape(bsz, nq, KT, KV_HEADS, NSA_HD).transpose(0, 1, 3, 4, 2)
        ones = jnp.ones((bsz, nq, KV_HEADS, 1, KT), BF16)
        zeros = jnp.zeros((bsz, nq, KV_HEADS, V_ROWS - NSA_HD - 1, KT), BF16)
        return jnp.concatenate([vt, ones, zeros], axis=3).reshape(bsz, nq, KV_HEADS * V_ROWS, KT)

    vst = values_t(C_SKV + KV_W)
    kw = tiles(C_WKV)
    vwt = values_t(C_WKV + KV_W)
    back = ncp - NC_PAD_FRONT - ckv.shape[1]
    kc = jnp.pad(ckv[:, :, :KV_W], ((0, 0), (NC_PAD_FRONT, back), (0, 0))).astype(BF16)
    vct = jnp.pad(ckv[:, :, KV_W:], ((0, 0), (NC_PAD_FRONT, back), (0, 0))).astype(BF16).transpose(0, 2, 1)
    cov = _cover_t(nsel, n_cmp, ncp)
    ng = proj[:, :, C_NG:C_NG + 3 * NSA_HEADS].reshape(bsz, nq, QT, KV_HEADS, Q_PER_KV, 3)
    gt = ng.transpose(0, 1, 3, 5, 4, 2).reshape(bsz, nq, KV_HEADS * 3, Q_PER_KV * QT)
    gt = jnp.pad(gt, ((0, 0), (0, 0), (0, 2), (0, 0)))

    whole = lambda a: pl.BlockSpec((1,) + a.shape[1:], lambda b, i: (b,) + (0,) * (a.ndim - 1))
    const = lambda a: pl.BlockSpec(a.shape, lambda b, i: (0,) * a.ndim)
    lanes4 = Q_PER_KV * QT
    return pl.pallas_call(
        functools.partial(_nsa_prompt_kernel, nsel=nsel, ncp=ncp, ncv=n_cmp),
        grid=(bsz, nq),
        in_specs=[pl.BlockSpec((1, QT, NSA_W), lambda b, i: (b, i, C_NQ // NSA_W)),
                  pl.BlockSpec((1, 1, 8, lanes4), lambda b, i: (b, i, 0, 0)),
                  whole(kc), whole(vct), const(cov), const(bct),
                  whole(ksa), whole(vst), whole(kw), whole(vwt), const(tsub), const(tdiag)],
        out_specs=pl.BlockSpec((1, QT, NSA_W), lambda b, i: (b, i, 0)),
        out_shape=jax.ShapeDtypeStruct((bsz, seq, NSA_W), BF16),
        scratch_shapes=[pltpu.VMEM((ncp, lanes4), F32), pltpu.VMEM((ncp, lanes4), BF16),
                        pltpu.VMEM((KV_HEADS, KV_W + LANES, lanes4), BF16),
                        pltpu.VMEM((KV_HEADS, 2 * KT, lanes4), F32), pltpu.VMEM((KV_HEADS, 2 * KT, lanes4), F32),
                        pltpu.VMEM((KV_HEADS, 1, lanes4), F32), pltpu.VMEM((KV_HEADS, 1, lanes4), F32),
                        pltpu.VMEM((KV_HEADS, V_ROWS, lanes4), F32), pltpu.VMEM((KV_HEADS, NSA_HD, lanes4), F32),
                        pltpu.VMEM((KV_HEADS, NSA_HD, lanes4), F32), pltpu.VMEM((NSA_W, QT), F32)],
        compiler_params=_cparams(("parallel", "arbitrary")),
        name="nsa_prompt",
    )(proj, gt, kc, vct, cov, bct, ksa, vst, kw, vwt, tsub, tdiag)


S_ROWS = Q_PER_KV * KV_HEADS * 4
NEW_PAD = 16
SAMPLE_NB = 4


def _sample_tables(rel_bias, past, t_new, n_cmp, win_len):
    row = jnp.arange(S_ROWS)
    i = (row % t_new)[:, None]
    g = (row // t_new) % KV_HEADS
    r = row // (t_new * KV_HEADS)
    head = g * Q_PER_KV + r

    def pick(tab):
        return jnp.take_along_axis(tab, head[:, None, None], axis=2)[..., 0]

    n = jnp.arange(LANES)[None, :]
    bcs = jnp.where(n < n_cmp, pick(_delta_bias(rel_bias, past + i - (CMP_STRIDE * n + CMP_BLOCK - 1))), NEG)
    jj = jnp.arange(LANES)[None, :]
    last = pick(_delta_bias(rel_bias, LANES + i - jj))
    w_old = jnp.where(jj > i, 0.0, NEG)
    j2 = jnp.arange(NEW_PAD)[None, :]
    d_new = i - j2
    new = jnp.where((j2 < t_new) & (d_new >= 0), pick(_delta_bias(rel_bias, d_new)), NEG)
    return bcs, jnp.stack([last, w_old]), new


def _nsa_sample_kernel(*refs, nb, npages, past, t_new):
    pt_ref, q_ref, g_ref, kc_ref = refs[:4]
    page_refs = refs[4:4 + nb * npages]
    win_ref, news_ref, neww_ref, bcs_ref, tab_ref, tnew_ref, cov_ref, o_ref, s_scr = refs[4 + nb * npages:]
    del pt_ref
    elems = range(nb)
    qb = [(q_ref[e] * (NSA_HD ** -0.5)).astype(BF16) for e in elems]

    def softmax_pv(tiles_of):
        mx = []
        for e in elems:
            m = jnp.full((S_ROWS, 1), NEG, F32)
            for c, (score_fn, _, add_fn, width) in enumerate(tiles_of(e)):
                s = score_fn()
                if add_fn is not None:
                    s = s + add_fn()
                s_scr[e, :, c * LANES:c * LANES + width] = s
                m = jnp.maximum(m, s.max(axis=1, keepdims=True))
            mx.append(m)
        out = []
        for e in elems:
            lsum = jnp.zeros((S_ROWS, 1), F32)
            acc = jnp.zeros((S_ROWS, KV_W), F32)
            for c, (_, pv_fn, _, width) in enumerate(tiles_of(e)):
                p = jnp.exp(s_scr[e, :, c * LANES:c * LANES + width] - mx[e])
                lsum = lsum + p.sum(axis=1, keepdims=True)
                acc = acc + pv_fn(p.astype(BF16))
            out.append(acc / lsum)
        return out

    def cached_tile(e, tile_ref, lanes, add_fn):
        return (lambda: _dot(qb[e], tile_ref[:KV_W, lanes].astype(BF16)),
                lambda p: _dot_nt(p, tile_ref[KV_W:, lanes].astype(BF16)), add_fn, LANES)

    def new_tile(e, ref, add_fn):
        return (lambda: _dot_nt(qb[e], ref[e, :, :KV_W].astype(BF16)),
                lambda p: _dot(p, ref[e, :, KV_W:].astype(BF16)), add_fn, NEW_PAD)

    rows_gi = KV_HEADS * t_new
    o_cmp, imp = [], []
    for e in elems:
        kc = kc_ref[e]
        s_c = _dot_nt(qb[e], kc[:, :KV_W].astype(BF16)) + bcs_ref[...]
        p_c = jnp.exp(s_c - s_c.max(axis=1, keepdims=True))
        p_c = (p_c / p_c.sum(axis=1, keepdims=True)).astype(BF16)
        o_cmp.append(_dot(p_c, kc[:, KV_W:].astype(BF16)))
        imp_r = _dot(p_c, cov_ref[...])
        tot = imp_r
        for r in range(1, Q_PER_KV):
            tot = tot + pltpu.roll(imp_r, r * rows_gi, axis=0)
        imp.append(tot)

    jb = lax.broadcasted_iota(jnp.int32, (S_ROWS, LANES), 1)
    qpos = past + lax.broadcasted_iota(jnp.int32, (S_ROWS, LANES), 0) % t_new
    cur = qpos // SEL_BLOCK
    forced = (jb == 0) | (jb == cur) | (jb == cur - 1)
    future = jb * SEL_BLOCK > qpos
    score = [jnp.where(forced, FORCE, jnp.where(future, -FORCE, imp[e])) for e in elems]
    chosen_any = [jnp.zeros((S_ROWS, LANES), jnp.bool_) for _ in elems]
    for _ in range(N_SELECT):
        for e in elems:
            best = jnp.max(score[e], axis=1, keepdims=True)
            first = jnp.min(jnp.where(score[e] == best, jb, LANES), axis=1, keepdims=True)
            chosen = jb == first
            chosen_any[e] = chosen_any[e] | chosen
            score[e] = jnp.where(chosen, REMOVED, score[e])
    sel = [jnp.where(chosen_any[e], 1.0, 0.0).astype(BF16) for e in elems]

    def block_mask(e, j0, nkeys=LANES):
        jrow = lax.broadcasted_iota(jnp.int32, (LANES, nkeys), 0)
        kcol = lax.broadcasted_iota(jnp.int32, (LANES, nkeys), 1)
        expand = jnp.where(jrow == j0 + kcol // SEL_BLOCK, 1.0, 0.0).astype(BF16)
        return jnp.where(_dot(sel[e], expand) > 0.5, 0.0, NEG)

    jn = past // SEL_BLOCK

    def sel_tiles(e):
        tiles = []
        for p in range(npages):
            if p == npages - 1:
                add = lambda p=p: block_mask(e, 2 * p) + tab_ref[0]
            else:
                add = lambda p=p: block_mask(e, 2 * p)
            tiles.append(cached_tile(e, page_refs[e * npages + p].at[0, 0], slice(None), add))
        tiles.append(new_tile(e, news_ref, lambda: tnew_ref[...] + block_mask(e, jn, NEW_PAD)))
        return tiles

    o_sel = softmax_pv(sel_tiles)

    nwin = win_ref.shape[3] // LANES

    def win_tiles(e):
        tiles = []
        for c in range(nwin):
            if c == 0:
                add = lambda: tab_ref[1]
            elif c == nwin - 1:
                add = lambda: tab_ref[0]
            else:
                add = None
            tiles.append(cached_tile(e, win_ref.at[0, e], slice(c * LANES, (c + 1) * LANES), add))
        tiles.append(new_tile(e, neww_ref, lambda: tnew_ref[...]))
        return tiles

    o_win = softmax_pv(win_tiles)

    for e in elems:
        gl = g_ref[e]
        o_ref[e] = (_sigmoid(gl[:, 0:1]) * o_cmp[e] + _sigmoid(gl[:, 1:2]) * o_sel[e]
                    + _sigmoid(gl[:, 2:3]) * o_win[e])


def nsa_sample(proj, ckv, pool_s, win_buf, layer, page_table, rel_bias):
    bsz, t_new, _ = proj.shape
    npages = page_table.shape[1]
    past = npages * PAGE
    n_cmp = (past + t_new - CMP_BLOCK) // CMP_STRIDE + 1
    n_sel = -(-(past + t_new) // SEL_BLOCK)
    assert n_cmp <= LANES and n_sel <= LANES and win_buf.shape[3] == WINDOW and t_new == 4
    assert (past + t_new - 1) // SEL_BLOCK == past // SEL_BLOCK
    bcs, tab, tnew = _sample_tables(rel_bias, past, t_new, n_cmp, win_buf.shape[3])
    n = np.arange(LANES)
    jsel = np.arange(LANES)
    cov = ((n[:, None] * CMP_STRIDE < jsel[None, :] * SEL_BLOCK + SEL_BLOCK)
           & (n[:, None] * CMP_STRIDE + CMP_BLOCK > jsel[None, :] * SEL_BLOCK)
           & (n[:, None] < n_cmp) & (jsel[None, :] < n_sel))
    cov = jnp.asarray(cov, BF16)

    eye = jnp.eye(KV_HEADS, dtype=F32)
    q5 = proj[:, :, C_NQ:C_NQ + NSA_W].reshape(bsz, t_new, KV_HEADS, Q_PER_KV, NSA_HD).transpose(0, 3, 2, 1, 4)
    qpad = (q5[:, :, :, :, None, :] * eye[None, None, :, None, :, None]).reshape(bsz, S_ROWS, KV_W)
    ng = proj[:, :, C_NG:C_NG + 3 * NSA_HEADS].reshape(bsz, t_new, KV_HEADS, Q_PER_KV, 3).transpose(0, 3, 2, 1, 4)
    gl = jnp.pad(ng.reshape(bsz, S_ROWS, 3), ((0, 0), (0, 0), (0, LANES - 3)))
    pad_new = lambda c: jnp.pad(proj[:, :, c:c + 2 * KV_W], ((0, 0), (0, NEW_PAD - t_new), (0, 0)))

    nb = SAMPLE_NB
    assert bsz % nb == 0
    per_b = lambda a: pl.BlockSpec((nb,) + a.shape[1:], lambda b, pt: (b,) + (0,) * (a.ndim - 1))
    const = lambda a: pl.BlockSpec(a.shape, lambda b, pt: (0,) * a.ndim)
    new_s, new_w = pad_new(C_SKV), pad_new(C_WKV)
    grid_spec = pltpu.PrefetchScalarGridSpec(
        num_scalar_prefetch=1,
        grid=(bsz // nb,),
        in_specs=[per_b(qpad), per_b(gl), per_b(ckv)]
        + [pl.BlockSpec((1, 1, 2 * KV_W, PAGE), (lambda b, pt, e=e, p=p: (layer, pt[b * nb + e, p], 0, 0)))
           for e in range(nb) for p in range(npages)]
        + [pl.BlockSpec((1, nb) + win_buf.shape[2:], lambda b, pt: (layer, b, 0, 0)),
           per_b(new_s), per_b(new_w), const(bcs), const(tab), const(tnew), const(cov)],
        out_specs=pl.BlockSpec((nb, S_ROWS, KV_W), lambda b, pt: (b, 0, 0)),
        scratch_shapes=[pltpu.VMEM((nb, S_ROWS, (npages + 1) * LANES), F32)],
    )
    out = pl.pallas_call(
        functools.partial(_nsa_sample_kernel, nb=nb, npages=npages, past=past, t_new=t_new),
        grid_spec=grid_spec,
        out_shape=jax.ShapeDtypeStruct((bsz, S_ROWS, KV_W), F32),
        compiler_params=_cparams(("parallel",)),
        name="nsa_sample",
    )(page_table, qpad, gl, ckv, *([pool_s] * (nb * npages)), win_buf, new_s, new_w, bcs, tab, tnew, cov)
    o6 = out.reshape(bsz, Q_PER_KV, KV_HEADS, t_new, KV_HEADS, NSA_HD)
    o5 = jnp.stack([o6[:, :, g, :, g, :] for g in range(KV_HEADS)], axis=2)
    return o5.transpose(0, 3, 2, 1, 4).reshape(bsz, t_new, NSA_W)


def _in_proj_weight(w):
    g0 = 4 * ML_W
    g1 = g0 + 2 * ML_HEADS
    n_kv = C_GATE
    assert w.shape[1] == n_kv + 2 * ML_HEADS + 3 * NSA_HEADS
    tail = w.shape[1] - (g1 - g0) - n_kv
    pieces = [w[:, :g0], w[:, g1:g1 + n_kv - g0], w[:, g0:g1], w[:, w.shape[1] - tail:],
              jnp.zeros((w.shape[0], N_IN_PAD - w.shape[1]), w.dtype)]
    return jnp.concatenate(pieces, axis=1).astype(BF16)


def _layer(x, mods, rows_per_group, lw, tm, tm_ffn, mixer):
    sh1, sc1, g1, sh2, sc2, g2 = mods
    w_in_bf, w_out_bf, w_up_bf, w_down_bf, ln_g, ln_b = lw
    proj, kv_bf = in_proj(x, sc1, sh1, w_in_bf, tm, rows_per_group // tm)
    h_ml, o_nsa, extras = mixer(proj, kv_bf)
    x1 = out_proj_ln(x, h_ml, o_nsa, w_out_bf, g1, ln_g[0], ln_b[0], tm, rows_per_group // tm)
    x2 = ffn_ln(x1, sc2, sh2, g2, w_up_bf, w_down_bf, ln_g[1], ln_b[1], tm_ffn, rows_per_group // tm_ffn)
    return x2, proj, extras


def kernel(x_prompt, x_sample, cache_cmp_kv, cache_slc_kv, cache_win_kv, state_mlstm_C, state_mlstm_n,
           state_mlstm_m, page_table, c_prompt, c_sample, w_ada, b_ada, w_in, b_gate, ml_norm_g, cmp_pe,
           cmp_w1, cmp_w2, rel_bias, w_out, ln_g, ln_b, w_up, w_down):
    bsz, seq, d = x_prompt.shape
    bs, ts, _ = x_sample.shape
    depth = w_in.shape[0]
    n_phys = cache_cmp_kv.shape[1]
    npages = page_table.shape[1]
    win_len = cache_win_kv.shape[2]
    tm = 512
    tm_ffn = 1024
    t_ml = 256
    t_pad = 16

    rows_last = lambda a: a.transpose(0, 1, 3, 4, 5, 2).reshape(a.shape[0], a.shape[1], 2 * KV_W, a.shape[2])
    pool_c, pool_s, win_t = rows_last(cache_cmp_kv), rows_last(cache_slc_kv), rows_last(cache_win_kv)

    prompt_tabs = _prompt_tables(rel_bias)
    nc_rows = bsz + bs
    c_all = jnp.pad(jnp.concatenate([c_prompt, c_sample], axis=0), ((0, -nc_rows % 8), (0, 0)))

    xp = x_prompt.reshape(bsz * seq, d)
    xs = x_sample.reshape(bs * ts, d)
    outs = {k: [] for k in ("cmp_p", "cmp_s", "slc_p", "slc_s", "win_p", "win_s",
                            "C_p", "C_s", "n_p", "n_s", "m_p", "m_s")}
    kv_shape = lambda b, t: (b, t, 2, KV_HEADS, NSA_HD)

    for l in range(depth):
        ada = ada_mod(c_all, w_ada[l].astype(BF16), b_ada[l]).reshape(c_all.shape[0], 6, d)
        mods_p = [ada[:bsz, k][:, None, :] for k in range(6)]
        mods_s = [jnp.repeat(ada[bsz:nc_rows, k], ts, axis=0)[None] for k in range(6)]
        w_in_bf = _in_proj_weight(w_in[l])
        lw = (w_in_bf, w_out[l].astype(BF16), w_up[l].astype(BF16), w_down[l].astype(BF16), ln_g[l], ln_b[l])
        cw = _compress_weights(cmp_pe[l], cmp_w1[l], cmp_w2[l])

        def prompt_mixer(proj, kv_bf):
            proj = proj.reshape(bsz, seq, N_IN_PAD)
            h_ml, c1, n1, m1 = mlstm(proj, b_gate[l], ml_norm_g[l],
                                     jnp.zeros((bsz, ML_HEADS, ML_HD, ML_HD), F32),
                                     jnp.zeros((bsz, ML_HEADS, ML_HD), F32), jnp.zeros((bsz, ML_HEADS), F32),
                                     nb=bsz, t=t_ml, valid=t_ml)
            ckv = compress(proj, [lambda s: (lambda b: (b, 0, C_CKV // KV_W + s))], 0, (), bsz, 1, seq, cw)
            o_nsa = nsa_prompt(proj, kv_bf.reshape(bsz, seq, 4 * KV_W), ckv, prompt_tabs)
            return h_ml.reshape(bsz * seq, ML_W), o_nsa.reshape(bsz * seq, NSA_W), (c1, n1, m1)

        def sample_mixer(proj, kv_bf):
            del kv_bf
            proj = proj.reshape(bs, ts, N_IN_PAD)
            proj_pad = jnp.pad(proj, ((0, 0), (0, t_pad - ts), (0, 0)))
            h_ml, c1, n1, m1 = mlstm(proj_pad, b_gate[l], ml_norm_g[l], state_mlstm_C, state_mlstm_n,
                                     state_mlstm_m, nb=8, t=t_pad, valid=ts, layer=l)
            pages = [(lambda b, pt, e=e, p=p: (l, pt[b * SAMPLE_NB + e, p], 0, 0))
                     for e in range(SAMPLE_NB) for p in range(npages)]
            ckv = compress(pool_c, pages, 1, (page_table,), bs // SAMPLE_NB, SAMPLE_NB * npages, PAGE, cw,
                           transposed=True, nb=SAMPLE_NB)
            o_nsa = nsa_sample(proj, ckv, pool_s, win_t, l, page_table, rel_bias)
            return (h_ml[:, :ts].reshape(bs * ts, ML_W), o_nsa.reshape(bs * ts, NSA_W).astype(BF16),
                    (c1, n1, m1))

        xp, proj_p, st_p = _layer(xp, mods_p, seq, lw, tm, tm_ffn, prompt_mixer)
        xs, proj_s, st_s = _layer(xs, mods_s, bs * ts, lw, bs * ts, bs * ts, sample_mixer)

        proj_p = proj_p.reshape(bsz, seq, N_IN_PAD)
        proj_s = proj_s.reshape(bs, ts, N_IN_PAD)
        rows = lambda proj, c: proj[:, :, c:c + 2 * KV_W]
        win = min(WINDOW, seq)
        outs["cmp_p"].append(rows(proj_p, C_CKV).reshape(kv_shape(bsz, seq)))
        outs["slc_p"].append(rows(proj_p, C_SKV).reshape(kv_shape(bsz, seq)))
        outs["win_p"].append(rows(proj_p, C_WKV)[:, seq - win:].reshape(kv_shape(bsz, win)))
        outs["cmp_s"].append(rows(proj_s, C_CKV).reshape(kv_shape(bs, ts)))
        outs["slc_s"].append(rows(proj_s, C_SKV).reshape(kv_shape(bs, ts)))
        outs["win_s"].append(rows(proj_s, C_WKV).reshape(kv_shape(bs, ts)))
        for tag, st in (("p", st_p), ("s", st_s)):
            outs["C_" + tag].append(st[0])
            outs["n_" + tag].append(st[1])
            outs["m_" + tag].append(st[2])

    stk = lambda k: jnp.stack(outs[k])
    assert ts <= win_len
    win_s = jnp.concatenate([cache_win_kv[:, :, ts:], stk("win_s")], axis=2)
    return (xp.reshape(bsz, seq, d), xs.reshape(bs, ts, d),
            stk("cmp_p"), stk("cmp_s"), stk("slc_p"), stk("slc_s"), stk("win_p"), win_s,
            stk("C_p"), stk("C_s"), stk("n_p"), stk("n_s"), stk("m_p"), stk("m_s"))
```

```python
import functools
import math

import numpy as np
import jax
import jax.numpy as jnp
from jax import lax
from jax.experimental import pallas as pl
from jax.experimental.pallas import tpu as pltpu

F32 = jnp.float32
BF16 = jnp.bfloat16

D_MODEL = 1024
DEPTH = 2
PAGE = 128
ML_HEADS = 4
ML_HD = 128
ML_W = ML_HEADS * ML_HD
NSA_HEADS = 8
NSA_HD = 64
NSA_W = NSA_HEADS * NSA_HD
KV_HEADS = 2
Q_PER_KV = 4
KV_W = KV_HEADS * NSA_HD
CMP_BLOCK = 32
CMP_STRIDE = 16
CMP_HIDDEN = 256
SEL_BLOCK = 64
N_SELECT = 16
WINDOW = 512
NUM_BUCKETS = 32
REL_MAX_DIST = 128
D_FF = 4 * D_MODEL
ALPHA = (2 * DEPTH) ** 0.25
LN_EPS = 1e-5
NEG = -1e30
FORCE = 1e9
REMOVED = -3e38
LOG2E = math.log2(math.e)

LANES = 128
QT = 128
KT = 128
V_ROWS = 80
NC_PAD_FRONT = 16
NEAR_ROWS = 24

C_MQ, C_MK, C_MV, C_MO = 0, 512, 1024, 1536
C_NQ = 2048
C_CKV, C_SKV, C_WKV = 2560, 2816, 3072
C_GATE = 3328
C_NG = 3336
N_IN_PAD = 3456
VMEM_LIMIT = 56 * 1024 * 1024


def _cparams(sem):
    return pltpu.CompilerParams(dimension_semantics=sem, vmem_limit_bytes=VMEM_LIMIT)


def _dot(a, b):
    return jnp.dot(a, b, preferred_element_type=F32)


def _dot_nt(a, b):
    return lax.dot_general(a, b, (((1,), (1,)), ((), ())), preferred_element_type=F32)


def _dot_tn(a, b):
    return lax.dot_general(a, b, (((0,), (0,)), ((), ())), preferred_element_type=F32)


def _sigmoid(x):
    return 1.0 / (1.0 + jnp.exp(-x))


def _layer_norm(y, g, b):
    mu = jnp.mean(y, axis=-1, keepdims=True)
    d = y - mu
    var = jnp.mean(d * d, axis=-1, keepdims=True)
    return d * lax.rsqrt(var + LN_EPS) * g + b


def _ada_kernel(c_ref, w_ref, b_ref, o_ref):
    c = c_ref[...]
    a = (c * _sigmoid(c)).astype(BF16)
    o_ref[...] = _dot(a, w_ref[...]) + b_ref[...]


def ada_mod(c, w_bf, b):
    m, d = c.shape
    n = w_bf.shape[1]
    tn = 1536
    return pl.pallas_call(
        _ada_kernel,
        grid=(n // tn,),
        in_specs=[pl.BlockSpec((m, d), lambda j: (0, 0)),
                  pl.BlockSpec((d, tn), lambda j: (0, j)),
                  pl.BlockSpec((1, tn), lambda j: (0, j))],
        out_specs=pl.BlockSpec((m, tn), lambda j: (0, j)),
        out_shape=jax.ShapeDtypeStruct((m, n), F32),
        compiler_params=_cparams(("arbitrary",)),
        name="ada_mod",
    )(c, w_bf, b.reshape(1, n))


def _inproj_kernel(x_ref, sc_ref, sh_ref, w_ref, o_ref, kv_ref):
    u = (x_ref[...] * (1.0 + sc_ref[0]) + sh_ref[0]).astype(BF16)
    y = _dot(u, w_ref[...])
    o_ref[...] = y
    kv_ref[...] = y[:, C_SKV:C_WKV + 2 * KV_W].astype(BF16)


def in_proj(x, sc, sh, w_bf, tm, tiles_per_group):
    m = x.shape[0]
    n = w_bf.shape[1]
    r = sc.shape[1]
    mod = pl.BlockSpec((1, r, D_MODEL), lambda i: (i // tiles_per_group, 0, 0))
    return pl.pallas_call(
        _inproj_kernel,
        grid=(m // tm,),
        in_specs=[pl.BlockSpec((tm, D_MODEL), lambda i: (i, 0)), mod, mod,
                  pl.BlockSpec((D_MODEL, n), lambda i: (0, 0))],
        out_specs=[pl.BlockSpec((tm, n), lambda i: (i, 0)), pl.BlockSpec((tm, 4 * KV_W), lambda i: (i, 0))],
        out_shape=[jax.ShapeDtypeStruct((m, n), F32), jax.ShapeDtypeStruct((m, 4 * KV_W), BF16)],
        compiler_params=_cparams(("parallel",)),
        name="in_proj",
    )(x, sc, sh, w_bf)


def _outproj_kernel(x_ref, a1_ref, a2_ref, w1_ref, w2_ref, g_ref, lg_ref, lb_ref, o_ref):
    mixed = _dot(a1_ref[...], w1_ref[...]) + _dot(a2_ref[...], w2_ref[...])
    y = ALPHA * x_ref[...] + g_ref[0] * mixed
    o_ref[...] = _layer_norm(y, lg_ref[...], lb_ref[...])


def out_proj_ln(x, a1, a2, w_bf, gate, ln_g, ln_b, tm, tiles_per_group):
    m = x.shape[0]
    k1 = a1.shape[1]
    r = gate.shape[1]
    return pl.pallas_call(
        _outproj_kernel,
        grid=(m // tm,),
        in_specs=[pl.BlockSpec((tm, D_MODEL), lambda i: (i, 0)),
                  pl.BlockSpec((tm, k1), lambda i: (i, 0)),
                  pl.BlockSpec((tm, k1), lambda i: (i, 0)),
                  pl.BlockSpec((k1, D_MODEL), lambda i: (0, 0)),
                  pl.BlockSpec((k1, D_MODEL), lambda i: (1, 0)),
                  pl.BlockSpec((1, r, D_MODEL), lambda i: (i // tiles_per_group, 0, 0)),
                  pl.BlockSpec((1, D_MODEL), lambda i: (0, 0)),
                  pl.BlockSpec((1, D_MODEL), lambda i: (0, 0))],
        out_specs=pl.BlockSpec((tm, D_MODEL), lambda i: (i, 0)),
        out_shape=jax.ShapeDtypeStruct((m, D_MODEL), F32),
        compiler_params=_cparams(("parallel",)),
        name="out_proj_ln",
    )(x, a1, a2, w_bf, w_bf, gate, ln_g.reshape(1, -1), ln_b.reshape(1, -1))


def _ffn_kernel(x_ref, sc_ref, sh_ref, g_ref, wu_ref, wd_ref, lg_ref, lb_ref, o_ref, u_scr, acc_scr):
    f = pl.program_id(1)

    @pl.when(f == 0)
    def _():
        u_scr[...] = (x_ref[...] * (1.0 + sc_ref[0]) + sh_ref[0]).astype(BF16)
        acc_scr[...] = jnp.zeros_like(acc_scr)

    h = jnp.maximum(_dot(u_scr[...], wu_ref[...]), 0.0)
    acc_scr[...] += _dot((h * h).astype(BF16), wd_ref[...])

    @pl.when(f == pl.num_programs(1) - 1)
    def _():
        y = ALPHA * x_ref[...] + g_ref[0] * acc_scr[...]
        o_ref[...] = _layer_norm(y, lg_ref[...], lb_ref[...])


def ffn_ln(x, sc, sh, gate, wu_bf, wd_bf, ln_g, ln_b, tm, tiles_per_group):
    m = x.shape[0]
    tf = 512
    r = sc.shape[1]
    mod = pl.BlockSpec((1, r, D_MODEL), lambda i, f: (i // tiles_per_group, 0, 0))
    return pl.pallas_call(
        _ffn_kernel,
        grid=(m // tm, D_FF // tf),
        in_specs=[pl.BlockSpec((tm, D_MODEL), lambda i, f: (i, 0)),
                  mod, mod, mod,
                  pl.BlockSpec((D_MODEL, tf), lambda i, f: (0, f)),
                  pl.BlockSpec((tf, D_MODEL), lambda i, f: (f, 0)),
                  pl.BlockSpec((1, D_MODEL), lambda i, f: (0, 0)),
                  pl.BlockSpec((1, D_MODEL), lambda i, f: (0, 0))],
        out_specs=pl.BlockSpec((tm, D_MODEL), lambda i, f: (i, 0)),
        out_shape=jax.ShapeDtypeStruct((m, D_MODEL), F32),
        scratch_shapes=[pltpu.VMEM((tm, D_MODEL), BF16), pltpu.VMEM((tm, D_MODEL), F32)],
        compiler_params=_cparams(("parallel", "arbitrary")),
        name="ffn_ln",
    )(x, sc, sh, gate, wu_bf, wd_bf, ln_g.reshape(1, -1), ln_b.reshape(1, -1))


def _mlstm_kernel(bg_ref, q_ref, k_ref, v_ref, o_ref, g_ref, ng_ref, c0_ref, n0_ref, m0_ref,
                  h_ref, c_out, n_out, m_out, c_scr, n_scr, m_scr, *, nb, t, valid):
    ci = pl.program_id(1)

    @pl.when(ci == 0)
    def _():
        c_scr[...] = c0_ref[...]
        n_scr[...] = n0_ref[...]
        m_scr[...] = m0_ref[...]

    row = lax.broadcasted_iota(jnp.int32, (t, t), 0)
    col = lax.broadcasted_iota(jnp.int32, (t, t), 1)
    tri = col <= row
    tri_t = row <= col
    eye = row == col
    row1 = lax.broadcasted_iota(jnp.int32, (t, 1), 0)
    scale = ML_HD ** -0.5

    def gate_scans(b, h):
        gates = g_ref[b]
        ig_col = gates[:, h:h + 1] + bg_ref[h]
        fr = gates[:, ML_HEADS + h:ML_HEADS + h + 1] + bg_ref[ML_HEADS + h]
        lf_col = jnp.minimum(fr, 0.0) - jnp.log1p(jnp.exp(-jnp.abs(fr)))
        if valid < t:
            ig_col = jnp.where(row1 < valid, ig_col, NEG)
            lf_col = jnp.where(row1 < valid, lf_col, 0.0)
        lf_row = jnp.sum(jnp.where(eye, lf_col, 0.0), axis=0, keepdims=True)
        ig_row = jnp.sum(jnp.where(eye, ig_col, 0.0), axis=0, keepdims=True)
        f_col = jnp.sum(jnp.where(tri, lf_row, 0.0), axis=1, keepdims=True)
        f_row = jnp.sum(jnp.where(tri_t, lf_col, 0.0), axis=0, keepdims=True)
        a_row = ig_row - f_row
        a_col = ig_col - f_col
        m0 = m_scr[b, h]
        cm_col = jnp.max(jnp.where(tri, a_row, NEG), axis=1, keepdims=True)
        g_col = jnp.maximum(m0, cm_col)
        g_end = jnp.maximum(m0, jnp.max(a_row, axis=1, keepdims=True))
        return dict(dmat=jnp.exp(jnp.where(tri, a_row - g_col, NEG)), decay_col=jnp.exp(m0 - g_col),
                    m_col=f_col + g_col, m_end=jnp.sum(lf_row, axis=1, keepdims=True) + g_end,
                    w_end_col=jnp.exp(a_col - g_end), carry=jnp.exp(m0 - g_end))

    streams = [(b, h) for b in range(nb) for h in range(ML_HEADS)]
    group = 2 * ML_HEADS
    for g0 in range(0, len(streams), group):
        grp = streams[g0:g0 + group]
        hs = [slice(h * ML_HD, (h + 1) * ML_HD) for _, h in grp]
        sc = [gate_scans(b, h) for b, h in grp]
        qf = [q_ref[b][:, s] for (b, _), s in zip(grp, hs)]
        qb = [x.astype(BF16) for x in qf]
        kf = [k_ref[b][:, s] * scale for (b, _), s in zip(grp, hs)]
        kb = [x.astype(BF16) for x in kf]
        vb = [v_ref[b][:, s].astype(BF16) for (b, _), s in zip(grp, hs)]
        cmat = [c_scr[b, h] for b, h in grp]
        nvec = [n_scr[b, h] for b, h in grp]
        n_st = range(len(grp))

        w = [_dot_nt(qb[i], kb[i]) * sc[i]["dmat"] for i in n_st]
        qc = [_dot(qb[i], cmat[i].astype(BF16)) for i in n_st]
        num = [_dot(w[i].astype(BF16), vb[i]) + qc[i] * sc[i]["decay_col"] for i in n_st]
        den = [jnp.sum(w[i], axis=1, keepdims=True)
               + jnp.sum(qf[i] * nvec[i], axis=1, keepdims=True) * sc[i]["decay_col"] for i in n_st]
        hh = [num[i] / jnp.maximum(jnp.abs(den[i]), jnp.exp(-sc[i]["m_col"])) for i in n_st]
        mu = [jnp.mean(x, axis=1, keepdims=True) for x in hh]
        dd = [hh[i] - mu[i] for i in n_st]
        var = [jnp.mean(x * x, axis=1, keepdims=True) for x in dd]
        for i, (b, h) in enumerate(grp):
            hn = dd[i] * lax.rsqrt(var[i] + LN_EPS) * ng_ref[:, hs[i]] * _sigmoid(o_ref[b][:, hs[i]])
            h_ref[b, :, hs[i]] = hn.astype(h_ref.dtype)

        kw = [kf[i] * sc[i]["w_end_col"] for i in n_st]
        upd = [_dot_tn(kw[i].astype(BF16), vb[i]) for i in n_st]
        for i, (b, h) in enumerate(grp):
            c_scr[b, h] = sc[i]["carry"] * cmat[i] + upd[i]
            n_scr[b, h] = sc[i]["carry"] * nvec[i] + jnp.sum(kw[i], axis=0, keepdims=True)
            m_scr[b, h] = sc[i]["m_end"]

    @pl.when(ci == pl.num_programs(1) - 1)
    def _():
        c_out[...] = c_scr[...]
        n_out[...] = n_scr[...]
        m_out[...] = m_scr[...]


def mlstm(proj, b_gate, norm_g, c0, n0, m0, *, nb, t, valid, layer=None):
    bsz, length, _ = proj.shape
    nchunk = length // t
    wide = lambda cb: pl.BlockSpec((nb, t, ML_W), lambda i, c: (i, c, cb))
    state4 = lambda s: pl.BlockSpec((nb,) + s, lambda i, c: (i, 0, 0, 0))
    if layer is None:
        state_in = state4
        lead = (bsz,)
    else:
        state_in = lambda s: pl.BlockSpec((None, nb) + s, lambda i, c: (layer, i, 0, 0, 0))
        lead = (c0.shape[0], bsz)
    kern = functools.partial(_mlstm_kernel, nb=nb, t=t, valid=valid)
    h, c1, n1, m1 = pl.pallas_call(
        kern,
        grid=(bsz // nb, nchunk),
        in_specs=[pl.BlockSpec(memory_space=pltpu.SMEM),
                  wide(C_MQ // ML_W), wide(C_MK // ML_W), wide(C_MV // ML_W), wide(C_MO // ML_W),
                  pl.BlockSpec((nb, t, LANES), lambda i, c: (i, c, C_GATE // LANES)),
                  pl.BlockSpec((1, ML_W), lambda i, c: (0, 0)),
                  state_in((ML_HEADS, ML_HD, ML_HD)), state_in((ML_HEADS, 1, ML_HD)), state_in((ML_HEADS, 1, 1))],
        out_specs=[pl.BlockSpec((nb, t, ML_W), lambda i, c: (i, c, 0)),
                   state4((ML_HEADS, ML_HD, ML_HD)), state4((ML_HEADS, 1, ML_HD)), state4((ML_HEADS, 1, 1))],
        out_shape=[jax.ShapeDtypeStruct((bsz, length, ML_W), BF16),
                   jax.ShapeDtypeStruct((bsz, ML_HEADS, ML_HD, ML_HD), F32),
                   jax.ShapeDtypeStruct((bsz, ML_HEADS, 1, ML_HD), F32),
                   jax.ShapeDtypeStruct((bsz, ML_HEADS, 1, 1), F32)],
        scratch_shapes=[pltpu.VMEM((nb, ML_HEADS, ML_HD, ML_HD), F32),
                        pltpu.VMEM((nb, ML_HEADS, 1, ML_HD), F32),
                        pltpu.VMEM((nb, ML_HEADS, 1, 1), F32)],
        compiler_params=_cparams(("parallel", "arbitrary")),
        name="mlstm",
    )(b_gate, proj, proj, proj, proj, proj, norm_g.reshape(1, ML_W),
      c0, n0.reshape(lead + (ML_HEADS, 1, ML_HD)), m0.reshape(lead + (ML_HEADS, 1, 1)))
    return h, c1, n1.reshape(bsz, ML_HEADS, ML_HD), m1.reshape(bsz, ML_HEADS)


def _gelu_tanh(x):
    return 0.5 * x * (1.0 + jnp.tanh(math.sqrt(2.0 / math.pi) * (x + 0.044715 * (x * x * x))))


def _compress_kernel(*refs, npages, rows, transposed, nb):
    jp = rows // CMP_STRIDE
    j = npages * jp
    lane = lax.broadcasted_iota(jnp.int32, (j, LANES), 1)
    low = lane < NSA_HD
    if transposed:
        pe_ref, w1_ref, w2_ref, o_ref, xk_scr, xv_scr = refs[-6:]
        page_refs = refs[-6 - npages:-6]
        for p in range(npages):
            x = page_refs[p][0, 0].T
            xk_scr[p * rows:(p + 1) * rows, :] = x[:, :KV_W]
            xv_scr[p * rows:(p + 1) * rows, :] = x[:, KV_W:]

        def chunk_phase(c, s):
            return (xk_scr, xv_scr)[s][pl.ds(c, j, stride=CMP_STRIDE), :]
    else:
        pe_ref, w1_ref, w2_ref, o_ref = refs[-4:]
        page_refs = refs[-4 - 2 * npages:-4]

        def chunk_phase(c, s):
            parts = [page_refs[2 * p + s][0, pl.ds(c, jp, stride=CMP_STRIDE), :] for p in range(npages)]
            return parts[0] if npages == 1 else jnp.concatenate(parts, axis=0)

    out = jnp.zeros((j, 2 * KV_HEADS * NSA_HD), F32)
    for s in range(2):
        halves = [[], []]
        for c in range(0, CMP_STRIDE, 2):
            va = chunk_phase(c, s)
            vb = chunk_phase(c + 1, s)
            halves[0].append(jnp.where(low, va, pltpu.roll(vb, NSA_HD, axis=1)).astype(BF16))
            halves[1].append(jnp.where(low, pltpu.roll(va, NSA_HD, axis=1), vb).astype(BF16))
        pe_c = _dot(pe_ref[s], w1_ref[s])
        pe_const = pe_c[0:1, :CMP_HIDDEN] + pe_c[1:2, CMP_HIDDEN:]
        for g in range(KV_HEADS):
            lhs = jnp.concatenate(halves[g], axis=1)
            acc = _dot(lhs, w1_ref[s])
            hid = acc[:, :CMP_HIDDEN] + pltpu.roll(acc[:, CMP_HIDDEN:], j - 1, axis=0) + pe_const
            out = out + _dot(_gelu_tanh(hid).astype(BF16), w2_ref[s * KV_HEADS + g])
    je = j // nb
    for e in range(nb):
        o_ref[e] = out[e * je:(e + 1) * je, :]


def _compress_weights(pe, w1, w2):
    w1r = w1.reshape(2, CMP_BLOCK, NSA_HD, CMP_HIDDEN)
    wa = w1r[:, :CMP_STRIDE].reshape(2, 4, 4 * NSA_HD, CMP_HIDDEN)
    wb = w1r[:, CMP_STRIDE:].reshape(2, 4, 4 * NSA_HD, CMP_HIDDEN)
    w1p = jnp.concatenate([wa, wb], axis=-1).astype(BF16).reshape(2, CMP_STRIDE * NSA_HD, 2 * CMP_HIDDEN)
    pea = pe[:, :CMP_STRIDE].reshape(2, 1, CMP_STRIDE * NSA_HD)
    peb = pe[:, CMP_STRIDE:].reshape(2, 1, CMP_STRIDE * NSA_HD)
    pep = jnp.concatenate([pea, peb, jnp.zeros((2, 14, CMP_STRIDE * NSA_HD), F32)], axis=1).astype(BF16)
    w2p = jnp.zeros((2, KV_HEADS, CMP_HIDDEN, 2, KV_HEADS, NSA_HD), F32)
    for s in range(2):
        for g in range(KV_HEADS):
            w2p = w2p.at[s, g, :, s, g, :].set(w2[s])
    w2p = w2p.reshape(2 * KV_HEADS, CMP_HIDDEN, 2 * KV_W).astype(BF16)
    return pep, w1p, w2p


def compress(pages, page_index_maps, num_scalar_prefetch, prefetch, grid_n, npages, rows, cw, transposed=False,
             nb=1):
    pep, w1p, w2p = cw
    j = npages * rows // CMP_STRIDE
    const = lambda nd: (lambda *a: (0,) * nd)
    if transposed:
        page_specs = [pl.BlockSpec((1, 1, 2 * KV_W, rows), im) for im in page_index_maps]
        scratch = [pltpu.VMEM((npages * rows, KV_W), F32)] * 2
    else:
        page_specs = [pl.BlockSpec((1, rows, KV_W), im(s)) for im in page_index_maps for s in range(2)]
        scratch = []
    grid_spec = pltpu.PrefetchScalarGridSpec(
        num_scalar_prefetch=num_scalar_prefetch,
        grid=(grid_n,),
        in_specs=page_specs
        + [pl.BlockSpec(pep.shape, const(3)), pl.BlockSpec(w1p.shape, const(3)), pl.BlockSpec(w2p.shape, const(3))],
        out_specs=pl.BlockSpec((nb, j // nb, 2 * KV_W), lambda b, *a: (b, 0, 0)),
        scratch_shapes=scratch,
    )
    return pl.pallas_call(
        functools.partial(_compress_kernel, npages=npages, rows=rows, transposed=transposed, nb=nb),
        grid_spec=grid_spec,
        out_shape=jax.ShapeDtypeStruct((grid_n * nb, j // nb, 2 * KV_W), F32),
        compiler_params=_cparams(("parallel",)),
        name="compress",
    )(*prefetch, *([pages] * len(page_specs)), pep, w1p, w2p)


def _rel_bucket(dist):
    n = jnp.maximum(dist, 0)
    max_exact = NUM_BUCKETS // 2
    nf = jnp.maximum(n, 1).astype(F32)
    large = max_exact + (jnp.log(nf / max_exact) / math.log(REL_MAX_DIST / max_exact)
                         * (NUM_BUCKETS - max_exact)).astype(jnp.int32)
    large = jnp.minimum(large, NUM_BUCKETS - 1)
    return jnp.where(n < max_exact, n, large)


def _delta_bias(rel_bias, dist):
    by_dist = rel_bias[_rel_bucket(jnp.arange(REL_MAX_DIST))] - rel_bias[NUM_BUCKETS - 1]
    near = (dist >= 0) & (dist < REL_MAX_DIST)
    onehot = (jnp.clip(dist, 0, REL_MAX_DIST - 1)[..., None] == jnp.arange(REL_MAX_DIST)).astype(F32)
    val = jnp.einsum("...d,dh->...h", onehot, by_dist, precision=lax.Precision.HIGHEST)
    return jnp.where(near[..., None], val, 0.0)


def _lanes_ri(tab, g):
    rows = tab.shape[0]
    return tab[:, :, g * Q_PER_KV:(g + 1) * Q_PER_KV].transpose(0, 2, 1).reshape(rows, Q_PER_KV * QT)


def _prompt_tables(rel_bias):
    i = jnp.arange(QT)[None, :]
    j = jnp.arange(KT)[:, None]
    d_diag = i - j
    diag = jnp.where((d_diag >= 0)[..., None], _delta_bias(rel_bias, d_diag), NEG)
    sub = _delta_bias(rel_bias, i + KT - j)
    n2 = jnp.arange(NEAR_ROWS)[:, None]
    near = _delta_bias(rel_bias, i + (CMP_STRIDE * NC_PAD_FRONT - CMP_BLOCK + 1) - CMP_STRIDE * n2)
    stack = lambda tab: jnp.stack([_lanes_ri(tab, g) for g in range(KV_HEADS)]) * LOG2E
    return stack(sub), stack(diag), stack(near)


def _cover_t(n_sel, n_cmp, ncp):
    n = np.arange(ncp) - NC_PAD_FRONT
    c_start = n * CMP_STRIDE
    s_start = np.arange(n_sel)[:, None] * SEL_BLOCK
    cov = (c_start[None, :] < s_start + SEL_BLOCK) & (c_start[None, :] + CMP_BLOCK > s_start)
    cov &= ((n >= 0) & (n < n_cmp))[None, :]
    return jnp.asarray(cov, BF16)


def _nsa_prompt_kernel(q_ref, gt_ref, kc_ref, vct_ref, cov_ref, bct_ref, ksa_ref, vst_ref, kw_ref, vwt_ref,
                       tsub_ref, tdiag_ref, o_ref, s_scr, p_scr, q_scr, sa_scr, sb_scr, m_scr, l_scr, acc_scr,
                       ocmp_scr, osel_scr, ot_scr, *, nsel, ncp, ncv):
    blk = pl.program_id(1)
    t0 = blk * QT
    lanes4 = Q_PER_KV * QT
    q_t = (q_ref[0] * (NSA_HD ** -0.5 * LOG2E)).T

    def tile4(x):
        return jnp.concatenate([x] * Q_PER_KV, axis=1)

    def weighted_values(scores, m, values):
        probs = [jnp.exp2(s - m).astype(BF16) for s in scores]
        p_all = probs[0] if len(probs) == 1 else jnp.concatenate(probs, axis=0)
        v_all = values[0] if len(values) == 1 else jnp.concatenate(values, axis=1)
        return _dot(v_all, p_all)

    def tiles_max(scores, m):
        for s in scores:
            m = jnp.maximum(m, jnp.max(s, axis=0, keepdims=True))
        return m

    def normalise(acc, m):
        inv = jnp.where(m > 0.5 * NEG, 1.0 / acc[NSA_HD:NSA_HD + 1, :], 0.0)
        return acc[0:NSA_HD, :] * inv

    def update(g, scores, values):
        m_old = m_scr[g]
        m_new = tiles_max(scores, m_old)
        acc_scr[g] = jnp.exp2(m_old - m_new) * acc_scr[g] + weighted_values(scores, m_new, values)
        m_scr[g] = m_new

    row_k = lax.broadcasted_iota(jnp.int32, (KT, QT), 0)
    lane_q = lax.broadcasted_iota(jnp.int32, (KT, QT), 1)

    importance = []
    for g in range(KV_HEADS):
        gs = slice(g * NSA_HD, (g + 1) * NSA_HD)
        qg = jnp.concatenate([q_t[(g * Q_PER_KV + r) * NSA_HD:(g * Q_PER_KV + r + 1) * NSA_HD, :]
                              for r in range(Q_PER_KV)], axis=1)
        zero = jnp.zeros_like(qg)
        qpad = jnp.concatenate([qg, zero] if g == 0 else [zero, qg], axis=0).astype(BF16)
        q_scr[g, 0:KV_W, :] = qpad

        s_scr[...] = _dot(kc_ref[0], qpad)
        near0 = pl.multiple_of(blk * (QT // CMP_STRIDE), 8)
        s_scr[pl.ds(near0, NEAR_ROWS), :] = s_scr[pl.ds(near0, NEAR_ROWS), :] + bct_ref[g]
        n_all = blk * (QT // CMP_STRIDE) - (CMP_BLOCK // CMP_STRIDE)
        n_any = n_all + (QT - 1) // CMP_STRIDE + 1
        m_scr[g] = jnp.full((1, lanes4), NEG, F32)
        l_scr[g] = jnp.zeros((1, lanes4), F32)
        nchunk = ncp // KT
        chunk_kind = []
        for c in range(nchunk):
            cs = slice(c * KT, (c + 1) * KT)
            lo, hi = c * KT - NC_PAD_FRONT, (c + 1) * KT - NC_PAD_FRONT - 1
            skip = (lo > n_any) if lo < ncv else True
            full = (hi <= n_all) if (lo >= 0 and hi < ncv) else False
            chunk_kind.append((cs, skip, full))
            if skip is True:
                continue

            @pl.when(full)
            def _(cs=cs):
                m_scr[g] = jnp.maximum(m_scr[g], jnp.max(s_scr[cs, :], axis=0, keepdims=True))

            @pl.when(jnp.logical_not(full) & jnp.logical_not(skip))
            def _(cs=cs, lo=lo):
                n = row_k + lo
                ok = (n >= 0) & (n < ncv) & (CMP_STRIDE * n + (CMP_BLOCK - 1) <= t0 + lane_q)
                sc = s_scr[cs, :] + tile4(jnp.where(ok, 0.0, NEG))
                s_scr[cs, :] = sc
                m_scr[g] = jnp.maximum(m_scr[g], jnp.max(sc, axis=0, keepdims=True))

        mx = m_scr[g]
        for cs, skip, _ in chunk_kind:
            if skip is True:
                p_scr[cs, :] = jnp.zeros((KT, lanes4), BF16)
                continue

            @pl.when(jnp.logical_not(skip))
            def _(cs=cs):
                p = jnp.exp2(s_scr[cs, :] - mx)
                l_scr[g] = l_scr[g] + jnp.sum(p, axis=0, keepdims=True)
                p_scr[cs, :] = p.astype(BF16)

            @pl.when(skip)
            def _(cs=cs):
                p_scr[cs, :] = jnp.zeros((KT, lanes4), BF16)

        inv_c = jnp.where(mx > 0.5 * NEG, 1.0 / l_scr[g], 0.0)
        ocmp_scr[g] = _dot(vct_ref[0, gs, :], p_scr[...]) * inv_c
        imp4 = _dot(cov_ref[...], p_scr[...]) * inv_c
        imp = imp4[:, 0:QT]
        for r in range(1, Q_PER_KV):
            imp = imp + imp4[:, r * QT:(r + 1) * QT]
        importance.append(imp)

    def select(rows):
        jb = lax.broadcasted_iota(jnp.int32, (rows, QT), 0)
        tq = t0 + lax.broadcasted_iota(jnp.int32, (rows, QT), 1)
        cur = tq // SEL_BLOCK
        forced = (jb == 0) | (jb == cur) | (jb == cur - 1)
        future = jb * SEL_BLOCK > tq
        score = [jnp.where(forced, FORCE, jnp.where(future, -FORCE, imp[0:rows, :])) for imp in importance]
        chosen_any = [jnp.zeros((rows, QT), jnp.bool_) for _ in range(KV_HEADS)]
        for _ in range(min(N_SELECT, rows)):
            for g in range(KV_HEADS):
                best = jnp.max(score[g], axis=0, keepdims=True)
                first = jnp.min(jnp.where(score[g] == best, jb, rows), axis=0, keepdims=True)
                chosen = jb == first
                chosen_any[g] = chosen_any[g] | chosen
                score[g] = jnp.where(chosen, REMOVED, score[g])
        for g in range(KV_HEADS):
            q_scr[g, KV_W:KV_W + rows, :] = tile4(jnp.where(chosen_any[g], 0.0, NEG)).astype(BF16)
            if rows < LANES:
                q_scr[g, KV_W + rows:, :] = jnp.full((LANES - rows, lanes4), NEG, BF16)

    last_block = (t0 + QT - 1) // SEL_BLOCK
    sizes = [r for r in (32, 64) if r < nsel] + [nsel]
    lower = 0
    for r in sizes:
        @pl.when((last_block >= lower) & ((last_block < r) | (r == nsel)))
        def _(r=r):
            select(r)
        lower = r

    def values_of(ref, kt, g):
        return ref[0, kt, g * V_ROWS:(g + 1) * V_ROWS, :]

    def qk_pair(k, dst):
        for g in range(KV_HEADS):
            dst[g, 0:KT, :] = _dot(ksa_ref[0, 2 * k], q_scr[g])
            dst[g, KT:2 * KT, :] = _dot(ksa_ref[0, 2 * k + 1], q_scr[g])

    def softmax_pair(k, src):
        for g in range(KV_HEADS):
            update(g, [src[g, 0:KT, :], src[g, KT:2 * KT, :]],
                   [values_of(vst_ref, 2 * k, g), values_of(vst_ref, 2 * k + 1, g)])

    m_scr[...] = jnp.full(m_scr.shape, NEG, F32)
    acc_scr[...] = jnp.zeros(acc_scr.shape, F32)
    n_far = jnp.maximum(blk - 1, 0)
    npairs = n_far // 2

    @pl.when(npairs > 0)
    def _():
        qk_pair(0, sa_scr)

    def two_pairs(k0):
        qk_pair(jnp.minimum(k0 + 1, npairs - 1), sb_scr)
        softmax_pair(k0, sa_scr)
        qk_pair(jnp.minimum(k0 + 2, npairs - 1), sa_scr)
        softmax_pair(k0 + 1, sb_scr)

    def far_body4(j, carry):
        two_pairs(4 * j)
        two_pairs(4 * j + 2)
        return carry

    def far_body2(j, carry):
        two_pairs(4 * (npairs // 4) + 2 * j)
        return carry

    lax.fori_loop(0, npairs // 4, far_body4, 0)
    lax.fori_loop(0, (npairs % 4) // 2, far_body2, 0)

    @pl.when(npairs % 2 == 1)
    def _():
        softmax_pair(npairs - 1, sa_scr)

    def gate(valid):
        return jnp.where(valid, 0.0, NEG).astype(F32)

    anti = tile4(jnp.where(row_k > lane_q, 0.0, NEG))
    sel_tiles = ((blk - 2, lambda g: gate(n_far % 2 == 1)),
                 (blk - 1, lambda g: tsub_ref[g] + gate(blk >= 1)),
                 (blk, lambda g: tdiag_ref[g]))
    win_tiles = ((blk - 4, lambda g: anti + gate(blk >= 4)),
                 (blk - 3, lambda g: gate(blk >= 3)),
                 (blk - 2, lambda g: gate(blk >= 2)),
                 (blk - 1, lambda g: tsub_ref[g] + gate(blk >= 1)),
                 (blk, lambda g: tdiag_ref[g]))
    sel_scores, win_scores = [], []
    for g in range(KV_HEADS):
        sel_scores.append([_dot(ksa_ref[0, jnp.maximum(kt, 0)], q_scr[g]) + add(g) for kt, add in sel_tiles])
        win_scores.append([_dot(kw_ref[0, jnp.maximum(kt, 0)], q_scr[g, 0:KV_W, :]) + add(g)
                           for kt, add in win_tiles])
    o_win = []
    for g in range(KV_HEADS):
        update(g, sel_scores[g], [values_of(vst_ref, jnp.maximum(kt, 0), g) for kt, _ in sel_tiles])
        osel_scr[g] = normalise(acc_scr[g], m_scr[g])
        m_win = tiles_max(win_scores[g], jnp.full((1, lanes4), NEG, F32))
        acc_win = weighted_values(win_scores[g], m_win,
                                  [values_of(vwt_ref, jnp.maximum(kt, 0), g) for kt, _ in win_tiles])
        o_win.append(normalise(acc_win, m_win))

    gl = gt_ref[0, 0]
    for g in range(KV_HEADS):
        o_g = (_sigmoid(gl[3 * g:3 * g + 1, :]) * ocmp_scr[g] + _sigmoid(gl[3 * g + 1:3 * g + 2, :]) * osel_scr[g]
               + _sigmoid(gl[3 * g + 2:3 * g + 3, :]) * o_win[g])
        for r in range(Q_PER_KV):
            h = g * Q_PER_KV + r
            ot_scr[h * NSA_HD:(h + 1) * NSA_HD, :] = o_g[:, r * QT:(r + 1) * QT]

    o_ref[0] = ot_scr[...].T.astype(o_ref.dtype)


def nsa_prompt(proj, kv_bf, ckv, rel_tabs):
    bsz, seq, _ = proj.shape
    nq = seq // QT
    n_cmp = (seq - CMP_BLOCK) // CMP_STRIDE + 1
    nsel = seq // SEL_BLOCK
    ncp = -(-(NC_PAD_FRONT + seq // CMP_STRIDE) // KT) * KT
    tsub, tdiag, bct = rel_tabs

    def tiles(cols):
        c0 = cols - C_SKV
        return kv_bf[:, :, c0:c0 + KV_W].reshape(bsz, nq, KT, KV_W)

    assert nsel <= LANES
    blk_of_key = np.arange(seq) // SEL_BLOCK
    onehot = jnp.asarray(blk_of_key[:, None] == np.arange(LANES)[None, :], BF16).reshape(nq, KT, LANES)
    ksa = jnp.concatenate([tiles(C_SKV), jnp.broadcast_to(onehot, (bsz, nq, KT, LANES))], axis=-1)

    def values_t(cols):
        vt = tiles(cols).reshape(bsz, nq, KT, KV_HEADS, NSA_HD).transpose(0, 1, 3, 4, 2)
        ones = jnp.ones((bsz, nq, KV_HEADS, 1, KT), BF16)
        zeros = jnp.zeros((bsz, nq, KV_HEADS, V_ROWS - NSA_HD - 1, KT), BF16)
        return jnp.concatenate([vt, ones, zeros], axis=3).reshape(bsz, nq, KV_HEADS * V_ROWS, KT)

    vst = values_t(C_SKV + KV_W)
    kw = tiles(C_WKV)
    vwt = values_t(C_WKV + KV_W)
    back = ncp - NC_PAD_FRONT - ckv.shape[1]
    kc = jnp.pad(ckv[:, :, :KV_W], ((0, 0), (NC_PAD_FRONT, back), (0, 0))).astype(BF16)
    vct = jnp.pad(ckv[:, :, KV_W:], ((0, 0), (NC_PAD_FRONT, back), (0, 0))).astype(BF16).transpose(0, 2, 1)
    cov = _cover_t(nsel, n_cmp, ncp)
    ng = proj[:, :, C_NG:C_NG + 3 * NSA_HEADS].reshape(bsz, nq, QT, KV_HEADS, Q_PER_KV, 3)
    gt = ng.transpose(0, 1, 3, 5, 4, 2).reshape(bsz, nq, KV_HEADS * 3, Q_PER_KV * QT)
    gt = jnp.pad(gt, ((0, 0), (0, 0), (0, 2), (0, 0)))

    whole = lambda a: pl.BlockSpec((1,) + a.shape[1:], lambda b, i: (b,) + (0,) * (a.ndim - 1))
    const = lambda a: pl.BlockSpec(a.shape, lambda b, i: (0,) * a.ndim)
    lanes4 = Q_PER_KV * QT
    return pl.pallas_call(
        functools.partial(_nsa_prompt_kernel, nsel=nsel, ncp=ncp, ncv=n_cmp),
        grid=(bsz, nq),
        in_specs=[pl.BlockSpec((1, QT, NSA_W), lambda b, i: (b, i, C_NQ // NSA_W)),
                  pl.BlockSpec((1, 1, 8, lanes4), lambda b, i: (b, i, 0, 0)),
                  whole(kc), whole(vct), const(cov), const(bct),
                  whole(ksa), whole(vst), whole(kw), whole(vwt), const(tsub), const(tdiag)],
        out_specs=pl.BlockSpec((1, QT, NSA_W), lambda b, i: (b, i, 0)),
        out_shape=jax.ShapeDtypeStruct((bsz, seq, NSA_W), BF16),
        scratch_shapes=[pltpu.VMEM((ncp, lanes4), F32), pltpu.VMEM((ncp, lanes4), BF16),
                        pltpu.VMEM((KV_HEADS, KV_W + LANES, lanes4), BF16),
                        pltpu.VMEM((KV_HEADS, 2 * KT, lanes4), F32), pltpu.VMEM((KV_HEADS, 2 * KT, lanes4), F32),
                        pltpu.VMEM((KV_HEADS, 1, lanes4), F32), pltpu.VMEM((KV_HEADS, 1, lanes4), F32),
                        pltpu.VMEM((KV_HEADS, V_ROWS, lanes4), F32), pltpu.VMEM((KV_HEADS, NSA_HD, lanes4), F32),
                        pltpu.VMEM((KV_HEADS, NSA_HD, lanes4), F32), pltpu.VMEM((NSA_W, QT), F32)],
        compiler_params=_cparams(("parallel", "arbitrary")),
        name="nsa_prompt",
    )(proj, gt, kc, vct, cov, bct, ksa, vst, kw, vwt, tsub, tdiag)


S_ROWS = Q_PER_KV * KV_HEADS * 4
NEW_PAD = 16
SAMPLE_NB = 4


def _sample_tables(rel_bias, past, t_new, n_cmp, win_len):
    row = jnp.arange(S_ROWS)
    i = (row % t_new)[:, None]
    g = (row // t_new) % KV_HEADS
    r = row // (t_new * KV_HEADS)
    head = g * Q_PER_KV + r

    def pick(tab):
        return jnp.take_along_axis(tab, head[:, None, None], axis=2)[..., 0]

    n = jnp.arange(LANES)[None, :]
    bcs = jnp.where(n < n_cmp, pick(_delta_bias(rel_bias, past + i - (CMP_STRIDE * n + CMP_BLOCK - 1))), NEG)
    jj = jnp.arange(LANES)[None, :]
    last = pick(_delta_bias(rel_bias, LANES + i - jj))
    w_old = jnp.where(jj > i, 0.0, NEG)
    j2 = jnp.arange(NEW_PAD)[None, :]
    d_new = i - j2
    new = jnp.where((j2 < t_new) & (d_new >= 0), pick(_delta_bias(rel_bias, d_new)), NEG)
    return bcs, jnp.stack([last, w_old]), new


def _nsa_sample_kernel(*refs, nb, npages, past, t_new):
    pt_ref, q_ref, g_ref, kc_ref = refs[:4]
    page_refs = refs[4:4 + nb * npages]
    win_ref, news_ref, neww_ref, bcs_ref, tab_ref, tnew_ref, cov_ref, o_ref, s_scr = refs[4 + nb * npages:]
    del pt_ref
    elems = range(nb)
    qb = [(q_ref[e] * (NSA_HD ** -0.5)).astype(BF16) for e in elems]

    def softmax_pv(tiles_of):
        mx = []
        for e in elems:
            m = jnp.full((S_ROWS, 1), NEG, F32)
            for c, (score_fn, _, add_fn, width) in enumerate(tiles_of(e)):
                s = score_fn()
                if add_fn is not None:
                    s = s + add_fn()
                s_scr[e, :, c * LANES:c * LANES + width] = s
                m = jnp.maximum(m, s.max(axis=1, keepdims=True))
            mx.append(m)
        out = []
        for e in elems:
            lsum = jnp.zeros((S_ROWS, 1), F32)
            acc = jnp.zeros((S_ROWS, KV_W), F32)
            for c, (_, pv_fn, _, width) in enumerate(tiles_of(e)):
                p = jnp.exp(s_scr[e, :, c * LANES:c * LANES + width] - mx[e])
                lsum = lsum + p.sum(axis=1, keepdims=True)
                acc = acc + pv_fn(p.astype(BF16))
            out.append(acc / lsum)
        return out

    def cached_tile(e, tile_ref, lanes, add_fn):
        return (lambda: _dot(qb[e], tile_ref[:KV_W, lanes].astype(BF16)),
                lambda p: _dot_nt(p, tile_ref[KV_W:, lanes].astype(BF16)), add_fn, LANES)

    def new_tile(e, ref, add_fn):
        return (lambda: _dot_nt(qb[e], ref[e, :, :KV_W].astype(BF16)),
                lambda p: _dot(p, ref[e, :, KV_W:].astype(BF16)), add_fn, NEW_PAD)

    rows_gi = KV_HEADS * t_new
    o_cmp, imp = [], []
    for e in elems:
        kc = kc_ref[e]
        s_c = _dot_nt(qb[e], kc[:, :KV_W].astype(BF16)) + bcs_ref[...]
        p_c = jnp.exp(s_c - s_c.max(axis=1, keepdims=True))
        p_c = (p_c / p_c.sum(axis=1, keepdims=True)).astype(BF16)
        o_cmp.append(_dot(p_c, kc[:, KV_W:].astype(BF16)))
        imp_r = _dot(p_c, cov_ref[...])
        tot = imp_r
        for r in range(1, Q_PER_KV):
            tot = tot + pltpu.roll(imp_r, r * rows_gi, axis=0)
        imp.append(tot)

    jb = lax.broadcasted_iota(jnp.int32, (S_ROWS, LANES), 1)
    qpos = past + lax.broadcasted_iota(jnp.int32, (S_ROWS, LANES), 0) % t_new
    cur = qpos // SEL_BLOCK
    forced = (jb == 0) | (jb == cur) | (jb == cur - 1)
    future = jb * SEL_BLOCK > qpos
    score = [jnp.where(forced, FORCE, jnp.where(future, -FORCE, imp[e])) for e in elems]
    chosen_any = [jnp.zeros((S_ROWS, LANES), jnp.bool_) for _ in elems]
    for _ in range(N_SELECT):
        for e in elems:
            best = jnp.max(score[e], axis=1, keepdims=True)
            first = jnp.min(jnp.where(score[e] == best, jb, LANES), axis=1, keepdims=True)
            chosen = jb == first
            chosen_any[e] = chosen_any[e] | chosen
            score[e] = jnp.where(chosen, REMOVED, score[e])
    sel = [jnp.where(chosen_any[e], 1.0, 0.0).astype(BF16) for e in elems]

    def block_mask(e, j0, nkeys=LANES):
        jrow = lax.broadcasted_iota(jnp.int32, (LANES, nkeys), 0)
        kcol = lax.broadcasted_iota(jnp.int32, (LANES, nkeys), 1)
        expand = jnp.where(jrow == j0 + kcol // SEL_BLOCK, 1.0, 0.0).astype(BF16)
        return jnp.where(_dot(sel[e], expand) > 0.5, 0.0, NEG)

    jn = past // SEL_BLOCK

    def sel_tiles(e):
        tiles = []
        for p in range(npages):
            if p == npages - 1:
                add = lambda p=p: block_mask(e, 2 * p) + tab_ref[0]
            else:
                add = lambda p=p: block_mask(e, 2 * p)
            tiles.append(cached_tile(e, page_refs[e * npages + p].at[0, 0], slice(None), add))
        tiles.append(new_tile(e, news_ref, lambda: tnew_ref[...] + block_mask(e, jn, NEW_PAD)))
        return tiles

    o_sel = softmax_pv(sel_tiles)

    nwin = win_ref.shape[3] // LANES

    def win_tiles(e):
        tiles = []
        for c in range(nwin):
            if c == 0:
                add = lambda: tab_ref[1]
            elif c == nwin - 1:
                add = lambda: tab_ref[0]
            else:
                add = None
            tiles.append(cached_tile(e, win_ref.at[0, e], slice(c * LANES, (c + 1) * LANES), add))
        tiles.append(new_tile(e, neww_ref, lambda: tnew_ref[...]))
        return tiles

    o_win = softmax_pv(win_tiles)

    for e in elems:
        gl = g_ref[e]
        o_ref[e] = (_sigmoid(gl[:, 0:1]) * o_cmp[e] + _sigmoid(gl[:, 1:2]) * o_sel[e]
                    + _sigmoid(gl[:, 2:3]) * o_win[e])


def nsa_sample(proj, ckv, pool_s, win_buf, layer, page_table, rel_bias):
    bsz, t_new, _ = proj.shape
    npages = page_table.shape[1]
    past = npages * PAGE
    n_cmp = (past + t_new - CMP_BLOCK) // CMP_STRIDE + 1
    n_sel = -(-(past + t_new) // SEL_BLOCK)
    assert n_cmp <= LANES and n_sel <= LANES and win_buf.shape[3] == WINDOW and t_new == 4
    assert (past + t_new - 1) // SEL_BLOCK == past // SEL_BLOCK
    bcs, tab, tnew = _sample_tables(rel_bias, past, t_new, n_cmp, win_buf.shape[3])
    n = np.arange(LANES)
    jsel = np.arange(LANES)
    cov = ((n[:, None] * CMP_STRIDE < jsel[None, :] * SEL_BLOCK + SEL_BLOCK)
           & (n[:, None] * CMP_STRIDE + CMP_BLOCK > jsel[None, :] * SEL_BLOCK)
           & (n[:, None] < n_cmp) & (jsel[None, :] < n_sel))
    cov = jnp.asarray(cov, BF16)

    eye = jnp.eye(KV_HEADS, dtype=F32)
    q5 = proj[:, :, C_NQ:C_NQ + NSA_W].reshape(bsz, t_new, KV_HEADS, Q_PER_KV, NSA_HD).transpose(0, 3, 2, 1, 4)
    qpad = (q5[:, :, :, :, None, :] * eye[None, None, :, None, :, None]).reshape(bsz, S_ROWS, KV_W)
    ng = proj[:, :, C_NG:C_NG + 3 * NSA_HEADS].reshape(bsz, t_new, KV_HEADS, Q_PER_KV, 3).transpose(0, 3, 2, 1, 4)
    gl = jnp.pad(ng.reshape(bsz, S_ROWS, 3), ((0, 0), (0, 0), (0, LANES - 3)))
    pad_new = lambda c: jnp.pad(proj[:, :, c:c + 2 * KV_W], ((0, 0), (0, NEW_PAD - t_new), (0, 0)))

    nb = SAMPLE_NB
    assert bsz % nb == 0
    per_b = lambda a: pl.BlockSpec((nb,) + a.shape[1:], lambda b, pt: (b,) + (0,) * (a.ndim - 1))
    const = lambda a: pl.BlockSpec(a.shape, lambda b, pt: (0,) * a.ndim)
    new_s, new_w = pad_new(C_SKV), pad_new(C_WKV)
    grid_spec = pltpu.PrefetchScalarGridSpec(
        num_scalar_prefetch=1,
        grid=(bsz // nb,),
        in_specs=[per_b(qpad), per_b(gl), per_b(ckv)]
        + [pl.BlockSpec((1, 1, 2 * KV_W, PAGE), (lambda b, pt, e=e, p=p: (layer, pt[b * nb + e, p], 0, 0)))
           for e in range(nb) for p in range(npages)]
        + [pl.BlockSpec((1, nb) + win_buf.shape[2:], lambda b, pt: (layer, b, 0, 0)),
           per_b(new_s), per_b(new_w), const(bcs), const(tab), const(tnew), const(cov)],
        out_specs=pl.BlockSpec((nb, S_ROWS, KV_W), lambda b, pt: (b, 0, 0)),
        scratch_shapes=[pltpu.VMEM((nb, S_ROWS, (npages + 1) * LANES), F32)],
    )
    out = pl.pallas_call(
        functools.partial(_nsa_sample_kernel, nb=nb, npages=npages, past=past, t_new=t_new),
        grid_spec=grid_spec,
        out_shape=jax.ShapeDtypeStruct((bsz, S_ROWS, KV_W), F32),
        compiler_params=_cparams(("parallel",)),
        name="nsa_sample",
    )(page_table, qpad, gl, ckv, *([pool_s] * (nb * npages)), win_buf, new_s, new_w, bcs, tab, tnew, cov)
    o6 = out.reshape(bsz, Q_PER_KV, KV_HEADS, t_new, KV_HEADS, NSA_HD)
    o5 = jnp.stack([o6[:, :, g, :, g, :] for g in range(KV_HEADS)], axis=2)
    return o5.transpose(0, 3, 2, 1, 4).reshape(bsz, t_new, NSA_W)


WIN_NB = 8


def _win_shift_kernel(win_ref, new_ref, o_ref, *, t_new):
    w = win_ref.shape[-1]
    lane = lax.broadcasted_iota(jnp.int32, (win_ref.shape[-2], LANES), 1)
    for e in range(win_ref.shape[1]):
        y = pltpu.roll(win_ref[0, e], w - t_new, axis=1)
        o_ref[0, e, :, :w - LANES] = y[:, :w - LANES]
        o_ref[0, e, :, w - LANES:] = jnp.where(lane < LANES - t_new, y[:, w - LANES:], new_ref[0, e])


def window_cache_update(win_t, new_rows):
    depth, bsz, feat, w = win_t.shape
    t_new = new_rows.shape[2]
    assert t_new <= LANES <= w and bsz % WIN_NB == 0
    new_t = jnp.pad(new_rows.transpose(0, 1, 3, 2), ((0, 0), (0, 0), (0, 0), (LANES - t_new, 0)))
    return pl.pallas_call(
        functools.partial(_win_shift_kernel, t_new=t_new),
        grid=(depth, bsz // WIN_NB),
        in_specs=[pl.BlockSpec((1, WIN_NB, feat, w), lambda l, b: (l, b, 0, 0)),
                  pl.BlockSpec((1, WIN_NB, feat, LANES), lambda l, b: (l, b, 0, 0))],
        out_specs=pl.BlockSpec((1, WIN_NB, feat, w), lambda l, b: (l, b, 0, 0)),
        out_shape=jax.ShapeDtypeStruct(win_t.shape, win_t.dtype),
        compiler_params=_cparams(("parallel", "parallel")),
        name="window_cache_update",
    )(win_t, new_t)


def _in_proj_weight(w):
    g0 = 4 * ML_W
    g1 = g0 + 2 * ML_HEADS
    n_kv = C_GATE
    assert w.shape[1] == n_kv + 2 * ML_HEADS + 3 * NSA_HEADS
    tail = w.shape[1] - (g1 - g0) - n_kv
    pieces = [w[:, :g0], w[:, g1:g1 + n_kv - g0], w[:, g0:g1], w[:, w.shape[1] - tail:],
              jnp.zeros((w.shape[0], N_IN_PAD - w.shape[1]), w.dtype)]
    return jnp.concatenate(pieces, axis=1).astype(BF16)


def _layer(x, mods, rows_per_group, lw, tm, tm_ffn, mixer):
    sh1, sc1, g1, sh2, sc2, g2 = mods
    w_in_bf, w_out_bf, w_up_bf, w_down_bf, ln_g, ln_b = lw
    proj, kv_bf = in_proj(x, sc1, sh1, w_in_bf, tm, rows_per_group // tm)
    h_ml, o_nsa, extras = mixer(proj, kv_bf)
    x1 = out_proj_ln(x, h_ml, o_nsa, w_out_bf, g1, ln_g[0], ln_b[0], tm, rows_per_group // tm)
    x2 = ffn_ln(x1, sc2, sh2, g2, w_up_bf, w_down_bf, ln_g[1], ln_b[1], tm_ffn, rows_per_group // tm_ffn)
    return x2, proj, extras


def kernel(x_prompt, x_sample, cache_cmp_kv, cache_slc_kv, cache_win_kv, state_mlstm_C, state_mlstm_n,
           state_mlstm_m, page_table, c_prompt, c_sample, w_ada, b_ada, w_in, b_gate, ml_norm_g, cmp_pe,
           cmp_w1, cmp_w2, rel_bias, w_out, ln_g, ln_b, w_up, w_down):
    bsz, seq, d = x_prompt.shape
    bs, ts, _ = x_sample.shape
    depth = w_in.shape[0]
    n_phys = cache_cmp_kv.shape[1]
    npages = page_table.shape[1]
    win_len = cache_win_kv.shape[2]
    tm = 512
    tm_ffn = 1024
    t_ml = 256
    t_pad = 16

    rows_last = lambda a: a.transpose(0, 1, 3, 4, 5, 2).reshape(a.shape[0], a.shape[1], 2 * KV_W, a.shape[2])
    pool_c, pool_s, win_t = rows_last(cache_cmp_kv), rows_last(cache_slc_kv), rows_last(cache_win_kv)

    prompt_tabs = _prompt_tables(rel_bias)
    nc_rows = bsz + bs
    c_all = jnp.pad(jnp.concatenate([c_prompt, c_sample], axis=0), ((0, -nc_rows % 8), (0, 0)))

    xp = x_prompt.reshape(bsz * seq, d)
    xs = x_sample.reshape(bs * ts, d)
    outs = {k: [] for k in ("cmp_p", "cmp_s", "slc_p", "slc_s", "win_p", "win_s",
                            "C_p", "C_s", "n_p", "n_s", "m_p", "m_s")}
    kv_shape = lambda b, t: (b, t, 2, KV_HEADS, NSA_HD)

    for l in range(depth):
        ada = ada_mod(c_all, w_ada[l].astype(BF16), b_ada[l]).reshape(c_all.shape[0], 6, d)
        mods_p = [ada[:bsz, k][:, None, :] for k in range(6)]
        mods_s = [jnp.repeat(ada[bsz:nc_rows, k], ts, axis=0)[None] for k in range(6)]
        w_in_bf = _in_proj_weight(w_in[l])
        lw = (w_in_bf, w_out[l].astype(BF16), w_up[l].astype(BF16), w_down[l].astype(BF16), ln_g[l], ln_b[l])
        cw = _compress_weights(cmp_pe[l], cmp_w1[l], cmp_w2[l])

        def prompt_mixer(proj, kv_bf):
            proj = proj.reshape(bsz, seq, N_IN_PAD)
            h_ml, c1, n1, m1 = mlstm(proj, b_gate[l], ml_norm_g[l],
                                     jnp.zeros((bsz, ML_HEADS, ML_HD, ML_HD), F32),
                                     jnp.zeros((bsz, ML_HEADS, ML_HD), F32), jnp.zeros((bsz, ML_HEADS), F32),
                                     nb=bsz, t=t_ml, valid=t_ml)
            ckv = compress(proj, [lambda s: (lambda b: (b, 0, C_CKV // KV_W + s))], 0, (), bsz, 1, seq, cw)
            o_nsa = nsa_prompt(proj, kv_bf.reshape(bsz, seq, 4 * KV_W), ckv, prompt_tabs)
            return h_ml.reshape(bsz * seq, ML_W), o_nsa.reshape(bsz * seq, NSA_W), (c1, n1, m1)

        def sample_mixer(proj, kv_bf):
            del kv_bf
            proj = proj.reshape(bs, ts, N_IN_PAD)
            proj_pad = jnp.pad(proj, ((0, 0), (0, t_pad - ts), (0, 0)))
            h_ml, c1, n1, m1 = mlstm(proj_pad, b_gate[l], ml_norm_g[l], state_mlstm_C, state_mlstm_n,
                                     state_mlstm_m, nb=8, t=t_pad, valid=ts, layer=l)
            pages = [(lambda b, pt, e=e, p=p: (l, pt[b * SAMPLE_NB + e, p], 0, 0))
                     for e in range(SAMPLE_NB) for p in range(npages)]
            ckv = compress(pool_c, pages, 1, (page_table,), bs // SAMPLE_NB, SAMPLE_NB * npages, PAGE, cw,
                           transposed=True, nb=SAMPLE_NB)
            o_nsa = nsa_sample(proj, ckv, pool_s, win_t, l, page_table, rel_bias)
            return (h_ml[:, :ts].reshape(bs * ts, ML_W), o_nsa.reshape(bs * ts, NSA_W).astype(BF16),
                    (c1, n1, m1))

        xp, proj_p, st_p = _layer(xp, mods_p, seq, lw, tm, tm_ffn, prompt_mixer)
        xs, proj_s, st_s = _layer(xs, mods_s, bs * ts, lw, bs * ts, bs * ts, sample_mixer)

        proj_p = proj_p.reshape(bsz, seq, N_IN_PAD)
        proj_s = proj_s.reshape(bs, ts, N_IN_PAD)
        rows = lambda proj, c: proj[:, :, c:c + 2 * KV_W]
        win = min(WINDOW, seq)
        outs["cmp_p"].append(rows(proj_p, C_CKV).reshape(kv_shape(bsz, seq)))
        outs["slc_p"].append(rows(proj_p, C_SKV).reshape(kv_shape(bsz, seq)))
        outs["win_p"].append(rows(proj_p, C_WKV)[:, seq - win:].reshape(kv_shape(bsz, win)))
        outs["cmp_s"].append(rows(proj_s, C_CKV).reshape(kv_shape(bs, ts)))
        outs["slc_s"].append(rows(proj_s, C_SKV).reshape(kv_shape(bs, ts)))
        outs["win_s"].append(rows(proj_s, C_WKV).reshape(kv_shape(bs, ts)))
        for tag, st in (("p", st_p), ("s", st_s)):
            outs["C_" + tag].append(st[0])
            outs["n_" + tag].append(st[1])
            outs["m_" + tag].append(st[2])

    stk = lambda k: jnp.stack(outs[k])
    assert ts <= win_len
    win_s = window_cache_update(win_t, stk("win_s").reshape(depth, bs, ts, 2 * KV_W))
    win_s = win_s.reshape(depth, bs, 2, KV_HEADS, NSA_HD, win_len).transpose(0, 1, 5, 2, 3, 4)
    return (xp.reshape(bsz, seq, d), xs.reshape(bs, ts, d),
            stk("cmp_p"), stk("cmp_s"), stk("slc_p"), stk("slc_s"), stk("win_p"), win_s,
            stk("C_p"), stk("C_s"), stk("n_p"), stk("n_s"), stk("m_p"), stk("m_s"))
```

```python
import functools
import math

import numpy as np
import jax
import jax.numpy as jnp
from jax import lax
from jax.experimental import pallas as pl
from jax.experimental.pallas import tpu as pltpu

F32 = jnp.float32
BF16 = jnp.bfloat16

D_MODEL = 1024
DEPTH = 2
PAGE = 128
ML_HEADS = 4
ML_HD = 128
ML_W = ML_HEADS * ML_HD
NSA_HEADS = 8
NSA_HD = 64
NSA_W = NSA_HEADS * NSA_HD
KV_HEADS = 2
Q_PER_KV = 4
KV_W = KV_HEADS * NSA_HD
CMP_BLOCK = 32
CMP_STRIDE = 16
CMP_HIDDEN = 256
SEL_BLOCK = 64
N_SELECT = 16
WINDOW = 512
NUM_BUCKETS = 32
REL_MAX_DIST = 128
D_FF = 4 * D_MODEL
ALPHA = (2 * DEPTH) ** 0.25
LN_EPS = 1e-5
NEG = -1e30
FORCE = 1e9
REMOVED = -3e38
LOG2E = math.log2(math.e)

LANES = 128
QT = 128
KT = 128
V_ROWS = 80
NC_PAD_FRONT = 16
NEAR_ROWS = 24

C_MQ, C_MK, C_MV, C_MO = 0, 512, 1024, 1536
C_NQ = 2048
C_CKV, C_SKV, C_WKV = 2560, 2816, 3072
C_GATE = 3328
C_NG = 3336
N_IN_PAD = 3456
VMEM_LIMIT = 56 * 1024 * 1024


def _cparams(sem):
    return pltpu.CompilerParams(dimension_semantics=sem, vmem_limit_bytes=VMEM_LIMIT)


def _dot(a, b):
    return jnp.dot(a, b, preferred_element_type=F32)


def _dot_nt(a, b):
    return lax.dot_general(a, b, (((1,), (1,)), ((), ())), preferred_element_type=F32)


def _dot_tn(a, b):
    return lax.dot_general(a, b, (((0,), (0,)), ((), ())), preferred_element_type=F32)


def _sigmoid(x):
    return 1.0 / (1.0 + jnp.exp(-x))


def _layer_norm(y, g, b):
    mu = jnp.mean(y, axis=-1, keepdims=True)
    d = y - mu
    var = jnp.mean(d * d, axis=-1, keepdims=True)
    return d * lax.rsqrt(var + LN_EPS) * g + b


def _ada_kernel(c_ref, w_ref, b_ref, o_ref):
    c = c_ref[...]
    a = (c * _sigmoid(c)).astype(BF16)
    o_ref[...] = _dot(a, w_ref[...]) + b_ref[...]


def ada_mod(c, w_bf, b):
    m, d = c.shape
    n = w_bf.shape[1]
    tn = 1536
    return pl.pallas_call(
        _ada_kernel,
        grid=(n // tn,),
        in_specs=[pl.BlockSpec((m, d), lambda j: (0, 0)),
                  pl.BlockSpec((d, tn), lambda j: (0, j)),
                  pl.BlockSpec((1, tn), lambda j: (0, j))],
        out_specs=pl.BlockSpec((m, tn), lambda j: (0, j)),
        out_shape=jax.ShapeDtypeStruct((m, n), F32),
        compiler_params=_cparams(("arbitrary",)),
        name="ada_mod",
    )(c, w_bf, b.reshape(1, n))


def _inproj_kernel(x_ref, sc_ref, sh_ref, w_ref, o_ref, kv_ref):
    u = (x_ref[...] * (1.0 + sc_ref[0]) + sh_ref[0]).astype(BF16)
    y = _dot(u, w_ref[...])
    o_ref[...] = y
    kv_ref[...] = y[:, C_SKV:C_WKV + 2 * KV_W].astype(BF16)


def in_proj(x, sc, sh, w_bf, tm, tiles_per_group):
    m = x.shape[0]
    n = w_bf.shape[1]
    r = sc.shape[1]
    mod = pl.BlockSpec((1, r, D_MODEL), lambda i: (i // tiles_per_group, 0, 0))
    return pl.pallas_call(
        _inproj_kernel,
        grid=(m // tm,),
        in_specs=[pl.BlockSpec((tm, D_MODEL), lambda i: (i, 0)), mod, mod,
                  pl.BlockSpec((D_MODEL, n), lambda i: (0, 0))],
        out_specs=[pl.BlockSpec((tm, n), lambda i: (i, 0)), pl.BlockSpec((tm, 4 * KV_W), lambda i: (i, 0))],
        out_shape=[jax.ShapeDtypeStruct((m, n), F32), jax.ShapeDtypeStruct((m, 4 * KV_W), BF16)],
        compiler_params=_cparams(("parallel",)),
        name="in_proj",
    )(x, sc, sh, w_bf)


def _outproj_kernel(x_ref, a1_ref, a2_ref, w1_ref, w2_ref, g_ref, lg_ref, lb_ref, o_ref):
    mixed = _dot(a1_ref[...], w1_ref[...]) + _dot(a2_ref[...], w2_ref[...])
    y = ALPHA * x_ref[...] + g_ref[0] * mixed
    o_ref[...] = _layer_norm(y, lg_ref[...], lb_ref[...])


def out_proj_ln(x, a1, a2, w_bf, gate, ln_g, ln_b, tm, tiles_per_group):
    m = x.shape[0]
    k1 = a1.shape[1]
    r = gate.shape[1]
    return pl.pallas_call(
        _outproj_kernel,
        grid=(m // tm,),
        in_specs=[pl.BlockSpec((tm, D_MODEL), lambda i: (i, 0)),
                  pl.BlockSpec((tm, k1), lambda i: (i, 0)),
                  pl.BlockSpec((tm, k1), lambda i: (i, 0)),
                  pl.BlockSpec((k1, D_MODEL), lambda i: (0, 0)),
                  pl.BlockSpec((k1, D_MODEL), lambda i: (1, 0)),
                  pl.BlockSpec((1, r, D_MODEL), lambda i: (i // tiles_per_group, 0, 0)),
                  pl.BlockSpec((1, D_MODEL), lambda i: (0, 0)),
                  pl.BlockSpec((1, D_MODEL), lambda i: (0, 0))],
        out_specs=pl.BlockSpec((tm, D_MODEL), lambda i: (i, 0)),
        out_shape=jax.ShapeDtypeStruct((m, D_MODEL), F32),
        compiler_params=_cparams(("parallel",)),
        name="out_proj_ln",
    )(x, a1, a2, w_bf, w_bf, gate, ln_g.reshape(1, -1), ln_b.reshape(1, -1))


def _ffn_kernel(x_ref, sc_ref, sh_ref, g_ref, wu_ref, wd_ref, lg_ref, lb_ref, o_ref, u_scr, acc_scr):
    f = pl.program_id(1)

    @pl.when(f == 0)
    def _():
        u_scr[...] = (x_ref[...] * (1.0 + sc_ref[0]) + sh_ref[0]).astype(BF16)
        acc_scr[...] = jnp.zeros_like(acc_scr)

    h = jnp.maximum(_dot(u_scr[...], wu_ref[...]), 0.0)
    acc_scr[...] += _dot((h * h).astype(BF16), wd_ref[...])

    @pl.when(f == pl.num_programs(1) - 1)
    def _():
        y = ALPHA * x_ref[...] + g_ref[0] * acc_scr[...]
        o_ref[...] = _layer_norm(y, lg_ref[...], lb_ref[...])


def ffn_ln(x, sc, sh, gate, wu_bf, wd_bf, ln_g, ln_b, tm, tiles_per_group):
    m = x.shape[0]
    tf = 1024
    r = sc.shape[1]
    mod = pl.BlockSpec((1, r, D_MODEL), lambda i, f: (i // tiles_per_group, 0, 0))
    return pl.pallas_call(
        _ffn_kernel,
        grid=(m // tm, D_FF // tf),
        in_specs=[pl.BlockSpec((tm, D_MODEL), lambda i, f: (i, 0)),
                  mod, mod, mod,
                  pl.BlockSpec((D_MODEL, tf), lambda i, f: (0, f)),
                  pl.BlockSpec((tf, D_MODEL), lambda i, f: (f, 0)),
                  pl.BlockSpec((1, D_MODEL), lambda i, f: (0, 0)),
                  pl.BlockSpec((1, D_MODEL), lambda i, f: (0, 0))],
        out_specs=pl.BlockSpec((tm, D_MODEL), lambda i, f: (i, 0)),
        out_shape=jax.ShapeDtypeStruct((m, D_MODEL), F32),
        scratch_shapes=[pltpu.VMEM((tm, D_MODEL), BF16), pltpu.VMEM((tm, D_MODEL), F32)],
        compiler_params=_cparams(("parallel", "arbitrary")),
        name="ffn_ln",
    )(x, sc, sh, gate, wu_bf, wd_bf, ln_g.reshape(1, -1), ln_b.reshape(1, -1))


def _mlstm_kernel(bg_ref, q_ref, k_ref, v_ref, o_ref, g_ref, ng_ref, c0_ref, n0_ref, m0_ref,
                  h_ref, c_out, n_out, m_out, c_scr, n_scr, m_scr, *, nb, t, valid):
    ci = pl.program_id(1)

    @pl.when(ci == 0)
    def _():
        c_scr[...] = c0_ref[...]
        n_scr[...] = n0_ref[...]
        m_scr[...] = m0_ref[...]

    row = lax.broadcasted_iota(jnp.int32, (t, t), 0)
    col = lax.broadcasted_iota(jnp.int32, (t, t), 1)
    tri = col <= row
    tri_t = row <= col
    eye = row == col
    row1 = lax.broadcasted_iota(jnp.int32, (t, 1), 0)
    scale = ML_HD ** -0.5

    def gate_scans(b, h):
        gates = g_ref[b]
        ig_col = gates[:, h:h + 1] + bg_ref[h]
        fr = gates[:, ML_HEADS + h:ML_HEADS + h + 1] + bg_ref[ML_HEADS + h]
        lf_col = jnp.minimum(fr, 0.0) - jnp.log1p(jnp.exp(-jnp.abs(fr)))
        if valid < t:
            ig_col = jnp.where(row1 < valid, ig_col, NEG)
            lf_col = jnp.where(row1 < valid, lf_col, 0.0)
        lf_row = jnp.sum(jnp.where(eye, lf_col, 0.0), axis=0, keepdims=True)
        ig_row = jnp.sum(jnp.where(eye, ig_col, 0.0), axis=0, keepdims=True)
        f_col = jnp.sum(jnp.where(tri, lf_row, 0.0), axis=1, keepdims=True)
        f_row = jnp.sum(jnp.where(tri_t, lf_col, 0.0), axis=0, keepdims=True)
        a_row = ig_row - f_row
        a_col = ig_col - f_col
        m0 = m_scr[b, h]
        cm_col = jnp.max(jnp.where(tri, a_row, NEG), axis=1, keepdims=True)
        g_col = jnp.maximum(m0, cm_col)
        g_end = jnp.maximum(m0, jnp.max(a_row, axis=1, keepdims=True))
        return dict(dmat=jnp.exp(jnp.where(tri, a_row - g_col, NEG)), decay_col=jnp.exp(m0 - g_col),
                    m_col=f_col + g_col, m_end=jnp.sum(lf_row, axis=1, keepdims=True) + g_end,
                    w_end_col=jnp.exp(a_col - g_end), carry=jnp.exp(m0 - g_end))

    streams = [(b, h) for b in range(nb) for h in range(ML_HEADS)]
    group = 2 * ML_HEADS
    for g0 in range(0, len(streams), group):
        grp = streams[g0:g0 + group]
        hs = [slice(h * ML_HD, (h + 1) * ML_HD) for _, h in grp]
        sc = [gate_scans(b, h) for b, h in grp]
        qf = [q_ref[b][:, s] for (b, _), s in zip(grp, hs)]
        qb = [x.astype(BF16) for x in qf]
        kf = [k_ref[b][:, s] * scale for (b, _), s in zip(grp, hs)]
        kb = [x.astype(BF16) for x in kf]
        vb = [v_ref[b][:, s].astype(BF16) for (b, _), s in zip(grp, hs)]
        cmat = [c_scr[b, h] for b, h in grp]
        nvec = [n_scr[b, h] for b, h in grp]
        n_st = range(len(grp))

        w = [_dot_nt(qb[i], kb[i]) * sc[i]["dmat"] for i in n_st]
        qc = [_dot(qb[i], cmat[i].astype(BF16)) for i in n_st]
        num = [_dot(w[i].astype(BF16), vb[i]) + qc[i] * sc[i]["decay_col"] for i in n_st]
        den = [jnp.sum(w[i], axis=1, keepdims=True)
               + jnp.sum(qf[i] * nvec[i], axis=1, keepdims=True) * sc[i]["decay_col"] for i in n_st]
        hh = [num[i] / jnp.maximum(jnp.abs(den[i]), jnp.exp(-sc[i]["m_col"])) for i in n_st]
        mu = [jnp.mean(x, axis=1, keepdims=True) for x in hh]
        dd = [hh[i] - mu[i] for i in n_st]
        var = [jnp.mean(x * x, axis=1, keepdims=True) for x in dd]
        for i, (b, h) in enumerate(grp):
            hn = dd[i] * lax.rsqrt(var[i] + LN_EPS) * ng_ref[:, hs[i]] * _sigmoid(o_ref[b][:, hs[i]])
            h_ref[b, :, hs[i]] = hn.astype(h_ref.dtype)

        kw = [kf[i] * sc[i]["w_end_col"] for i in n_st]
        upd = [_dot_tn(kw[i].astype(BF16), vb[i]) for i in n_st]
        for i, (b, h) in enumerate(grp):
            c_scr[b, h] = sc[i]["carry"] * cmat[i] + upd[i]
            n_scr[b, h] = sc[i]["carry"] * nvec[i] + jnp.sum(kw[i], axis=0, keepdims=True)
            m_scr[b, h] = sc[i]["m_end"]

    @pl.when(ci == pl.num_programs(1) - 1)
    def _():
        c_out[...] = c_scr[...]
        n_out[...] = n_scr[...]
        m_out[...] = m_scr[...]


def mlstm(proj, b_gate, norm_g, c0, n0, m0, *, nb, t, valid, layer=None):
    bsz, length, _ = proj.shape
    nchunk = length // t
    wide = lambda cb: pl.BlockSpec((nb, t, ML_W), lambda i, c: (i, c, cb))
    state4 = lambda s: pl.BlockSpec((nb,) + s, lambda i, c: (i, 0, 0, 0))
    if layer is None:
        state_in = state4
        lead = (bsz,)
    else:
        state_in = lambda s: pl.BlockSpec((None, nb) + s, lambda i, c: (layer, i, 0, 0, 0))
        lead = (c0.shape[0], bsz)
    kern = functools.partial(_mlstm_kernel, nb=nb, t=t, valid=valid)
    h, c1, n1, m1 = pl.pallas_call(
        kern,
        grid=(bsz // nb, nchunk),
        in_specs=[pl.BlockSpec(memory_space=pltpu.SMEM),
                  wide(C_MQ // ML_W), wide(C_MK // ML_W), wide(C_MV // ML_W), wide(C_MO // ML_W),
                  pl.BlockSpec((nb, t, LANES), lambda i, c: (i, c, C_GATE // LANES)),
                  pl.BlockSpec((1, ML_W), lambda i, c: (0, 0)),
                  state_in((ML_HEADS, ML_HD, ML_HD)), state_in((ML_HEADS, 1, ML_HD)), state_in((ML_HEADS, 1, 1))],
        out_specs=[pl.BlockSpec((nb, t, ML_W), lambda i, c: (i, c, 0)),
                   state4((ML_HEADS, ML_HD, ML_HD)), state4((ML_HEADS, 1, ML_HD)), state4((ML_HEADS, 1, 1))],
        out_shape=[jax.ShapeDtypeStruct((bsz, length, ML_W), BF16),
                   jax.ShapeDtypeStruct((bsz, ML_HEADS, ML_HD, ML_HD), F32),
                   jax.ShapeDtypeStruct((bsz, ML_HEADS, 1, ML_HD), F32),
                   jax.ShapeDtypeStruct((bsz, ML_HEADS, 1, 1), F32)],
        scratch_shapes=[pltpu.VMEM((nb, ML_HEADS, ML_HD, ML_HD), F32),
                        pltpu.VMEM((nb, ML_HEADS, 1, ML_HD), F32),
                        pltpu.VMEM((nb, ML_HEADS, 1, 1), F32)],
        compiler_params=_cparams(("parallel", "arbitrary")),
        name="mlstm",
    )(b_gate, proj, proj, proj, proj, proj, norm_g.reshape(1, ML_W),
      c0, n0.reshape(lead + (ML_HEADS, 1, ML_HD)), m0.reshape(lead + (ML_HEADS, 1, 1)))
    return h, c1, n1.reshape(bsz, ML_HEADS, ML_HD), m1.reshape(bsz, ML_HEADS)


def _gelu_tanh(x):
    return 0.5 * x * (1.0 + jnp.tanh(math.sqrt(2.0 / math.pi) * (x + 0.044715 * (x * x * x))))


def _compress_kernel(*refs, npages, rows, transposed, nb):
    jp = rows // CMP_STRIDE
    j = npages * jp
    lane = lax.broadcasted_iota(jnp.int32, (j, LANES), 1)
    low = lane < NSA_HD
    if transposed:
        pe_ref, w1_ref, w2_ref, o_ref, xk_scr, xv_scr = refs[-6:]
        page_refs = refs[-6 - npages:-6]
        for p in range(npages):
            x = page_refs[p][0, 0].T
            xk_scr[p * rows:(p + 1) * rows, :] = x[:, :KV_W]
            xv_scr[p * rows:(p + 1) * rows, :] = x[:, KV_W:]

        def chunk_phase(c, s):
            return (xk_scr, xv_scr)[s][pl.ds(c, j, stride=CMP_STRIDE), :]
    else:
        pe_ref, w1_ref, w2_ref, o_ref = refs[-4:]
        page_refs = refs[-4 - 2 * npages:-4]

        def chunk_phase(c, s):
            parts = [page_refs[2 * p + s][0, pl.ds(c, jp, stride=CMP_STRIDE), :] for p in range(npages)]
            return parts[0] if npages == 1 else jnp.concatenate(parts, axis=0)

    out = jnp.zeros((j, 2 * KV_HEADS * NSA_HD), F32)
    for s in range(2):
        halves = [[], []]
        for c in range(0, CMP_STRIDE, 2):
            va = chunk_phase(c, s)
            vb = chunk_phase(c + 1, s)
            halves[0].append(jnp.where(low, va, pltpu.roll(vb, NSA_HD, axis=1)).astype(BF16))
            halves[1].append(jnp.where(low, pltpu.roll(va, NSA_HD, axis=1), vb).astype(BF16))
        pe_c = _dot(pe_ref[s], w1_ref[s])
        pe_const = pe_c[0:1, :CMP_HIDDEN] + pe_c[1:2, CMP_HIDDEN:]
        for g in range(KV_HEADS):
            lhs = jnp.concatenate(halves[g], axis=1)
            acc = _dot(lhs, w1_ref[s])
            hid = acc[:, :CMP_HIDDEN] + pltpu.roll(acc[:, CMP_HIDDEN:], j - 1, axis=0) + pe_const
            out = out + _dot(_gelu_tanh(hid).astype(BF16), w2_ref[s * KV_HEADS + g])
    je = j // nb
    for e in range(nb):
        o_ref[e] = out[e * je:(e + 1) * je, :]


def _compress_weights(pe, w1, w2):
    w1r = w1.reshape(2, CMP_BLOCK, NSA_HD, CMP_HIDDEN)
    wa = w1r[:, :CMP_STRIDE].reshape(2, 4, 4 * NSA_HD, CMP_HIDDEN)
    wb = w1r[:, CMP_STRIDE:].reshape(2, 4, 4 * NSA_HD, CMP_HIDDEN)
    w1p = jnp.concatenate([wa, wb], axis=-1).astype(BF16).reshape(2, CMP_STRIDE * NSA_HD, 2 * CMP_HIDDEN)
    pea = pe[:, :CMP_STRIDE].reshape(2, 1, CMP_STRIDE * NSA_HD)
    peb = pe[:, CMP_STRIDE:].reshape(2, 1, CMP_STRIDE * NSA_HD)
    pep = jnp.concatenate([pea, peb, jnp.zeros((2, 14, CMP_STRIDE * NSA_HD), F32)], axis=1).astype(BF16)
    w2p = jnp.zeros((2, KV_HEADS, CMP_HIDDEN, 2, KV_HEADS, NSA_HD), F32)
    for s in range(2):
        for g in range(KV_HEADS):
            w2p = w2p.at[s, g, :, s, g, :].set(w2[s])
    w2p = w2p.reshape(2 * KV_HEADS, CMP_HIDDEN, 2 * KV_W).astype(BF16)
    return pep, w1p, w2p


def compress(pages, page_index_maps, num_scalar_prefetch, prefetch, grid_n, npages, rows, cw, transposed=False,
             nb=1):
    pep, w1p, w2p = cw
    j = npages * rows // CMP_STRIDE
    const = lambda nd: (lambda *a: (0,) * nd)
    if transposed:
        page_specs = [pl.BlockSpec((1, 1, 2 * KV_W, rows), im) for im in page_index_maps]
        scratch = [pltpu.VMEM((npages * rows, KV_W), F32)] * 2
    else:
        page_specs = [pl.BlockSpec((1, rows, KV_W), im(s)) for im in page_index_maps for s in range(2)]
        scratch = []
    grid_spec = pltpu.PrefetchScalarGridSpec(
        num_scalar_prefetch=num_scalar_prefetch,
        grid=(grid_n,),
        in_specs=page_specs
        + [pl.BlockSpec(pep.shape, const(3)), pl.BlockSpec(w1p.shape, const(3)), pl.BlockSpec(w2p.shape, const(3))],
        out_specs=pl.BlockSpec((nb, j // nb, 2 * KV_W), lambda b, *a: (b, 0, 0)),
        scratch_shapes=scratch,
    )
    return pl.pallas_call(
        functools.partial(_compress_kernel, npages=npages, rows=rows, transposed=transposed, nb=nb),
        grid_spec=grid_spec,
        out_shape=jax.ShapeDtypeStruct((grid_n * nb, j // nb, 2 * KV_W), F32),
        compiler_params=_cparams(("parallel",)),
        name="compress",
    )(*prefetch, *([pages] * len(page_specs)), pep, w1p, w2p)


def _rel_bucket(dist):
    n = jnp.maximum(dist, 0)
    max_exact = NUM_BUCKETS // 2
    nf = jnp.maximum(n, 1).astype(F32)
    large = max_exact + (jnp.log(nf / max_exact) / math.log(REL_MAX_DIST / max_exact)
                         * (NUM_BUCKETS - max_exact)).astype(jnp.int32)
    large = jnp.minimum(large, NUM_BUCKETS - 1)
    return jnp.where(n < max_exact, n, large)


def _delta_bias(rel_bias, dist):
    by_dist = rel_bias[_rel_bucket(jnp.arange(REL_MAX_DIST))] - rel_bias[NUM_BUCKETS - 1]
    near = (dist >= 0) & (dist < REL_MAX_DIST)
    onehot = (jnp.clip(dist, 0, REL_MAX_DIST - 1)[..., None] == jnp.arange(REL_MAX_DIST)).astype(F32)
    val = jnp.einsum("...d,dh->...h", onehot, by_dist, precision=lax.Precision.HIGHEST)
    return jnp.where(near[..., None], val, 0.0)


def _lanes_ri(tab, g):
    rows = tab.shape[0]
    return tab[:, :, g * Q_PER_KV:(g + 1) * Q_PER_KV].transpose(0, 2, 1).reshape(rows, Q_PER_KV * QT)


def _prompt_tables(rel_bias):
    i = jnp.arange(QT)[None, :]
    j = jnp.arange(KT)[:, None]
    d_diag = i - j
    diag = jnp.where((d_diag >= 0)[..., None], _delta_bias(rel_bias, d_diag), NEG)
    sub = _delta_bias(rel_bias, i + KT - j)
    n2 = jnp.arange(NEAR_ROWS)[:, None]
    near = _delta_bias(rel_bias, i + (CMP_STRIDE * NC_PAD_FRONT - CMP_BLOCK + 1) - CMP_STRIDE * n2)
    stack = lambda tab: jnp.stack([_lanes_ri(tab, g) for g in range(KV_HEADS)]) * LOG2E
    return stack(sub), stack(diag), stack(near)


def _cover_t(n_sel, n_cmp, ncp):
    n = np.arange(ncp) - NC_PAD_FRONT
    c_start = n * CMP_STRIDE
    s_start = np.arange(n_sel)[:, None] * SEL_BLOCK
    cov = (c_start[None, :] < s_start + SEL_BLOCK) & (c_start[None, :] + CMP_BLOCK > s_start)
    cov &= ((n >= 0) & (n < n_cmp))[None, :]
    return jnp.asarray(cov, BF16)


def _nsa_prompt_kernel(q_ref, gt_ref, kc_ref, vct_ref, cov_ref, bct_ref, ksa_ref, vst_ref, kw_ref, vwt_ref,
                       tsub_ref, tdiag_ref, o_ref, s_scr, p_scr, q_scr, sa_scr, sb_scr, m_scr, l_scr, acc_scr,
                       ocmp_scr, osel_scr, ot_scr, *, nsel, ncp, ncv):
    blk = pl.program_id(1)
    t0 = blk * QT
    lanes4 = Q_PER_KV * QT
    q_t = (q_ref[0] * (NSA_HD ** -0.5 * LOG2E)).T

    def tile4(x):
        return jnp.concatenate([x] * Q_PER_KV, axis=1)

    def weighted_values(scores, m, values):
        probs = [jnp.exp2(s - m).astype(BF16) for s in scores]
        p_all = probs[0] if len(probs) == 1 else jnp.concatenate(probs, axis=0)
        v_all = values[0] if len(values) == 1 else jnp.concatenate(values, axis=1)
        return _dot(v_all, p_all)

    def tiles_max(scores, m):
        for s in scores:
            m = jnp.maximum(m, jnp.max(s, axis=0, keepdims=True))
        return m

    def normalise(acc, m):
        inv = jnp.where(m > 0.5 * NEG, 1.0 / acc[NSA_HD:NSA_HD + 1, :], 0.0)
        return acc[0:NSA_HD, :] * inv

    def update(g, scores, values):
        m_old = m_scr[g]
        m_new = tiles_max(scores, m_old)
        acc_scr[g] = jnp.exp2(m_old - m_new) * acc_scr[g] + weighted_values(scores, m_new, values)
        m_scr[g] = m_new

    row_k = lax.broadcasted_iota(jnp.int32, (KT, QT), 0)
    lane_q = lax.broadcasted_iota(jnp.int32, (KT, QT), 1)

    importance = []
    for g in range(KV_HEADS):
        gs = slice(g * NSA_HD, (g + 1) * NSA_HD)
        qg = jnp.concatenate([q_t[(g * Q_PER_KV + r) * NSA_HD:(g * Q_PER_KV + r + 1) * NSA_HD, :]
                              for r in range(Q_PER_KV)], axis=1)
        zero = jnp.zeros_like(qg)
        qpad = jnp.concatenate([qg, zero] if g == 0 else [zero, qg], axis=0).astype(BF16)
        q_scr[g, 0:KV_W, :] = qpad

        s_scr[...] = _dot(kc_ref[0], qpad)
        near0 = pl.multiple_of(blk * (QT // CMP_STRIDE), 8)
        s_scr[pl.ds(near0, NEAR_ROWS), :] = s_scr[pl.ds(near0, NEAR_ROWS), :] + bct_ref[g]
        n_all = blk * (QT // CMP_STRIDE) - (CMP_BLOCK // CMP_STRIDE)
        n_any = n_all + (QT - 1) // CMP_STRIDE + 1
        m_scr[g] = jnp.full((1, lanes4), NEG, F32)
        l_scr[g] = jnp.zeros((1, lanes4), F32)
        nchunk = ncp // KT
        chunk_kind = []
        for c in range(nchunk):
            cs = slice(c * KT, (c + 1) * KT)
            lo, hi = c * KT - NC_PAD_FRONT, (c + 1) * KT - NC_PAD_FRONT - 1
            skip = (lo > n_any) if lo < ncv else True
            full = (hi <= n_all) if (lo >= 0 and hi < ncv) else False
            chunk_kind.append((cs, skip, full))
            if skip is True:
                continue

            @pl.when(full)
            def _(cs=cs):
                m_scr[g] = jnp.maximum(m_scr[g], jnp.max(s_scr[cs, :], axis=0, keepdims=True))

            @pl.when(jnp.logical_not(full) & jnp.logical_not(skip))
            def _(cs=cs, lo=lo):
                n = row_k + lo
                ok = (n >= 0) & (n < ncv) & (CMP_STRIDE * n + (CMP_BLOCK - 1) <= t0 + lane_q)
                sc = s_scr[cs, :] + tile4(jnp.where(ok, 0.0, NEG))
                s_scr[cs, :] = sc
                m_scr[g] = jnp.maximum(m_scr[g], jnp.max(sc, axis=0, keepdims=True))

        mx = m_scr[g]
        for cs, skip, _ in chunk_kind:
            if skip is True:
                p_scr[cs, :] = jnp.zeros((KT, lanes4), BF16)
                continue

            @pl.when(jnp.logical_not(skip))
            def _(cs=cs):
                p = jnp.exp2(s_scr[cs, :] - mx)
                l_scr[g] = l_scr[g] + jnp.sum(p, axis=0, keepdims=True)
                p_scr[cs, :] = p.astype(BF16)

            @pl.when(skip)
            def _(cs=cs):
                p_scr[cs, :] = jnp.zeros((KT, lanes4), BF16)

        inv_c = jnp.where(mx > 0.5 * NEG, 1.0 / l_scr[g], 0.0)
        ocmp_scr[g] = _dot(vct_ref[0, gs, :], p_scr[...]) * inv_c
        imp4 = _dot(cov_ref[...], p_scr[...]) * inv_c
        imp = imp4[:, 0:QT]
        for r in range(1, Q_PER_KV):
            imp = imp + imp4[:, r * QT:(r + 1) * QT]
        importance.append(imp)

    def select(rows):
        jb = lax.broadcasted_iota(jnp.int32, (rows, QT), 0)
        tq = t0 + lax.broadcasted_iota(jnp.int32, (rows, QT), 1)
        cur = tq // SEL_BLOCK
        forced = (jb == 0) | (jb == cur) | (jb == cur - 1)
        future = jb * SEL_BLOCK > tq
        score = [jnp.where(forced, FORCE, jnp.where(future, -FORCE, imp[0:rows, :])) for imp in importance]
        chosen_any = [jnp.zeros((rows, QT), jnp.bool_) for _ in range(KV_HEADS)]
        for _ in range(min(N_SELECT, rows)):
            for g in range(KV_HEADS):
                best = jnp.max(score[g], axis=0, keepdims=True)
                first = jnp.min(jnp.where(score[g] == best, jb, rows), axis=0, keepdims=True)
                chosen = jb == first
                chosen_any[g] = chosen_any[g] | chosen
                score[g] = jnp.where(chosen, REMOVED, score[g])
        for g in range(KV_HEADS):
            q_scr[g, KV_W:KV_W + rows, :] = tile4(jnp.where(chosen_any[g], 0.0, NEG)).astype(BF16)
            if rows < LANES:
                q_scr[g, KV_W + rows:, :] = jnp.full((LANES - rows, lanes4), NEG, BF16)

    last_block = (t0 + QT - 1) // SEL_BLOCK
    sizes = [r for r in (32, 64) if r < nsel] + [nsel]
    lower = 0
    for r in sizes:
        @pl.when((last_block >= lower) & ((last_block < r) | (r == nsel)))
        def _(r=r):
            select(r)
        lower = r

    def values_of(ref, kt, g):
        return ref[0, kt, g * V_ROWS:(g + 1) * V_ROWS, :]

    def qk_pair(k, dst):
        for g in range(KV_HEADS):
            dst[g, 0:KT, :] = _dot(ksa_ref[0, 2 * k], q_scr[g])
            dst[g, KT:2 * KT, :] = _dot(ksa_ref[0, 2 * k + 1], q_scr[g])

    def softmax_pair(k, src):
        for g in range(KV_HEADS):
            update(g, [src[g, 0:KT, :], src[g, KT:2 * KT, :]],
                   [values_of(vst_ref, 2 * k, g), values_of(vst_ref, 2 * k + 1, g)])

    m_scr[...] = jnp.full(m_scr.shape, NEG, F32)
    acc_scr[...] = jnp.zeros(acc_scr.shape, F32)
    n_far = jnp.maximum(blk - 1, 0)
    npairs = n_far // 2

    @pl.when(npairs > 0)
    def _():
        qk_pair(0, sa_scr)

    def two_pairs(k0):
        qk_pair(jnp.minimum(k0 + 1, npairs - 1), sb_scr)
        softmax_pair(k0, sa_scr)
        qk_pair(jnp.minimum(k0 + 2, npairs - 1), sa_scr)
        softmax_pair(k0 + 1, sb_scr)

    def far_body4(j, carry):
        two_pairs(4 * j)
        two_pairs(4 * j + 2)
        return carry

    def far_body2(j, carry):
        two_pairs(4 * (npairs // 4) + 2 * j)
        return carry

    lax.fori_loop(0, npairs // 4, far_body4, 0)
    lax.fori_loop(0, (npairs % 4) // 2, far_body2, 0)

    @pl.when(npairs % 2 == 1)
    def _():
        softmax_pair(npairs - 1, sa_scr)

    def gate(valid):
        return jnp.where(valid, 0.0, NEG).astype(F32)

    anti = tile4(jnp.where(row_k > lane_q, 0.0, NEG))
    sel_tiles = ((blk - 2, lambda g: gate(n_far % 2 == 1)),
                 (blk - 1, lambda g: tsub_ref[g] + gate(blk >= 1)),
                 (blk, lambda g: tdiag_ref[g]))
    win_tiles = ((blk - 4, lambda g: anti + gate(blk >= 4)),
                 (blk - 3, lambda g: gate(blk >= 3)),
                 (blk - 2, lambda g: gate(blk >= 2)),
                 (blk - 1, lambda g: tsub_ref[g] + gate(blk >= 1)),
                 (blk, lambda g: tdiag_ref[g]))
    sel_scores, win_scores = [], []
    for g in range(KV_HEADS):
        sel_scores.append([_dot(ksa_ref[0, jnp.maximum(kt, 0)], q_scr[g]) + add(g) for kt, add in sel_tiles])
        win_scores.append([_dot(kw_ref[0, jnp.maximum(kt, 0)], q_scr[g, 0:KV_W, :]) + add(g)
                           for kt, add in win_tiles])
    o_win = []
    for g in range(KV_HEADS):
        update(g, sel_scores[g], [values_of(vst_ref, jnp.maximum(kt, 0), g) for kt, _ in sel_tiles])
        osel_scr[g] = normalise(acc_scr[g], m_scr[g])
        m_win = tiles_max(win_scores[g], jnp.full((1, lanes4), NEG, F32))
        acc_win = weighted_values(win_scores[g], m_win,
                                  [values_of(vwt_ref, jnp.maximum(kt, 0), g) for kt, _ in win_tiles])
        o_win.append(normalise(acc_win, m_win))

    gl = gt_ref[0, 0]
    for g in range(KV_HEADS):
        o_g = (_sigmoid(gl[3 * g:3 * g + 1, :]) * ocmp_scr[g] + _sigmoid(gl[3 * g + 1:3 * g + 2, :]) * osel_scr[g]
               + _sigmoid(gl[3 * g + 2:3 * g + 3, :]) * o_win[g])
        for r in range(Q_PER_KV):
            h = g * Q_PER_KV + r
            ot_scr[h * NSA_HD:(h + 1) * NSA_HD, :] = o_g[:, r * QT:(r + 1) * QT]

    o_ref[0] = ot_scr[...].T.astype(o_ref.dtype)


def nsa_prompt(proj, kv_bf, ckv, rel_tabs):
    bsz, seq, _ = proj.shape
    nq = seq // QT
    n_cmp = (seq - CMP_BLOCK) // CMP_STRIDE + 1
    nsel = seq // SEL_BLOCK
    ncp = -(-(NC_PAD_FRONT + seq // CMP_STRIDE) // KT) * KT
    tsub, tdiag, bct = rel_tabs

    def tiles(cols):
        c0 = cols - C_SKV
        return kv_bf[:, :, c0:c0 + KV_W].reshape(bsz, nq, KT, KV_W)

    assert nsel <= LANES
    blk_of_key = np.arange(seq) // SEL_BLOCK
    onehot = jnp.asarray(blk_of_key[:, None] == np.arange(LANES)[None, :], BF16).reshape(nq, KT, LANES)
    ksa = jnp.concatenate([tiles(C_SKV), jnp.broadcast_to(onehot, (bsz, nq, KT, LANES))], axis=-1)

    def values_t(cols):
        vt = tiles(cols).reshape(bsz, nq, KT, KV_HEADS, NSA_HD).transpose(0, 1, 3, 4, 2)
        ones = jnp.ones((bsz, nq, KV_HEADS, 1, KT), BF16)
        zeros = jnp.zeros((bsz, nq, KV_HEADS, V_ROWS - NSA_HD - 1, KT), BF16)
        return jnp.concatenate([vt, ones, zeros], axis=3).reshape(bsz, nq, KV_HEADS * V_ROWS, KT)

    vst = values_t(C_SKV + KV_W)
    kw = tiles(C_WKV)
    vwt = values_t(C_WKV + KV_W)
    back = ncp - NC_PAD_FRONT - ckv.shape[1]
    kc = jnp.pad(ckv[:, :, :KV_W], ((0, 0), (NC_PAD_FRONT, back), (0, 0))).astype(BF16)
    vct = jnp.pad(ckv[:, :, KV_W:], ((0, 0), (NC_PAD_FRONT, back), (0, 0))).astype(BF16).transpose(0, 2, 1)
    cov = _cover_t(nsel, n_cmp, ncp)
    ng = proj[:, :, C_NG:C_NG + 3 * NSA_HEADS].reshape(bsz, nq, QT, KV_HEADS, Q_PER_KV, 3)
    gt = ng.transpose(0, 1, 3, 5, 4, 2).reshape(bsz, nq, KV_HEADS * 3, Q_PER_KV * QT)
    gt = jnp.pad(gt, ((0, 0), (0, 0), (0, 2), (0, 0)))

    whole = lambda a: pl.BlockSpec((1,) + a.shape[1:], lambda b, i: (b,) + (0,) * (a.ndim - 1))
    const = lambda a: pl.BlockSpec(a.shape, lambda b, i: (0,) * a.ndim)
    lanes4 = Q_PER_KV * QT
    return pl.pallas_call(
        functools.partial(_nsa_prompt_kernel, nsel=nsel, ncp=ncp, ncv=n_cmp),
        grid=(bsz, nq),
        in_specs=[pl.BlockSpec((1, QT, NSA_W), lambda b, i: (b, i, C_NQ // NSA_W)),
                  pl.BlockSpec((1, 1, 8, lanes4), lambda b, i: (b, i, 0, 0)),
                  whole(kc), whole(vct), const(cov), const(bct),
                  whole(ksa), whole(vst), whole(kw), whole(vwt), const(tsub), const(tdiag)],
        out_specs=pl.BlockSpec((1, QT, NSA_W), lambda b, i: (b, i, 0)),
        out_shape=jax.ShapeDtypeStruct((bsz, seq, NSA_W), BF16),
        scratch_shapes=[pltpu.VMEM((ncp, lanes4), F32), pltpu.VMEM((ncp, lanes4), BF16),
                        pltpu.VMEM((KV_HEADS, KV_W + LANES, lanes4), BF16),
                        pltpu.VMEM((KV_HEADS, 2 * KT, lanes4), F32), pltpu.VMEM((KV_HEADS, 2 * KT, lanes4), F32),
                        pltpu.VMEM((KV_HEADS, 1, lanes4), F32), pltpu.VMEM((KV_HEADS, 1, lanes4), F32),
                        pltpu.VMEM((KV_HEADS, V_ROWS, lanes4), F32), pltpu.VMEM((KV_HEADS, NSA_HD, lanes4), F32),
                        pltpu.VMEM((KV_HEADS, NSA_HD, lanes4), F32), pltpu.VMEM((NSA_W, QT), F32)],
        compiler_params=_cparams(("parallel", "arbitrary")),
        name="nsa_prompt",
    )(proj, gt, kc, vct, cov, bct, ksa, vst, kw, vwt, tsub, tdiag)


S_ROWS = Q_PER_KV * KV_HEADS * 4
NEW_PAD = 16
SAMPLE_NB = 4


def _sample_tables(rel_bias, past, t_new, n_cmp, win_len):
    row = jnp.arange(S_ROWS)
    i = (row % t_new)[:, None]
    g = (row // t_new) % KV_HEADS
    r = row // (t_new * KV_HEADS)
    head = g * Q_PER_KV + r

    def pick(tab):
        return jnp.take_along_axis(tab, head[:, None, None], axis=2)[..., 0]

    n = jnp.arange(LANES)[None, :]
    bcs = jnp.where(n < n_cmp, pick(_delta_bias(rel_bias, past + i - (CMP_STRIDE * n + CMP_BLOCK - 1))), NEG)
    jj = jnp.arange(LANES)[None, :]
    last = pick(_delta_bias(rel_bias, LANES + i - jj))
    w_old = jnp.where(jj > i, 0.0, NEG)
    j2 = jnp.arange(NEW_PAD)[None, :]
    d_new = i - j2
    new = jnp.where((j2 < t_new) & (d_new >= 0), pick(_delta_bias(rel_bias, d_new)), NEG)
    return bcs, jnp.stack([last, w_old]), new


def _nsa_sample_kernel(*refs, nb, npages, past, t_new):
    pt_ref, q_ref, g_ref, kc_ref = refs[:4]
    page_refs = refs[4:4 + nb * npages]
    win_ref, news_ref, neww_ref, bcs_ref, tab_ref, tnew_ref, cov_ref, o_ref, s_scr = refs[4 + nb * npages:]
    del pt_ref
    elems = range(nb)
    qb = [(q_ref[e] * (NSA_HD ** -0.5)).astype(BF16) for e in elems]

    def softmax_pv(tiles_of):
        mx = []
        for e in elems:
            m = jnp.full((S_ROWS, 1), NEG, F32)
            for c, (score_fn, _, add_fn, width) in enumerate(tiles_of(e)):
                s = score_fn()
                if add_fn is not None:
                    s = s + add_fn()
                s_scr[e, :, c * LANES:c * LANES + width] = s
                m = jnp.maximum(m, s.max(axis=1, keepdims=True))
            mx.append(m)
        out = []
        for e in elems:
            lsum = jnp.zeros((S_ROWS, 1), F32)
            acc = jnp.zeros((S_ROWS, KV_W), F32)
            for c, (_, pv_fn, _, width) in enumerate(tiles_of(e)):
                p = jnp.exp(s_scr[e, :, c * LANES:c * LANES + width] - mx[e])
                lsum = lsum + p.sum(axis=1, keepdims=True)
                acc = acc + pv_fn(p.astype(BF16))
            out.append(acc / lsum)
        return out

    def cached_tile(e, tile_ref, lanes, add_fn):
        return (lambda: _dot(qb[e], tile_ref[:KV_W, lanes].astype(BF16)),
                lambda p: _dot_nt(p, tile_ref[KV_W:, lanes].astype(BF16)), add_fn, LANES)

    def new_tile(e, ref, add_fn):
        return (lambda: _dot_nt(qb[e], ref[e, :, :KV_W].astype(BF16)),
                lambda p: _dot(p, ref[e, :, KV_W:].astype(BF16)), add_fn, NEW_PAD)

    rows_gi = KV_HEADS * t_new
    o_cmp, imp = [], []
    for e in elems:
        kc = kc_ref[e]
        s_c = _dot_nt(qb[e], kc[:, :KV_W].astype(BF16)) + bcs_ref[...]
        p_c = jnp.exp(s_c - s_c.max(axis=1, keepdims=True))
        p_c = (p_c / p_c.sum(axis=1, keepdims=True)).astype(BF16)
        o_cmp.append(_dot(p_c, kc[:, KV_W:].astype(BF16)))
        imp_r = _dot(p_c, cov_ref[...])
        tot = imp_r
        for r in range(1, Q_PER_KV):
            tot = tot + pltpu.roll(imp_r, r * rows_gi, axis=0)
        imp.append(tot)

    jb = lax.broadcasted_iota(jnp.int32, (S_ROWS, LANES), 1)
    qpos = past + lax.broadcasted_iota(jnp.int32, (S_ROWS, LANES), 0) % t_new
    cur = qpos // SEL_BLOCK
    forced = (jb == 0) | (jb == cur) | (jb == cur - 1)
    future = jb * SEL_BLOCK > qpos
    score = [jnp.where(forced, FORCE, jnp.where(future, -FORCE, imp[e])) for e in elems]
    chosen_any = [jnp.zeros((S_ROWS, LANES), jnp.bool_) for _ in elems]
    for _ in range(N_SELECT):
        for e in elems:
            best = jnp.max(score[e], axis=1, keepdims=True)
            first = jnp.min(jnp.where(score[e] == best, jb, LANES), axis=1, keepdims=True)
            chosen = jb == first
            chosen_any[e] = chosen_any[e] | chosen
            score[e] = jnp.where(chosen, REMOVED, score[e])
    sel = [jnp.where(chosen_any[e], 1.0, 0.0).astype(BF16) for e in elems]

    def block_mask(e, j0, nkeys=LANES):
        jrow = lax.broadcasted_iota(jnp.int32, (LANES, nkeys), 0)
        kcol = lax.broadcasted_iota(jnp.int32, (LANES, nkeys), 1)
        expand = jnp.where(jrow == j0 + kcol // SEL_BLOCK, 1.0, 0.0).astype(BF16)
        return jnp.where(_dot(sel[e], expand) > 0.5, 0.0, NEG)

    jn = past // SEL_BLOCK

    def sel_tiles(e):
        tiles = []
        for p in range(npages):
            if p == npages - 1:
                add = lambda p=p: block_mask(e, 2 * p) + tab_ref[0]
            else:
                add = lambda p=p: block_mask(e, 2 * p)
            tiles.append(cached_tile(e, page_refs[e * npages + p].at[0, 0], slice(None), add))
        tiles.append(new_tile(e, news_ref, lambda: tnew_ref[...] + block_mask(e, jn, NEW_PAD)))
        return tiles

    o_sel = softmax_pv(sel_tiles)

    nwin = win_ref.shape[3] // LANES

    def win_tiles(e):
        tiles = []
        for c in range(nwin):
            if c == 0:
                add = lambda: tab_ref[1]
            elif c == nwin - 1:
                add = lambda: tab_ref[0]
            else:
                add = None
            tiles.append(cached_tile(e, win_ref.at[0, e], slice(c * LANES, (c + 1) * LANES), add))
        tiles.append(new_tile(e, neww_ref, lambda: tnew_ref[...]))
        return tiles

    o_win = softmax_pv(win_tiles)

    for e in elems:
        gl = g_ref[e]
        o_ref[e] = (_sigmoid(gl[:, 0:1]) * o_cmp[e] + _sigmoid(gl[:, 1:2]) * o_sel[e]
                    + _sigmoid(gl[:, 2:3]) * o_win[e])


def nsa_sample(proj, ckv, pool_s, win_buf, layer, page_table, rel_bias):
    bsz, t_new, _ = proj.shape
    npages = page_table.shape[1]
    past = npages * PAGE
    n_cmp = (past + t_new - CMP_BLOCK) // CMP_STRIDE + 1
    n_sel = -(-(past + t_new) // SEL_BLOCK)
    assert n_cmp <= LANES and n_sel <= LANES and win_buf.shape[3] == WINDOW and t_new == 4
    assert (past + t_new - 1) // SEL_BLOCK == past // SEL_BLOCK
    bcs, tab, tnew = _sample_tables(rel_bias, past, t_new, n_cmp, win_buf.shape[3])
    n = np.arange(LANES)
    jsel = np.arange(LANES)
    cov = ((n[:, None] * CMP_STRIDE < jsel[None, :] * SEL_BLOCK + SEL_BLOCK)
           & (n[:, None] * CMP_STRIDE + CMP_BLOCK > jsel[None, :] * SEL_BLOCK)
           & (n[:, None] < n_cmp) & (jsel[None, :] < n_sel))
    cov = jnp.asarray(cov, BF16)

    eye = jnp.eye(KV_HEADS, dtype=F32)
    q5 = proj[:, :, C_NQ:C_NQ + NSA_W].reshape(bsz, t_new, KV_HEADS, Q_PER_KV, NSA_HD).transpose(0, 3, 2, 1, 4)
    qpad = (q5[:, :, :, :, None, :] * eye[None, None, :, None, :, None]).reshape(bsz, S_ROWS, KV_W)
    ng = proj[:, :, C_NG:C_NG + 3 * NSA_HEADS].reshape(bsz, t_new, KV_HEADS, Q_PER_KV, 3).transpose(0, 3, 2, 1, 4)
    gl = jnp.pad(ng.reshape(bsz, S_ROWS, 3), ((0, 0), (0, 0), (0, LANES - 3)))
    pad_new = lambda c: jnp.pad(proj[:, :, c:c + 2 * KV_W], ((0, 0), (0, NEW_PAD - t_new), (0, 0)))

    nb = SAMPLE_NB
    assert bsz % nb == 0
    per_b = lambda a: pl.BlockSpec((nb,) + a.shape[1:], lambda b, pt: (b,) + (0,) * (a.ndim - 1))
    const = lambda a: pl.BlockSpec(a.shape, lambda b, pt: (0,) * a.ndim)
    new_s, new_w = pad_new(C_SKV), pad_new(C_WKV)
    grid_spec = pltpu.PrefetchScalarGridSpec(
        num_scalar_prefetch=1,
        grid=(bsz // nb,),
        in_specs=[per_b(qpad), per_b(gl), per_b(ckv)]
        + [pl.BlockSpec((1, 1, 2 * KV_W, PAGE), (lambda b, pt, e=e, p=p: (layer, pt[b * nb + e, p], 0, 0)))
           for e in range(nb) for p in range(npages)]
        + [pl.BlockSpec((1, nb) + win_buf.shape[2:], lambda b, pt: (layer, b, 0, 0)),
           per_b(new_s), per_b(new_w), const(bcs), const(tab), const(tnew), const(cov)],
        out_specs=pl.BlockSpec((nb, S_ROWS, KV_W), lambda b, pt: (b, 0, 0)),
        scratch_shapes=[pltpu.VMEM((nb, S_ROWS, (npages + 1) * LANES), F32)],
    )
    out = pl.pallas_call(
        functools.partial(_nsa_sample_kernel, nb=nb, npages=npages, past=past, t_new=t_new),
        grid_spec=grid_spec,
        out_shape=jax.ShapeDtypeStruct((bsz, S_ROWS, KV_W), F32),
        compiler_params=_cparams(("parallel",)),
        name="nsa_sample",
    )(page_table, qpad, gl, ckv, *([pool_s] * (nb * npages)), win_buf, new_s, new_w, bcs, tab, tnew, cov)
    o6 = out.reshape(bsz, Q_PER_KV, KV_HEADS, t_new, KV_HEADS, NSA_HD)
    o5 = jnp.stack([o6[:, :, g, :, g, :] for g in range(KV_HEADS)], axis=2)
    return o5.transpose(0, 3, 2, 1, 4).reshape(bsz, t_new, NSA_W)


WIN_NB = 8


def _win_shift_kernel(win_ref, new_ref, o_ref, *, t_new):
    w = win_ref.shape[-1]
    lane = lax.broadcasted_iota(jnp.int32, (win_ref.shape[-2], LANES), 1)
    for e in range(win_ref.shape[1]):
        y = pltpu.roll(win_ref[0, e], w - t_new, axis=1)
        o_ref[0, e, :, :w - LANES] = y[:, :w - LANES]
        o_ref[0, e, :, w - LANES:] = jnp.where(lane < LANES - t_new, y[:, w - LANES:], new_ref[0, e])


def window_cache_update(win_t, new_rows):
    depth, bsz, feat, w = win_t.shape
    t_new = new_rows.shape[2]
    assert t_new <= LANES <= w and bsz % WIN_NB == 0
    new_t = jnp.pad(new_rows.transpose(0, 1, 3, 2), ((0, 0), (0, 0), (0, 0), (LANES - t_new, 0)))
    return pl.pallas_call(
        functools.partial(_win_shift_kernel, t_new=t_new),
        grid=(depth, bsz // WIN_NB),
        in_specs=[pl.BlockSpec((1, WIN_NB, feat, w), lambda l, b: (l, b, 0, 0)),
                  pl.BlockSpec((1, WIN_NB, feat, LANES), lambda l, b: (l, b, 0, 0))],
        out_specs=pl.BlockSpec((1, WIN_NB, feat, w), lambda l, b: (l, b, 0, 0)),
        out_shape=jax.ShapeDtypeStruct(win_t.shape, win_t.dtype),
        compiler_params=_cparams(("parallel", "parallel")),
        name="window_cache_update",
    )(win_t, new_t)


def _in_proj_weight(w):
    g0 = 4 * ML_W
    g1 = g0 + 2 * ML_HEADS
    n_kv = C_GATE
    assert w.shape[1] == n_kv + 2 * ML_HEADS + 3 * NSA_HEADS
    tail = w.shape[1] - (g1 - g0) - n_kv
    pieces = [w[:, :g0], w[:, g1:g1 + n_kv - g0], w[:, g0:g1], w[:, w.shape[1] - tail:],
              jnp.zeros((w.shape[0], N_IN_PAD - w.shape[1]), w.dtype)]
    return jnp.concatenate(pieces, axis=1).astype(BF16)


def _layer(x, mods, rows_per_group, lw, tm, tm_ffn, mixer):
    sh1, sc1, g1, sh2, sc2, g2 = mods
    w_in_bf, w_out_bf, w_up_bf, w_down_bf, ln_g, ln_b = lw
    proj, kv_bf = in_proj(x, sc1, sh1, w_in_bf, tm, rows_per_group // tm)
    h_ml, o_nsa, extras = mixer(proj, kv_bf)
    x1 = out_proj_ln(x, h_ml, o_nsa, w_out_bf, g1, ln_g[0], ln_b[0], tm, rows_per_group // tm)
    x2 = ffn_ln(x1, sc2, sh2, g2, w_up_bf, w_down_bf, ln_g[1], ln_b[1], tm_ffn, rows_per_group // tm_ffn)
    return x2, proj, extras


def kernel(x_prompt, x_sample, cache_cmp_kv, cache_slc_kv, cache_win_kv, state_mlstm_C, state_mlstm_n,
           state_mlstm_m, page_table, c_prompt, c_sample, w_ada, b_ada, w_in, b_gate, ml_norm_g, cmp_pe,
           cmp_w1, cmp_w2, rel_bias, w_out, ln_g, ln_b, w_up, w_down):
    bsz, seq, d = x_prompt.shape
    bs, ts, _ = x_sample.shape
    depth = w_in.shape[0]
    n_phys = cache_cmp_kv.shape[1]
    npages = page_table.shape[1]
    win_len = cache_win_kv.shape[2]
    tm = 512
    tm_ffn = 1024
    t_ml = 256
    t_pad = 16

    rows_last = lambda a: a.transpose(0, 1, 3, 4, 5, 2).reshape(a.shape[0], a.shape[1], 2 * KV_W, a.shape[2])
    pool_c, pool_s, win_t = rows_last(cache_cmp_kv), rows_last(cache_slc_kv), rows_last(cache_win_kv)

    prompt_tabs = _prompt_tables(rel_bias)
    nc_rows = bsz + bs
    c_all = jnp.pad(jnp.concatenate([c_prompt, c_sample], axis=0), ((0, -nc_rows % 8), (0, 0)))

    xp = x_prompt.reshape(bsz * seq, d)
    xs = x_sample.reshape(bs * ts, d)
    outs = {k: [] for k in ("cmp_p", "cmp_s", "slc_p", "slc_s", "win_p", "win_s",
                            "C_p", "C_s", "n_p", "n_s", "m_p", "m_s")}
    kv_shape = lambda b, t: (b, t, 2, KV_HEADS, NSA_HD)

    for l in range(depth):
        ada = ada_mod(c_all, w_ada[l].astype(BF16), b_ada[l]).reshape(c_all.shape[0], 6, d)
        mods_p = [ada[:bsz, k][:, None, :] for k in range(6)]
        mods_s = [jnp.repeat(ada[bsz:nc_rows, k], ts, axis=0)[None] for k in range(6)]
        w_in_bf = _in_proj_weight(w_in[l])
        lw = (w_in_bf, w_out[l].astype(BF16), w_up[l].astype(BF16), w_down[l].astype(BF16), ln_g[l], ln_b[l])
        cw = _compress_weights(cmp_pe[l], cmp_w1[l], cmp_w2[l])

        def prompt_mixer(proj, kv_bf):
            proj = proj.reshape(bsz, seq, N_IN_PAD)
            h_ml, c1, n1, m1 = mlstm(proj, b_gate[l], ml_norm_g[l],
                                     jnp.zeros((bsz, ML_HEADS, ML_HD, ML_HD), F32),
                                     jnp.zeros((bsz, ML_HEADS, ML_HD), F32), jnp.zeros((bsz, ML_HEADS), F32),
                                     nb=bsz, t=t_ml, valid=t_ml)
            ckv = compress(proj, [lambda s: (lambda b: (b, 0, C_CKV // KV_W + s))], 0, (), bsz, 1, seq, cw)
            o_nsa = nsa_prompt(proj, kv_bf.reshape(bsz, seq, 4 * KV_W), ckv, prompt_tabs)
            return h_ml.reshape(bsz * seq, ML_W), o_nsa.reshape(bsz * seq, NSA_W), (c1, n1, m1)

        def sample_mixer(proj, kv_bf):
            del kv_bf
            proj = proj.reshape(bs, ts, N_IN_PAD)
            proj_pad = jnp.pad(proj, ((0, 0), (0, t_pad - ts), (0, 0)))
            h_ml, c1, n1, m1 = mlstm(proj_pad, b_gate[l], ml_norm_g[l], state_mlstm_C, state_mlstm_n,
                                     state_mlstm_m, nb=8, t=t_pad, valid=ts, layer=l)
            pages = [(lambda b, pt, e=e, p=p: (l, pt[b * SAMPLE_NB + e, p], 0, 0))
                     for e in range(SAMPLE_NB) for p in range(npages)]
            ckv = compress(pool_c, pages, 1, (page_table,), bs // SAMPLE_NB, SAMPLE_NB * npages, PAGE, cw,
                           transposed=True, nb=SAMPLE_NB)
            o_nsa = nsa_sample(proj, ckv, pool_s, win_t, l, page_table, rel_bias)
            return (h_ml[:, :ts].reshape(bs * ts, ML_W), o_nsa.reshape(bs * ts, NSA_W).astype(BF16),
                    (c1, n1, m1))

        xp, proj_p, st_p = _layer(xp, mods_p, seq, lw, tm, tm_ffn, prompt_mixer)
        xs, proj_s, st_s = _layer(xs, mods_s, bs * ts, lw, bs * ts, bs * ts, sample_mixer)

        proj_p = proj_p.reshape(bsz, seq, N_IN_PAD)
        proj_s = proj_s.reshape(bs, ts, N_IN_PAD)
        rows = lambda proj, c: proj[:, :, c:c + 2 * KV_W]
        win = min(WINDOW, seq)
        outs["cmp_p"].append(rows(proj_p, C_CKV).reshape(kv_shape(bsz, seq)))
        outs["slc_p"].append(rows(proj_p, C_SKV).reshape(kv_shape(bsz, seq)))
        outs["win_p"].append(rows(proj_p, C_WKV)[:, seq - win:].reshape(kv_shape(bsz, win)))
        outs["cmp_s"].append(rows(proj_s, C_CKV).reshape(kv_shape(bs, ts)))
        outs["slc_s"].append(rows(proj_s, C_SKV).reshape(kv_shape(bs, ts)))
        outs["win_s"].append(rows(proj_s, C_WKV).reshape(kv_shape(bs, ts)))
        for tag, st in (("p", st_p), ("s", st_s)):
            outs["C_" + tag].append(st[0])
            outs["n_" + tag].append(st[1])
            outs["m_" + tag].append(st[2])

    stk = lambda k: jnp.stack(outs[k])
    assert ts <= win_len
    win_s = window_cache_update(win_t, stk("win_s").reshape(depth, bs, ts, 2 * KV_W))
    win_s = win_s.reshape(depth, bs, 2, KV_HEADS, NSA_HD, win_len).transpose(0, 1, 5, 2, 3, 4)
    return (xp.reshape(bsz, seq, d), xs.reshape(bs, ts, d),
            stk("cmp_p"), stk("cmp_s"), stk("slc_p"), stk("slc_s"), stk("win_p"), win_s,
            stk("C_p"), stk("C_s"), stk("n_p"), stk("n_s"), stk("m_p"), stk("m_s"))
```
